```python
import math
import jax
import jax.numpy as jnp
from jax import lax
import numpy as np

D_MODEL = 1024
BATCH = 2
SEQ = 16384
DEPTH = 2

GRID_W = 64
CTX_LEN = 256
HEAD_DIM = 64
N_BRANCH = 4
BRANCH_W = D_MODEL // N_BRANCH
BLOCK = 128
WINDOW = 128
ROPE_THETA = 10000.0
EPS = 1e-6
D_FF = 2816
N_SUB = 3
POOL_WINDOWS = (2, 4, 8, 16)
POOL_GROUP = BRANCH_W // len(POOL_WINDOWS)
WIN_HEADS = BRANCH_W // HEAD_DIM
WIN_KV_HEADS = WIN_HEADS // 2
GLB_HEADS = BRANCH_W // HEAD_DIM
GLB_KV_HEADS = GLB_HEADS // 2
SSM_GROUP = 16
SSM_GROUPS = BRANCH_W // SSM_GROUP
SSM_STATE = 64

O_KB = 0
O_VB = O_KB + WIN_KV_HEADS * HEAD_DIM
O_UC = O_VB + WIN_KV_HEADS * HEAD_DIM
O_KD = O_UC + BRANCH_W
O_VD = O_KD + GLB_KV_HEADS * HEAD_DIM
CTX_COLS = O_VD + GLB_KV_HEADS * HEAD_DIM
O_QB = CTX_COLS
O_QD = O_QB + WIN_HEADS * HEAD_DIM
O_XA = O_QD + GLB_HEADS * HEAD_DIM
O_GATE = O_XA + BRANCH_W
IN_W = O_GATE + N_BRANCH * D_MODEL

kernel_name = 'hybrid_gated_pool_swa_s5_gqa_flow'


def rms_norm(x, g):
    x32 = x.astype(jnp.float32)
    y = x32 * lax.rsqrt(jnp.mean(x32 * x32, axis=-1, keepdims=True) + EPS)
    return (y * g.astype(jnp.float32)).astype(x.dtype)


def modulate(h, shift, scale):
    return h * (1.0 + scale) + shift


def swiglu(h, w_in, w_out):
    gate, up = jnp.split(h @ w_in, 2, axis=-1)
    return (jax.nn.silu(gate) * up) @ w_out


def split_heads(z, n_heads):
    return z.reshape(z.shape[:-1] + (n_heads, HEAD_DIM))


def rope_tables(rows):
    n_freq = HEAD_DIM // 4
    row = jnp.repeat(jnp.arange(rows), GRID_W)
    col = jnp.tile(jnp.arange(GRID_W), rows)
    inv = ROPE_THETA ** (-jnp.arange(n_freq, dtype=jnp.float32) / n_freq)
    ang = jnp.stack([row, col], axis=-1).astype(jnp.float32)[..., None] * inv
    return jnp.cos(ang), jnp.sin(ang)


def apply_rope(x, cos, sin):
    xs = x.astype(jnp.float32).reshape(x.shape[:-1] + (2, 2, HEAD_DIM // 4))
    x1, x2 = xs[..., 0, :], xs[..., 1, :]
    c = cos[None, :, None]
    s = sin[None, :, None]
    out = jnp.stack([x1 * c - x2 * s, x2 * c + x1 * s], axis=-2)
    return out.reshape(x.shape).astype(x.dtype)


def multi_scale_pool(xa, w_pool, pool_scale):
    b, n, _ = xa.shape
    xg = xa.astype(jnp.float32).reshape(b, n, len(POOL_WINDOWS), POOL_GROUP)
    csum = jnp.concatenate([jnp.zeros((b, 1, len(POOL_WINDOWS), POOL_GROUP), jnp.float32),
                            jnp.cumsum(xg, axis=1)], axis=1)
    t = jnp.arange(n)
    pooled = []
    for gi, w in enumerate(POOL_WINDOWS):
        lo = jnp.clip(t - w // 2, 0, n)
        hi = jnp.clip(t - w // 2 + w, 0, n)
        cg = csum[:, :, gi]
        pooled.append((cg[:, hi] - cg[:, lo]) / (hi - lo).astype(jnp.float32)[:, None])
    diff = jnp.stack(pooled, axis=2) - xg
    y = jnp.einsum('bngc,gcd->bngd', diff, w_pool.astype(jnp.float32))
    return (y.reshape(b, n, BRANCH_W) * pool_scale.astype(jnp.float32)).astype(xa.dtype)


def window_attention(q, k, v, kc, vc, sink):
    b, n, h, dh = q.shape
    kvh = k.shape[2]
    g = h // kvh
    nb = n // BLOCK
    n_ctx = kc.shape[1]
    qb = q.reshape(b, nb, BLOCK, kvh, g, dh)
    pad = ((0, 0), (BLOCK, BLOCK), (0, 0), (0, 0))

    def band(z):
        zp = jnp.pad(z, pad).reshape(b, nb + 2, BLOCK, kvh, dh)
        return jnp.concatenate([zp[:, :-2], zp[:, 1:-1], zp[:, 2:]], axis=2)

    kw, vw = band(k), band(v)
    scale = dh ** -0.5
    s_lat = jnp.einsum('bnqhgd,bnkhd->bnhgqk', qb, kw, preferred_element_type=jnp.float32) * scale
    rel = (jnp.arange(3 * BLOCK)[None, :] - BLOCK) - jnp.arange(BLOCK)[:, None]
    kpos = jnp.arange(nb)[:, None] * BLOCK - BLOCK + jnp.arange(3 * BLOCK)[None, :]
    mask = (jnp.abs(rel) <= WINDOW)[None] & ((kpos >= 0) & (kpos < n))[:, None, :]
    s_lat = jnp.where(mask[None, :, None, None], s_lat, -jnp.inf)
    s_ctx = jnp.einsum('bnqhgd,bchd->bnhgqc', qb, kc, preferred_element_type=jnp.float32) * scale
    sink_col = jnp.broadcast_to(sink.astype(jnp.float32).reshape(1, 1, kvh, g, 1, 1), s_lat.shape[:-1] + (1,))
    p = jax.nn.softmax(jnp.concatenate([s_lat, s_ctx, sink_col], axis=-1), axis=-1)
    p_lat = p[..., :3 * BLOCK].astype(v.dtype)
    p_ctx = p[..., 3 * BLOCK:3 * BLOCK + n_ctx].astype(v.dtype)
    o = (jnp.einsum('bnhgqk,bnkhd->bnqhgd', p_lat, vw)
         + jnp.einsum('bnhgqc,bchd->bnqhgd', p_ctx, vc))
    return o.reshape(b, n, h * dh)


def context_attention(q, k, v, sink):
    b, c, h, dh = q.shape
    kvh = k.shape[2]
    g = h // kvh
    s = jnp.einsum('bqhgd,bkhd->bhgqk', q.reshape(b, c, kvh, g, dh), k,
                   preferred_element_type=jnp.float32) * dh ** -0.5
    if sink is not None:
        sink_col = jnp.broadcast_to(sink.astype(jnp.float32).reshape(1, kvh, g, 1, 1), s.shape[:-1] + (1,))
        s = jnp.concatenate([s, sink_col], axis=-1)
    p = jax.nn.softmax(s, axis=-1)[..., :k.shape[1]]
    o = jnp.einsum('bhgqk,bkhd->bqhgd', p.astype(v.dtype), v)
    return o.reshape(b, c, h * dh)


def global_attention(q, k_all, v_all):
    b, n, h, dh = q.shape
    kvh = k_all.shape[2]
    g = h // kvh
    nb = n // BLOCK
    qb = jnp.moveaxis(q.reshape(b, nb, BLOCK, kvh, g, dh), 1, 0)
    scale = dh ** -0.5

    def one_block(qblk):
        s = jnp.einsum('bqhgd,bkhd->bhgqk', qblk, k_all, preferred_element_type=jnp.float32) * scale
        p = jax.nn.softmax(s, axis=-1)
        return jnp.einsum('bhgqk,bkhd->bqhgd', p.astype(v_all.dtype), v_all)

    o = lax.map(one_block, qb)
    return jnp.moveaxis(o, 0, 1).reshape(b, n, h * dh)


def s5_discretize(a_re, a_im, log_dt, b_re, b_im):
    lam = lax.complex(a_re.astype(jnp.float32), a_im.astype(jnp.float32))
    dt = jnp.exp(log_dt.astype(jnp.float32))[:, None]
    lam_bar = jnp.exp(lam * dt)
    b_mat = lax.complex(b_re.astype(jnp.float32), b_im.astype(jnp.float32))
    return lam_bar, ((lam_bar - 1.0) / lam)[..., None] * b_mat


def _ssm_combine(e_i, e_j):
    a_i, b_i = e_i
    a_j, b_j = e_j
    return a_j * a_i, a_j * b_i + b_j


def s5_scan(u, lam_bar, b_bar, h0):
    bu = jnp.einsum('btgh,gph->btgp', u.astype(jnp.complex64), b_bar)
    bu = bu.at[:, 0].add(lam_bar * h0)
    a = jnp.broadcast_to(lam_bar, bu.shape)
    _, states = lax.associative_scan(_ssm_combine, (a, bu), axis=1)
    return states


def s5_readout(states, c_re, c_im):
    return (jnp.einsum('btgp,ghp->btgh', states.real, c_re.astype(jnp.float32))
            - jnp.einsum('btgp,ghp->btgh', states.imag, c_im.astype(jnp.float32)))


def s5_output(y, u, d_skip, w_glu):
    b, n = y.shape[:2]
    y = (y + d_skip.astype(jnp.float32).reshape(SSM_GROUPS, SSM_GROUP) * u).reshape(b, n, BRANCH_W)
    z = jax.nn.gelu(y) @ w_glu.astype(jnp.float32)
    return z[..., :BRANCH_W] * jax.nn.sigmoid(z[..., BRANCH_W:])


def maybe_flip(z, rev):
    return jnp.flip(z, axis=1) if rev else z


def s5_branch(u_lat, u_ctx, a_re, a_im, log_dt, b_re, b_im, c_re, c_im, d_skip, w_glu, with_ctx_out):
    b, n, _ = u_lat.shape
    ul = u_lat.astype(jnp.float32).reshape(b, n, SSM_GROUPS, SSM_GROUP)
    uc = u_ctx.astype(jnp.float32).reshape(b, u_ctx.shape[1], SSM_GROUPS, SSM_GROUP)
    h_zero = jnp.zeros((b, SSM_GROUPS, SSM_STATE), jnp.complex64)
    ys_lat, ys_ctx = [], []
    for d in range(2):
        rev = d == 1
        lam_bar, b_bar = s5_discretize(a_re[d], a_im[d], log_dt[d], b_re[d], b_im[d])
        st_ctx = s5_scan(maybe_flip(uc, rev), lam_bar, b_bar, h_zero)
        st_lat = s5_scan(maybe_flip(ul, rev), lam_bar, b_bar, st_ctx[:, -1])
        ys_lat.append(maybe_flip(s5_readout(st_lat, c_re[d], c_im[d]), rev))
        if with_ctx_out:
            ys_ctx.append(maybe_flip(s5_readout(st_ctx, c_re[d], c_im[d]), rev))
    y_lat = s5_output(ys_lat[0] + ys_lat[1], ul, d_skip, w_glu).astype(u_lat.dtype)
    if not with_ctx_out:
        return y_lat, None
    y_ctx = s5_output(ys_ctx[0] + ys_ctx[1], uc, d_skip, w_glu).astype(u_ctx.dtype)
    return y_lat, y_ctx


def merge_branches(branches, gate_logits, w_branch, w_out):
    g = jax.nn.sigmoid(gate_logits.reshape(gate_logits.shape[:-1] + (N_BRANCH, D_MODEL)))
    terms = [g[..., k, :] * (y_k @ w_branch[k]) for k, y_k in enumerate(branches)]
    return sum(terms[1:], terms[0]) @ w_out


def token_mixer(h, hc, cos, sin, w_in, sink, qk_g, pool_w, pool_scale, a_re, a_im, log_dt,
                b_re, b_im, c_re, c_im, d_skip, w_glu, w_branch, w_out, with_ctx_out):
    p = h @ w_in
    pc = hc @ (w_in if with_ctx_out else w_in[:, :CTX_COLS])
    q_g, k_g = qk_g[0], qk_g[1]
    k_win_c = split_heads(pc[..., O_KB:O_VB], WIN_KV_HEADS)
    v_win_c = split_heads(pc[..., O_VB:O_UC], WIN_KV_HEADS)
    k_glb_c = rms_norm(split_heads(pc[..., O_KD:O_VD], GLB_KV_HEADS), k_g)
    v_glb_c = split_heads(pc[..., O_VD:CTX_COLS], GLB_KV_HEADS)
    y_a = multi_scale_pool(p[..., O_XA:O_GATE], pool_w, pool_scale)
    q_win = apply_rope(split_heads(p[..., O_QB:O_QD], WIN_HEADS), cos, sin)
    k_win = apply_rope(split_heads(p[..., O_KB:O_VB], WIN_KV_HEADS), cos, sin)
    v_win = split_heads(p[..., O_VB:O_UC], WIN_KV_HEADS)
    y_b = window_attention(q_win, k_win, v_win, k_win_c, v_win_c, sink)
    y_c, y_c_ctx = s5_branch(p[..., O_UC:O_KD], pc[..., O_UC:O_KD], a_re, a_im, log_dt,
                             b_re, b_im, c_re, c_im, d_skip, w_glu, with_ctx_out)
    q_glb = apply_rope(rms_norm(split_heads(p[..., O_QD:O_XA], GLB_HEADS), q_g), cos, sin)
    k_glb = apply_rope(rms_norm(split_heads(p[..., O_KD:O_VD], GLB_KV_HEADS), k_g), cos, sin)
    v_glb = split_heads(p[..., O_VD:CTX_COLS], GLB_KV_HEADS)
    y_d = global_attention(q_glb, jnp.concatenate([k_glb_c, k_glb], axis=1),
                           jnp.concatenate([v_glb_c, v_glb], axis=1))
    y = merge_branches((y_a, y_b, y_c, y_d), p[..., O_GATE:], w_branch, w_out)
    if not with_ctx_out:
        return y, None
    y_a_c = multi_scale_pool(pc[..., O_XA:O_GATE], pool_w, pool_scale)
    y_b_c = context_attention(split_heads(pc[..., O_QB:O_QD], WIN_HEADS), k_win_c, v_win_c, sink)
    q_glb_c = rms_norm(split_heads(pc[..., O_QD:O_XA], GLB_HEADS), q_g)
    y_d_c = context_attention(q_glb_c, k_glb_c, v_glb_c, None)
    y_ctx = merge_branches((y_a_c, y_b_c, y_c_ctx, y_d_c), pc[..., O_GATE:], w_branch, w_out)
    return y, y_ctx


def setup_inputs(seed: int = 0) -> dict:
    key = jax.random.key(seed)
    ks = jax.random.split(key, 26)
    L = DEPTH
    G, P, H = SSM_GROUPS, SSM_STATE, SSM_GROUP

    def nrm(k, shape, s):
        return jax.random.normal(k, shape, jnp.float32) * s

    return {
        'x': nrm(ks[0], (BATCH, SEQ, D_MODEL), 1.0),
        'c': nrm(ks[1], (BATCH, D_MODEL), 1.0),
        'ctx': nrm(ks[2], (BATCH, CTX_LEN, D_MODEL), 1.0),
        'c_ctx': nrm(ks[3], (D_MODEL,), 1.0),
        'w_mod': nrm(ks[4], (L, D_MODEL, N_SUB * 3 * D_MODEL), 0.5 * D_MODEL ** -0.5),
        'b_mod': nrm(ks[5], (L, N_SUB * 3 * D_MODEL), 0.02),
        'norm_g': 1.0 + nrm(ks[6], (L, N_SUB, D_MODEL), 0.02),
        'ffn_in': nrm(ks[7], (L, 2, D_MODEL, 2 * D_FF), D_MODEL ** -0.5),
        'ffn_out': nrm(ks[8], (L, 2, D_FF, D_MODEL), D_FF ** -0.5),
        'w_in': nrm(ks[9], (L, D_MODEL, IN_W), D_MODEL ** -0.5),
        'win_sink': nrm(ks[10], (L, WIN_HEADS), 0.5),
        'qk_norm': 1.0 + nrm(ks[11], (L, 2, HEAD_DIM), 0.02),
        'pool_w': nrm(ks[12], (L, len(POOL_WINDOWS), POOL_GROUP, POOL_GROUP), POOL_GROUP ** -0.5),
        'pool_scale': 1.0 + nrm(ks[13], (L, BRANCH_W), 0.1),
        'ssm_a_re': -0.5 + nrm(ks[14], (L, 2, G, P), 0.01),
        'ssm_a_im': math.pi * jnp.arange(P, dtype=jnp.float32) + nrm(ks[15], (L, 2, G, P), 0.01),
        'ssm_log_dt': jax.random.uniform(ks[16], (L, 2, G), jnp.float32, math.log(1e-3), math.log(1e-1)),
        'ssm_b_re': nrm(ks[17], (L, 2, G, P, H), (2 * H) ** -0.5),
        'ssm_b_im': nrm(ks[18], (L, 2, G, P, H), (2 * H) ** -0.5),
        'ssm_c_re': nrm(ks[19], (L, 2, G, H, P), P ** -0.5),
        'ssm_c_im': nrm(ks[20], (L, 2, G, H, P), P ** -0.5),
        'ssm_d': nrm(ks[21], (L, BRANCH_W), 1.0),
        'glu_w': nrm(ks[22], (L, BRANCH_W, 2 * BRANCH_W), BRANCH_W ** -0.5),
        'branch_w': nrm(ks[23], (L, N_BRANCH, BRANCH_W, D_MODEL), BRANCH_W ** -0.5),
        'out_w': nrm(ks[24], (L, D_MODEL, D_MODEL), D_MODEL ** -0.5),
        'final_g': 1.0 + nrm(ks[25], (D_MODEL,), 0.02),
    }


def reference(x, c, ctx, c_ctx, w_mod, b_mod, norm_g, ffn_in, ffn_out, w_in, win_sink, qk_norm,
              pool_w, pool_scale, ssm_a_re, ssm_a_im, ssm_log_dt, ssm_b_re, ssm_b_im, ssm_c_re,
              ssm_c_im, ssm_d, glu_w, branch_w, out_w, final_g):
    b = x.shape[0]
    rows = x.shape[1] // GRID_W
    cos, sin = rope_tables(rows)
    s_lat = jax.nn.silu(c)
    s_ctx = jax.nn.silu(c_ctx)
    for l in range(DEPTH):
        last = l == DEPTH - 1
        m = (s_lat @ w_mod[l] + b_mod[l]).reshape(b, N_SUB, 3, D_MODEL)[:, :, :, None, :]
        mc = (s_ctx @ w_mod[l] + b_mod[l]).reshape(N_SUB, 3, D_MODEL)
        h = modulate(rms_norm(x, norm_g[l, 0]), m[:, 0, 0], m[:, 0, 1])
        x = x + 0.5 * m[:, 0, 2] * swiglu(h, ffn_in[l, 0], ffn_out[l, 0])
        hc = modulate(rms_norm(ctx, norm_g[l, 0]), mc[0, 0], mc[0, 1])
        ctx = ctx + 0.5 * mc[0, 2] * swiglu(hc, ffn_in[l, 0], ffn_out[l, 0])
        h = modulate(rms_norm(x, norm_g[l, 1]), m[:, 1, 0], m[:, 1, 1])
        hc = modulate(rms_norm(ctx, norm_g[l, 1]), mc[1, 0], mc[1, 1])
        y, y_ctx = token_mixer(h, hc, cos, sin, w_in[l], win_sink[l], qk_norm[l], pool_w[l],
                               pool_scale[l], ssm_a_re[l], ssm_a_im[l], ssm_log_dt[l], ssm_b_re[l],
                               ssm_b_im[l], ssm_c_re[l], ssm_c_im[l], ssm_d[l], glu_w[l],
                               branch_w[l], out_w[l], not last)
        x = x + m[:, 1, 2] * y
        if not last:
            ctx = ctx + mc[1, 2] * y_ctx
        h = modulate(rms_norm(x, norm_g[l, 2]), m[:, 2, 0], m[:, 2, 1])
        x = x + 0.5 * m[:, 2, 2] * swiglu(h, ffn_in[l, 1], ffn_out[l, 1])
        if not last:
            hc = modulate(rms_norm(ctx, norm_g[l, 2]), mc[2, 0], mc[2, 1])
            ctx = ctx + 0.5 * mc[2, 2] * swiglu(hc, ffn_in[l, 1], ffn_out[l, 1])
    return rms_norm(x, final_g)
```

```python
import functools
import math

import jax
import jax.numpy as jnp
import numpy as np
from jax import lax
from jax.experimental import pallas as pl
from jax.experimental.pallas import tpu as pltpu

F32 = jnp.float32
BF16 = jnp.bfloat16

D_MODEL = 1024
D_FF = 2816
N_SUB = 3
BRANCH_W = 256
HEAD_DIM = 64
GRID_W = 64
BLOCK = 128
EPS = 1e-6
ROPE_THETA = 10000.0
N_BRANCH = 4
POOL_WINDOWS = (2, 4, 8, 16)
SSM_GROUPS = 16
SSM_GROUP = 16
SSM_STATE = 64
N_STATE = SSM_GROUPS * SSM_STATE
PROJ_W = 6 * BRANCH_W
MOD_ROWS = 8
POOL_HALO = 8
SCAN_ROWS = 8
NEG = -1e30
QK_SCALE = HEAD_DIM ** -0.5
TOKEN_TILE = 256
VMEM_LIMIT = 56 * 1024 * 1024


def _cparams(sem):
    return pltpu.CompilerParams(dimension_semantics=sem, vmem_limit_bytes=VMEM_LIMIT)


def _dot(a, b):
    return jnp.dot(a, b, preferred_element_type=F32)


def _dot_nt(a, b):
    return lax.dot_general(a, b, (((1,), (1,)), ((), ())), preferred_element_type=F32)


def _rms(x, g):
    ms = jnp.mean(x * x, axis=-1, keepdims=True)
    return x * lax.rsqrt(ms + EPS) * g


def _silu(x):
    return x * jax.nn.sigmoid(x)


def _const_spec(shape):
    nd = len(shape)
    return pl.BlockSpec(shape, lambda *_: (0,) * nd)


def _mod_kernel(s_ref, w_ref, b_ref, o_ref):
    s = _silu(s_ref[...])
    w = w_ref[0]
    s_hi = s.astype(BF16)
    s_lo = (s - s_hi.astype(F32)).astype(BF16)
    w_hi = w.astype(BF16)
    w_lo = (w - w_hi.astype(F32)).astype(BF16)
    o_ref[0] = _dot(s_hi, w_hi) + _dot(s_hi, w_lo) + _dot(s_lo, w_hi) + b_ref[0]


def _modulation(rows, w_mod, b_mod):
    depth, d, width = w_mod.shape
    tn = 1024
    out = pl.pallas_call(
        _mod_kernel,
        grid=(depth, width // tn),
        in_specs=[
            pl.BlockSpec((MOD_ROWS, d), lambda l, j: (0, 0)),
            pl.BlockSpec((1, d, tn), lambda l, j: (l, 0, j)),
            pl.BlockSpec((1, 1, tn), lambda l, j: (l, 0, j)),
        ],
        out_specs=pl.BlockSpec((1, MOD_ROWS, tn), lambda l, j: (l, 0, j)),
        out_shape=jax.ShapeDtypeStruct((depth, MOD_ROWS, width), F32),
        compiler_params=_cparams(("parallel", "parallel")),
        name="modulation",
    )(rows, w_mod, b_mod.reshape(depth, 1, width))
    return out.reshape(depth, MOD_ROWS, N_SUB * 3, 1, d).transpose(0, 2, 1, 3, 4)


def _mod_spec(layer, vec, row_of):
    return pl.BlockSpec((None, None, None, 1, D_MODEL), lambda b, t: (layer, vec, row_of(b, t), 0, 0))


def _ffn_kernel(x_ref, sh_ref, sc_ref, gt_ref, ng_ref, win_ref, wout_ref, *rest, final):
    o_ref = rest[-1]
    x = x_ref[0]
    h = (_rms(x, ng_ref[...]) * (1.0 + sc_ref[...]) + sh_ref[...]).astype(BF16)
    gu = _dot(h, win_ref[...])
    a = (_silu(gu[:, :D_FF]) * gu[:, D_FF:]).astype(BF16)
    xn = x + (0.5 * gt_ref[...]) * _dot(a, wout_ref[...])
    if final:
        xn = _rms(xn, rest[0][...])
    o_ref[0] = xn


def _ffn(x, mod, layer, sub, row_of, norm_g, w_in, w_out, n_rows, final_g=None):
    bsz = x.shape[0]
    tm = TOKEN_TILE
    final = final_g is not None
    in_specs = [
        pl.BlockSpec((1, tm, D_MODEL), lambda b, t: (b, t, 0)),
        _mod_spec(layer, sub * 3 + 0, row_of),
        _mod_spec(layer, sub * 3 + 1, row_of),
        _mod_spec(layer, sub * 3 + 2, row_of),
        _const_spec((1, D_MODEL)),
        pl.BlockSpec((D_MODEL, 2 * D_FF), lambda b, t: (0, 0), pipeline_mode=pl.Buffered(1)),
        pl.BlockSpec((D_FF, D_MODEL), lambda b, t: (0, 0), pipeline_mode=pl.Buffered(1)),
    ]
    args = [x, mod, mod, mod, norm_g.reshape(1, D_MODEL), w_in, w_out]
    if final:
        in_specs.append(_const_spec((1, D_MODEL)))
        args.append(final_g.reshape(1, D_MODEL))
    return pl.pallas_call(
        functools.partial(_ffn_kernel, final=final),
        grid=(bsz, n_rows // tm),
        in_specs=in_specs,
        out_specs=pl.BlockSpec((1, tm, D_MODEL), lambda b, t: (b, t, 0)),
        out_shape=jax.ShapeDtypeStruct((bsz, n_rows, D_MODEL), F32),
        compiler_params=_cparams(("parallel", "parallel")),
        name="ffn_final" if final else "ffn",
    )(*args)


def _rope(x, cos, sin, first_half):
    w = x.shape[-1]
    swapped = jnp.where(first_half, pltpu.roll(x, w - 16, 1), pltpu.roll(x, 16, 1))
    return x * cos + swapped * sin


def _head_rms(x, g, low_head):
    x2 = x * x
    lo = jnp.sum(jnp.where(low_head, x2, 0.0), axis=-1, keepdims=True)
    hi = jnp.sum(jnp.where(low_head, 0.0, x2), axis=-1, keepdims=True)
    ms = jnp.where(low_head, lo, hi) * (1.0 / HEAD_DIM)
    return x * lax.rsqrt(ms + EPS) * g


def _store_padded(ref, x, low_head):
    sw = pltpu.roll(x, HEAD_DIM, 1)
    zero = jnp.zeros_like(x)
    ref[0, 0, 0] = jnp.where(low_head, x, zero).astype(ref.dtype)
    ref[0, 0, 1] = jnp.where(low_head, zero, sw).astype(ref.dtype)
    ref[0, 1, 0] = jnp.where(low_head, sw, zero).astype(ref.dtype)
    ref[0, 1, 1] = jnp.where(low_head, zero, x).astype(ref.dtype)


def _proj_kernel(x_ref, sh_ref, sc_ref, ng_ref, w_ref, cos_ref, sin_ref, qg_ref, kg_ref,
                 qw_ref, qd_ref, kw_ref, vw_ref, kd_ref, vd_ref, u_ref, xa_ref):
    x = x_ref[0]
    tm = x.shape[0]
    h = (_rms(x, ng_ref[...]) * (1.0 + sc_ref[...]) + sh_ref[...]).astype(BF16)
    p = _dot(h, w_ref[...])
    lane = lax.broadcasted_iota(jnp.int32, (tm, 2 * HEAD_DIM), 1)
    low_head = lane < HEAD_DIM
    first_half = (lane % 32) < 16
    cos = cos_ref[...]
    sin = sin_ref[...]
    rope = lambda z: _rope(z, cos, sin, first_half)
    k_win = rope(p[:, 0:128])
    v_win = p[:, 128:256]
    u_ref[0] = p[:, 256:512]
    k_glb = rope(_head_rms(p[:, 512:640], kg_ref[...], low_head))
    v_glb = p[:, 640:768]
    for c in range(2):
        qw = rope(p[:, 768 + 128 * c:896 + 128 * c]) * QK_SCALE
        qw_ref[0, :, 128 * c:128 * (c + 1)] = qw.astype(BF16)
        qd = rope(_head_rms(p[:, 1024 + 128 * c:1152 + 128 * c], qg_ref[...], low_head)) * QK_SCALE
        qd_ref[0, :, 128 * c:128 * (c + 1)] = qd.astype(BF16)
    xa_ref[0] = p[:, 1280:1536]
    _store_padded(kw_ref, k_win, low_head)
    _store_padded(vw_ref, v_win, low_head)
    _store_padded(kd_ref, k_glb, low_head)
    _store_padded(vd_ref, v_glb, low_head)


def _proj(x, mod, layer, row_of, norm_g, w, cos, sin, q_g, k_g):
    bsz, t_all, _ = x.shape
    tm = TOKEN_TILE
    tok = lambda width: pl.BlockSpec((1, tm, width), lambda b, t: (b, t, 0))
    pad = pl.BlockSpec((1, 2, 2, tm, 128), lambda b, t: (b, 0, 0, t, 0))
    tab = pl.BlockSpec((tm, 128), lambda b, t: (t, 0))
    pad_shape = jax.ShapeDtypeStruct((bsz, 2, 2, t_all, 128), BF16)
    return pl.pallas_call(
        _proj_kernel,
        grid=(bsz, t_all // tm),
        in_specs=[
            tok(D_MODEL),
            _mod_spec(layer, 3, row_of),
            _mod_spec(layer, 4, row_of),
            _const_spec((1, D_MODEL)),
            pl.BlockSpec((D_MODEL, PROJ_W), lambda b, t: (0, 0), pipeline_mode=pl.Buffered(1)),
            tab, tab,
            _const_spec((1, 128)), _const_spec((1, 128)),
        ],
        out_specs=[tok(256), tok(256), pad, pad, pad, pad, tok(256), tok(256)],
        out_shape=[
            jax.ShapeDtypeStruct((bsz, t_all, 256), BF16),
            jax.ShapeDtypeStruct((bsz, t_all, 256), BF16),
            pad_shape, pad_shape, pad_shape, pad_shape,
            jax.ShapeDtypeStruct((bsz, t_all, 256), F32),
            jax.ShapeDtypeStruct((bsz, t_all, 256), F32),
        ],
        compiler_params=_cparams(("parallel", "parallel")),
        name="proj",
    )(x, mod, mod, norm_g.reshape(1, D_MODEL), w, cos, sin,
      jnp.tile(q_g, 2).reshape(1, 128), jnp.tile(k_g, 2).reshape(1, 128))


def _win_kernel(sink_ref, q_ref, kp_ref, kc_ref, kn_ref, kx_ref, vp_ref, vc_ref, vn_ref, vx_ref,
                o_ref, *, n_lat_blocks):
    i = pl.program_id(1)
    r = lax.broadcasted_iota(jnp.int32, (BLOCK, BLOCK), 0)
    c = lax.broadcasted_iota(jnp.int32, (BLOCK, BLOCK), 1)
    d = c - r
    is_lat = i < n_lat_blocks
    far = 4 * BLOCK
    m_prev = d >= jnp.where(jnp.logical_and(is_lat, i >= 1), 0, far)
    m_cur = d >= jnp.where(is_lat, -far, far)
    m_next = (-d) >= jnp.where(jnp.logical_and(is_lat, i + 1 < n_lat_blocks), 0, far)
    for j in range(2):
        q = q_ref[0, :, 128 * j:128 * (j + 1)]
        acc = jnp.zeros((BLOCK, 128), F32)
        for g in range(2):
            s_p = jnp.where(m_prev, _dot_nt(q, kp_ref[0, j, g]), NEG)
            s_c = jnp.where(m_cur, _dot_nt(q, kc_ref[0, j, g]), NEG)
            s_n = jnp.where(m_next, _dot_nt(q, kn_ref[0, j, g]), NEG)
            s_x = _dot_nt(q, kx_ref[0, j, g])
            sink = sink_ref[2 * j + g]
            m = jnp.maximum(jnp.maximum(s_p.max(-1, keepdims=True), s_c.max(-1, keepdims=True)),
                            jnp.maximum(s_n.max(-1, keepdims=True), s_x.max(-1, keepdims=True)))
            m = jnp.maximum(m, sink)
            p_p = jnp.exp(s_p - m)
            p_c = jnp.exp(s_c - m)
            p_n = jnp.exp(s_n - m)
            p_x = jnp.exp(s_x - m)
            den = (p_p.sum(-1, keepdims=True) + p_c.sum(-1, keepdims=True)
                   + p_n.sum(-1, keepdims=True) + p_x.sum(-1, keepdims=True) + jnp.exp(sink - m))
            pv = (_dot(p_p.astype(BF16), vp_ref[0, j, g]) + _dot(p_c.astype(BF16), vc_ref[0, j, g])
                  + _dot(p_n.astype(BF16), vn_ref[0, j, g]) + _dot(p_x.astype(BF16), vx_ref[0, j, g]))
            acc = acc + pv / den
        o_ref[0, :, 128 * j:128 * (j + 1)] = acc.astype(o_ref.dtype)


def _window_attention(q, k_pad, v_pad, sink, n_lat, n_ctx, with_ctx):
    bsz, t_all, _ = q.shape
    n_lat_blocks = n_lat // BLOCK
    n_blocks = t_all // BLOCK
    n_q_blocks = n_blocks if with_ctx else n_lat_blocks
    band = lambda off: pl.BlockSpec(
        (1, 2, 2, BLOCK, 128), lambda b, i: (b, 0, 0, jnp.clip(i + off, 0, n_blocks - 1), 0))
    ctx = pl.BlockSpec((1, 2, 2, n_ctx, 128), lambda b, i: (b, 0, 0, n_lat // n_ctx, 0))
    return pl.pallas_call(
        functools.partial(_win_kernel, n_lat_blocks=n_lat_blocks),
        grid=(bsz, n_q_blocks),
        in_specs=[
            pl.BlockSpec(memory_space=pltpu.SMEM),
            pl.BlockSpec((1, BLOCK, 256), lambda b, i: (b, i, 0)),
            band(-1), band(0), band(1), ctx,
            band(-1), band(0), band(1), ctx,
        ],
        out_specs=pl.BlockSpec((1, BLOCK, 256), lambda b, i: (b, i, 0)),
        out_shape=jax.ShapeDtypeStruct((bsz, n_q_blocks * BLOCK, 256), BF16),
        compiler_params=_cparams(("parallel", "parallel")),
        name="window_attention",
    )(sink, q, k_pad, k_pad, k_pad, k_pad, v_pad, v_pad, v_pad, v_pad)


def _glb_kernel(q_ref, k_ref, v_ref, o_ref, m_ref, l_ref, acc_ref):
    ki = pl.program_id(3)

    @pl.when(ki == 0)
    def _():
        m_ref[...] = jnp.full_like(m_ref, NEG)
        l_ref[...] = jnp.zeros_like(l_ref)
        acc_ref[...] = jnp.zeros_like(acc_ref)

    q = q_ref[0]
    tq = q.shape[0]
    low_head = lax.broadcasted_iota(jnp.int32, (tq, 128), 1) < HEAD_DIM
    m_prev = m_ref[...]
    m_new, p_sum, pv = [], [], []
    for g in range(2):
        s = _dot_nt(q, k_ref[0, 0, g])
        m_g = jnp.maximum(m_prev[:, HEAD_DIM * g:HEAD_DIM * g + 1], s.max(-1, keepdims=True))
        p = jnp.exp(s - m_g)
        m_new.append(m_g)
        p_sum.append(p.sum(-1, keepdims=True))
        pv.append(_dot(p.astype(BF16), v_ref[0, 0, g]))
    m_cur = jnp.where(low_head, m_new[0], m_new[1])
    alpha = jnp.exp(m_prev - m_cur)
    l_ref[...] = alpha * l_ref[...] + jnp.where(low_head, p_sum[0], p_sum[1])
    acc_ref[...] = alpha * acc_ref[...] + pv[0] + pv[1]
    m_ref[...] = m_cur

    @pl.when(ki == pl.num_programs(3) - 1)
    def _():
        o_ref[0] = (acc_ref[...] / l_ref[...]).astype(o_ref.dtype)


def _global_attention(q, k_pad, v_pad, q_row0, n_q, k_row0, n_k, tq, tk):
    bsz = q.shape[0]
    q0, k0 = q_row0 // tq, k_row0 // tk
    kv = pl.BlockSpec((1, 1, 2, tk, 128), lambda b, j, qi, ki: (b, j, 0, k0 + ki, 0))
    return pl.pallas_call(
        _glb_kernel,
        grid=(bsz, 2, n_q // tq, n_k // tk),
        in_specs=[pl.BlockSpec((1, tq, 128), lambda b, j, qi, ki: (b, q0 + qi, j)), kv, kv],
        out_specs=pl.BlockSpec((1, tq, 128), lambda b, j, qi, ki: (b, qi, j)),
        out_shape=jax.ShapeDtypeStruct((bsz, n_q, 256), BF16),
        scratch_shapes=[pltpu.VMEM((tq, 128), F32)] * 3,
        compiler_params=_cparams(("parallel", "parallel", "parallel", "arbitrary")),
        name="global_attention",
    )(q, k_pad, v_pad)


def _s5_tile(u, bmat_ref, cmat_ref, a_ref, pc_ref, carry_ref, st_ref, reverse):
    rows = u.shape[0]
    n_groups = rows // SCAN_ROWS
    ns = N_STATE
    st_ref[0:rows, :] = _dot(u.astype(BF16), bmat_ref[0])
    pr = pc_ref[0, :, :ns]
    pi = pc_ref[0, :, ns:]

    def body(n, carry):
        cr, ci = carry
        g = (n_groups - 1 - n) if reverse else n
        row0 = pl.multiple_of(g * SCAN_ROWS, SCAN_ROWS)
        vr = st_ref[pl.ds(row0, SCAN_ROWS), :ns]
        vi = st_ref[pl.ds(row0, SCAN_ROWS), ns:]
        for s, k in enumerate((1, 2, 4)):
            ar = a_ref[0, s, :, :ns]
            ai = a_ref[0, s, :, ns:]
            shift = SCAN_ROWS - k if reverse else k
            sr = pltpu.roll(vr, shift, 0)
            si = pltpu.roll(vi, shift, 0)
            vr, vi = vr + ar * sr - ai * si, vi + ar * si + ai * sr
        vr, vi = vr + pr * cr - pi * ci, vi + pr * ci + pi * cr
        st_ref[pl.ds(row0, SCAN_ROWS), :ns] = vr
        st_ref[pl.ds(row0, SCAN_ROWS), ns:] = vi
        last = 0 if reverse else SCAN_ROWS - 1
        return (jnp.broadcast_to(vr[last:last + 1], (SCAN_ROWS, ns)),
                jnp.broadcast_to(vi[last:last + 1], (SCAN_ROWS, ns)))

    cr, ci = lax.fori_loop(0, n_groups, body, (carry_ref[:, :ns], carry_ref[:, ns:]))
    carry_ref[:, :ns] = cr
    carry_ref[:, ns:] = ci
    return _dot(st_ref[0:rows, :].astype(BF16), cmat_ref[0])


def _s5_kernel(ul_ref, uc_ref, bmat_ref, cmat_ref, a_ref, pc_ref, y_ref, yc_ref, carry_ref, st_ref):
    direction = pl.program_id(0)
    t = pl.program_id(2)
    for reverse in (False, True):
        @pl.when(direction == int(reverse))
        def _(reverse=reverse):
            tile = functools.partial(_s5_tile, bmat_ref=bmat_ref, cmat_ref=cmat_ref, a_ref=a_ref,
                                     pc_ref=pc_ref, carry_ref=carry_ref, st_ref=st_ref, reverse=reverse)

            @pl.when(t == 0)
            def _():
                carry_ref[...] = jnp.zeros_like(carry_ref)
                yc_ref[0, 0] = tile(uc_ref[0])

            y_ref[0, 0] = tile(ul_ref[0])


def _s5_tables(a_re, a_im, log_dt, b_re, b_im, c_re, c_im):
    lam = lax.complex(a_re, a_im)
    lam_dt = lam * jnp.exp(log_dt)[..., None]
    lam_bar = jnp.exp(lam_dt)
    b_bar = ((lam_bar - 1.0) / lam)[..., None] * lax.complex(b_re, b_im)
    eye = jnp.eye(SSM_GROUPS, dtype=F32)
    in_w, st_w = SSM_GROUPS * SSM_GROUP, N_STATE
    b_blk = lambda z: jnp.einsum('dgph,gk->dghkp', z, eye).reshape(2, in_w, st_w)
    bmat = jnp.concatenate([b_blk(b_bar.real), b_blk(b_bar.imag)], axis=-1).astype(BF16)
    c_blk = lambda z: jnp.einsum('dghp,gk->dgpkh', z, eye).reshape(2, st_w, in_w)
    cmat = jnp.concatenate([c_blk(c_re), -c_blk(c_im)], axis=1).astype(BF16)
    steps = jnp.arange(1, SCAN_ROWS + 1, dtype=F32)
    powers = jnp.exp(lam_dt[:, None] * steps[None, :, None, None]).reshape(2, SCAN_ROWS, st_w)
    split = lambda z: jnp.concatenate([z.real, z.imag], axis=-1)
    row = jnp.arange(SCAN_ROWS)
    pc = jnp.stack([split(powers[0]), split(powers[1][::-1])])
    a_tabs = []
    for k in (1, 2, 4):
        fwd = jnp.where((row >= k)[:, None], split(powers[0, k - 1])[None], 0.0)
        bwd = jnp.where((row <= SCAN_ROWS - 1 - k)[:, None], split(powers[1, k - 1])[None], 0.0)
        a_tabs.append(jnp.stack([fwd, bwd]))
    return bmat, cmat, jnp.stack(a_tabs, axis=1), pc


def _s5_scan(u, tables, n_lat, n_ctx):
    bsz = u.shape[0]
    tm = TOKEN_TILE
    assert n_ctx <= tm and n_ctx % SCAN_ROWS == 0
    nt = n_lat // tm
    bmat, cmat, a_tab, pc = tables
    ns2 = 2 * N_STATE
    tile_of = lambda d, t: t + d * (nt - 1 - 2 * t)
    return pl.pallas_call(
        _s5_kernel,
        grid=(2, bsz, nt),
        in_specs=[
            pl.BlockSpec((1, tm, 256), lambda d, b, t: (b, tile_of(d, t), 0)),
            pl.BlockSpec((1, n_ctx, 256), lambda d, b, t: (b, n_lat // n_ctx, 0)),
            pl.BlockSpec((1, 256, ns2), lambda d, b, t: (d, 0, 0)),
            pl.BlockSpec((1, ns2, 256), lambda d, b, t: (d, 0, 0)),
            pl.BlockSpec((1, 3, SCAN_ROWS, ns2), lambda d, b, t: (d, 0, 0, 0)),
            pl.BlockSpec((1, SCAN_ROWS, ns2), lambda d, b, t: (d, 0, 0)),
        ],
        out_specs=[
            pl.BlockSpec((1, 1, tm, 256), lambda d, b, t: (d, b, tile_of(d, t), 0)),
            pl.BlockSpec((1, 1, n_ctx, 256), lambda d, b, t: (d, b, 0, 0)),
        ],
        out_shape=[
            jax.ShapeDtypeStruct((2, bsz, n_lat, 256), F32),
            jax.ShapeDtypeStruct((2, bsz, n_ctx, 256), F32),
        ],
        scratch_shapes=[pltpu.VMEM((SCAN_ROWS, ns2), F32), pltpu.VMEM((tm, ns2), F32)],
        compiler_params=_cparams(("parallel", "parallel", "arbitrary")),
        name="s5_scan",
    )(u, u, bmat, cmat, a_tab, pc)


def _gelu_tanh(x):
    c = math.sqrt(2.0 / math.pi)
    return x * (0.5 * (1.0 + jnp.tanh(c * (x + 0.044715 * (x * x * x)))))


def _merge_kernel(x_ref, sh_ref, sc_ref, gt_ref, ng_ref, xa_ref, xap_ref, xan_ref, yb_ref, yf_ref, yr_ref,
                  u_ref, yd_ref, wg_ref, pw_ref, ps_ref, dsk_ref, glu_ref, wb_ref, wo_ref, o_ref, ext_ref,
                  *, n_lat_tiles, n_lat, n_ctx):
    t = pl.program_id(1)
    x = x_ref[0]
    tm = x.shape[0]
    bw = BRANCH_W
    is_ctx = t >= n_lat_tiles
    seg_tile = jnp.where(is_ctx, t - n_lat_tiles, t)
    seg_tiles = jnp.where(is_ctx, n_ctx // tm, n_lat_tiles)
    seg_len = jnp.where(is_ctx, n_ctx, n_lat)
    xa = xa_ref[0]
    ext_ref[0:POOL_HALO, :] = jnp.where(seg_tile > 0, xap_ref[0], 0.0)
    ext_ref[POOL_HALO:POOL_HALO + tm, :] = xa
    ext_ref[POOL_HALO + tm:, :] = jnp.where(seg_tile + 1 < seg_tiles, xan_ref[0], 0.0)
    at = lambda off: ext_ref[POOL_HALO + off:POOL_HALO + off + tm, :]
    sums = []
    acc = at(-1) + xa
    sums.append(acc)
    for half in (2, 4, 8):
        for off in range(half // 2, half):
            acc = acc + at(-off - 1) + at(off)
        sums.append(acc)
    lane = lax.broadcasted_iota(jnp.int32, (tm, bw), 1)
    pos = seg_tile * tm + lax.broadcasted_iota(jnp.int32, (tm, bw), 0)
    group = lane // (bw // len(POOL_WINDOWS))
    half_w = jnp.left_shift(1, group)
    count = jnp.minimum(pos + half_w, seg_len) - jnp.maximum(pos - half_w, 0)
    window_sum = jnp.where(group == 0, sums[0], jnp.where(group == 1, sums[1],
                                                           jnp.where(group == 2, sums[2], sums[3])))
    diff = window_sum / count.astype(F32) - xa
    y_a = _dot(diff.astype(BF16), pw_ref[...]) * ps_ref[...]
    y_s = yf_ref[0, 0] + yr_ref[0, 0] + dsk_ref[...] * u_ref[0]
    z = _dot(_gelu_tanh(y_s).astype(BF16), glu_ref[...])
    y_c = z[:, :bw] * jax.nn.sigmoid(z[:, bw:])
    h = (_rms(x, ng_ref[...]) * (1.0 + sc_ref[...]) + sh_ref[...]).astype(BF16)
    branches = (y_a.astype(BF16), yb_ref[0], y_c.astype(BF16), yd_ref[0])
    total = None
    for k, y_k in enumerate(branches):
        gate = jax.nn.sigmoid(_dot(h, wg_ref[:, D_MODEL * k:D_MODEL * (k + 1)]))
        term = gate * _dot(y_k, wb_ref[k])
        total = term if total is None else total + term
    o_ref[0] = x + gt_ref[...] * _dot(total.astype(BF16), wo_ref[...])


def _merge(x, mod, layer, row_of, norm_g, xa, y_b, y_s5, u, y_d, w_gate, pool_bd, pool_scale, d_skip,
           glu_w, branch_w, out_w, n_rows, n_lat, n_ctx):
    bsz = x.shape[0]
    tm = TOKEN_TILE
    assert n_ctx % tm == 0 or n_rows == n_lat
    halo_blocks = xa.shape[1] // POOL_HALO
    per_tile = tm // POOL_HALO
    tok = lambda width: pl.BlockSpec((1, tm, width), lambda b, t: (b, t, 0))
    single = dict(pipeline_mode=pl.Buffered(1))
    return pl.pallas_call(
        functools.partial(_merge_kernel, n_lat_tiles=n_lat // tm, n_lat=n_lat, n_ctx=n_ctx),
        grid=(bsz, n_rows // tm),
        in_specs=[
            tok(D_MODEL),
            _mod_spec(layer, 3, row_of), _mod_spec(layer, 4, row_of), _mod_spec(layer, 5, row_of),
            _const_spec((1, D_MODEL)),
            tok(256),
            pl.BlockSpec((1, POOL_HALO, 256), lambda b, t: (b, jnp.maximum(t * per_tile - 1, 0), 0)),
            pl.BlockSpec((1, POOL_HALO, 256),
                         lambda b, t: (b, jnp.minimum((t + 1) * per_tile, halo_blocks - 1), 0)),
            tok(256),
            pl.BlockSpec((1, 1, tm, 256), lambda b, t: (0, b, t, 0)),
            pl.BlockSpec((1, 1, tm, 256), lambda b, t: (1, b, t, 0)),
            tok(256), tok(256),
            pl.BlockSpec((D_MODEL, N_BRANCH * D_MODEL), lambda b, t: (0, 0), **single),
            _const_spec((256, 256)), _const_spec((1, 256)), _const_spec((1, 256)),
            _const_spec((256, 512)),
            pl.BlockSpec((N_BRANCH, 256, D_MODEL), lambda b, t: (0, 0, 0), **single),
            pl.BlockSpec((D_MODEL, D_MODEL), lambda b, t: (0, 0), **single),
        ],
        out_specs=tok(D_MODEL),
        out_shape=jax.ShapeDtypeStruct((bsz, n_rows, D_MODEL), F32),
        scratch_shapes=[pltpu.VMEM((tm + 2 * POOL_HALO, 256), F32)],
        compiler_params=_cparams(("parallel", "parallel")),
        name="merge",
    )(x, mod, mod, mod, norm_g.reshape(1, D_MODEL), xa, xa, xa, y_b, y_s5, y_s5, u, y_d,
      w_gate, pool_bd, pool_scale.reshape(1, 256), d_skip.reshape(1, 256), glu_w, branch_w, out_w)


def _rope_tables(n_lat, n_ctx):
    rows = n_lat // GRID_W
    n_freq = HEAD_DIM // 4
    row = jnp.repeat(jnp.arange(rows), GRID_W)
    col = jnp.tile(jnp.arange(GRID_W), rows)
    inv = ROPE_THETA ** (-jnp.arange(n_freq, dtype=F32) / n_freq)
    ang = jnp.stack([row, col], axis=-1).astype(F32)[..., None] * inv
    cos = jnp.broadcast_to(jnp.cos(ang)[:, :, None, :], (n_lat, 2, 2, n_freq)).reshape(n_lat, HEAD_DIM)
    sign = jnp.array([-1.0, 1.0], F32)[None, None, :, None]
    sin = (jnp.sin(ang)[:, :, None, :] * sign).reshape(n_lat, HEAD_DIM)
    cos = jnp.concatenate([cos, jnp.ones((n_ctx, HEAD_DIM), F32)], axis=0)
    sin = jnp.concatenate([sin, jnp.zeros((n_ctx, HEAD_DIM), F32)], axis=0)
    return jnp.tile(cos, (1, 2)), jnp.tile(sin, (1, 2))


def _largest_divisor(n, candidates):
    for c in candidates:
        if n % c == 0:
            return c
    raise ValueError(f"no tile in {candidates} divides {n}")


def _pool_block_diag(pool_w):
    groups, width, _ = pool_w.shape
    eye = jnp.eye(groups, dtype=pool_w.dtype)
    return jnp.einsum('gcd,gk->gckd', pool_w, eye).reshape(groups * width, groups * width)


def kernel(x, c, ctx, c_ctx, w_mod, b_mod, norm_g, ffn_in, ffn_out, w_in, win_sink, qk_norm, pool_w, pool_scale, ssm_a_re, ssm_a_im, ssm_log_dt, ssm_b_re, ssm_b_im, ssm_c_re, ssm_c_im, ssm_d, glu_w, branch_w, out_w, final_g):
    bsz, n_lat, d = x.shape
    n_ctx = ctx.shape[1]
    depth = w_mod.shape[0]
    t_all = n_lat + n_ctx
    assert d == D_MODEL and bsz < MOD_ROWS and n_lat % TOKEN_TILE == 0 and n_ctx == TOKEN_TILE
    assert n_lat % GRID_W == 0

    cond_rows = jnp.zeros((MOD_ROWS, d), F32).at[:bsz].set(c).at[bsz].set(c_ctx)
    mod = _modulation(cond_rows, w_mod, b_mod)
    n_lat_tiles = n_lat // TOKEN_TILE
    row_of = lambda b, t: jnp.where(t >= n_lat_tiles, bsz, b)
    lat_row = lambda b, t: b

    cos, sin = _rope_tables(n_lat, n_ctx)
    tq = _largest_divisor(n_lat, (1024, 512, 256))
    tk = _largest_divisor(t_all, (1280, 1024, 768, 512, 256))

    xs = jnp.concatenate([x, ctx], axis=1)
    for l in range(depth):
        last = l == depth - 1
        bf = lambda w: w.astype(BF16)
        xs = _ffn(xs, mod, l, 0, row_of, norm_g[l, 0], bf(ffn_in[l, 0]), bf(ffn_out[l, 0]), t_all)

        qw, qd, kw, vw, kd, vd, u, xa = _proj(xs, mod, l, row_of, norm_g[l, 1], bf(w_in[l, :, :PROJ_W]),
                                              cos, sin, qk_norm[l, 0], qk_norm[l, 1])
        y_b = _window_attention(qw, kw, vw, win_sink[l], n_lat, n_ctx, not last)
        y_d = _global_attention(qd, kd, vd, 0, n_lat, 0, t_all, tq, tk)
        tables = _s5_tables(ssm_a_re[l], ssm_a_im[l], ssm_log_dt[l], ssm_b_re[l], ssm_b_im[l],
                            ssm_c_re[l], ssm_c_im[l])
        y_s5, y_s5_ctx = _s5_scan(u, tables, n_lat, n_ctx)
        n_rows = n_lat
        if not last:
            y_d_ctx = _global_attention(qd, kd, vd, n_lat, n_ctx, n_lat, n_ctx, n_ctx, n_ctx)
            y_d = jnp.concatenate([y_d, y_d_ctx], axis=1)
            y_s5 = jnp.concatenate([y_s5, y_s5_ctx], axis=2)
            n_rows = t_all
        xs = _merge(xs, mod, l, row_of, norm_g[l, 1], xa, y_b, y_s5, u, y_d, bf(w_in[l, :, PROJ_W:]),
                    bf(_pool_block_diag(pool_w[l])), pool_scale[l], ssm_d[l], bf(glu_w[l]), bf(branch_w[l]),
                    bf(out_w[l]), n_rows, n_lat, n_ctx)
        xs = _ffn(xs, mod, l, 2, row_of if not last else lat_row, norm_g[l, 2], bf(ffn_in[l, 1]),
                  bf(ffn_out[l, 1]), n_rows, final_g=final_g if last else None)
    return xs
```

```python
import functools
import math

import jax
import jax.numpy as jnp
import numpy as np
from jax import lax
from jax.experimental import pallas as pl
from jax.experimental.pallas import tpu as pltpu

F32 = jnp.float32
BF16 = jnp.bfloat16

D_MODEL = 1024
D_FF = 2816
N_SUB = 3
BRANCH_W = 256
HEAD_DIM = 64
GRID_W = 64
BLOCK = 128
EPS = 1e-6
ROPE_THETA = 10000.0
N_BRANCH = 4
POOL_WINDOWS = (2, 4, 8, 16)
SSM_GROUPS = 16
SSM_GROUP = 16
SSM_STATE = 64
N_STATE = SSM_GROUPS * SSM_STATE
PROJ_W = 6 * BRANCH_W
MOD_ROWS = 8
POOL_HALO = 8
SCAN_ROWS = 8
NEG = -1e30
QK_SCALE = HEAD_DIM ** -0.5
LOG2E = math.log2(math.e)
V_ROWS = HEAD_DIM + 16
MAX_SCORE_BOUND = 40.0
TOKEN_TILE = 256
VMEM_LIMIT = 56 * 1024 * 1024


def _cparams(sem):
    return pltpu.CompilerParams(dimension_semantics=sem, vmem_limit_bytes=VMEM_LIMIT)


def _dot(a, b):
    return jnp.dot(a, b, preferred_element_type=F32)


def _dot_nt(a, b):
    return lax.dot_general(a, b, (((1,), (1,)), ((), ())), preferred_element_type=F32)


def _rms(x, g):
    ms = jnp.mean(x * x, axis=-1, keepdims=True)
    return x * lax.rsqrt(ms + EPS) * g


def _silu(x):
    return x * jax.nn.sigmoid(x)


def _const_spec(shape):
    nd = len(shape)
    return pl.BlockSpec(shape, lambda *_: (0,) * nd)


def _mod_kernel(s_ref, w_ref, b_ref, o_ref):
    s = _silu(s_ref[...])
    w = w_ref[0]
    s_hi = s.astype(BF16)
    s_lo = (s - s_hi.astype(F32)).astype(BF16)
    w_hi = w.astype(BF16)
    w_lo = (w - w_hi.astype(F32)).astype(BF16)
    o_ref[0] = _dot(s_hi, w_hi) + _dot(s_hi, w_lo) + _dot(s_lo, w_hi) + b_ref[0]


def _modulation(rows, w_mod, b_mod):
    depth, d, width = w_mod.shape
    tn = 1024
    out = pl.pallas_call(
        _mod_kernel,
        grid=(depth, width // tn),
        in_specs=[
            pl.BlockSpec((MOD_ROWS, d), lambda l, j: (0, 0)),
            pl.BlockSpec((1, d, tn), lambda l, j: (l, 0, j)),
            pl.BlockSpec((1, 1, tn), lambda l, j: (l, 0, j)),
        ],
        out_specs=pl.BlockSpec((1, MOD_ROWS, tn), lambda l, j: (l, 0, j)),
        out_shape=jax.ShapeDtypeStruct((depth, MOD_ROWS, width), F32),
        compiler_params=_cparams(("parallel", "parallel")),
        name="modulation",
    )(rows, w_mod, b_mod.reshape(depth, 1, width))
    return out.reshape(depth, MOD_ROWS, N_SUB * 3, 1, d).transpose(0, 2, 1, 3, 4)


def _mod_spec(layer, vec, row_of):
    return pl.BlockSpec((None, None, None, 1, D_MODEL), lambda b, t: (layer, vec, row_of(b, t), 0, 0))


def _ffn_kernel(x_ref, sh_ref, sc_ref, gt_ref, ng_ref, win_ref, wout_ref, *rest, final):
    o_ref = rest[-1]
    x = x_ref[0]
    h = (_rms(x, ng_ref[...]) * (1.0 + sc_ref[...]) + sh_ref[...]).astype(BF16)
    gu = _dot(h, win_ref[...])
    a = (_silu(gu[:, :D_FF]) * gu[:, D_FF:]).astype(BF16)
    xn = x + (0.5 * gt_ref[...]) * _dot(a, wout_ref[...])
    if final:
        xn = _rms(xn, rest[0][...])
    o_ref[0] = xn


def _ffn(x, mod, layer, sub, row_of, norm_g, w_in, w_out, n_rows, final_g=None):
    bsz = x.shape[0]
    tm = TOKEN_TILE
    final = final_g is not None
    in_specs = [
        pl.BlockSpec((1, tm, D_MODEL), lambda b, t: (b, t, 0)),
        _mod_spec(layer, sub * 3 + 0, row_of),
        _mod_spec(layer, sub * 3 + 1, row_of),
        _mod_spec(layer, sub * 3 + 2, row_of),
        _const_spec((1, D_MODEL)),
        pl.BlockSpec((D_MODEL, 2 * D_FF), lambda b, t: (0, 0), pipeline_mode=pl.Buffered(1)),
        pl.BlockSpec((D_FF, D_MODEL), lambda b, t: (0, 0), pipeline_mode=pl.Buffered(1)),
    ]
    args = [x, mod, mod, mod, norm_g.reshape(1, D_MODEL), w_in, w_out]
    if final:
        in_specs.append(_const_spec((1, D_MODEL)))
        args.append(final_g.reshape(1, D_MODEL))
    return pl.pallas_call(
        functools.partial(_ffn_kernel, final=final),
        grid=(bsz, n_rows // tm),
        in_specs=in_specs,
        out_specs=pl.BlockSpec((1, tm, D_MODEL), lambda b, t: (b, t, 0)),
        out_shape=jax.ShapeDtypeStruct((bsz, n_rows, D_MODEL), F32),
        compiler_params=_cparams(("parallel", "parallel")),
        name="ffn_final" if final else "ffn",
    )(*args)


def _rope(x, cos, sin, first_half):
    w = x.shape[-1]
    swapped = jnp.where(first_half, pltpu.roll(x, w - 16, 1), pltpu.roll(x, 16, 1))
    return x * cos + swapped * sin


def _head_rms(x, g, low_head):
    x2 = x * x
    lo = jnp.sum(jnp.where(low_head, x2, 0.0), axis=-1, keepdims=True)
    hi = jnp.sum(jnp.where(low_head, 0.0, x2), axis=-1, keepdims=True)
    ms = jnp.where(low_head, lo, hi) * (1.0 / HEAD_DIM)
    return x * lax.rsqrt(ms + EPS) * g


def _store_padded(ref, x, low_head):
    sw = pltpu.roll(x, HEAD_DIM, 1)
    zero = jnp.zeros_like(x)
    ref[0, 0, 0] = jnp.where(low_head, x, zero).astype(ref.dtype)
    ref[0, 0, 1] = jnp.where(low_head, zero, sw).astype(ref.dtype)
    ref[0, 1, 0] = jnp.where(low_head, sw, zero).astype(ref.dtype)
    ref[0, 1, 1] = jnp.where(low_head, zero, x).astype(ref.dtype)


def _proj_kernel(x_ref, sh_ref, sc_ref, ng_ref, w_ref, cos_ref, sin_ref, qg_ref, kg_ref, qb_ref, kvb_ref,
                 qw_ref, qt_ref, kw_ref, vw_ref, ka_ref, vt_ref, u_ref, xa_ref):
    x = x_ref[0]
    tm = x.shape[0]
    h = (_rms(x, ng_ref[...]) * (1.0 + sc_ref[...]) + sh_ref[...]).astype(BF16)
    p = _dot(h, w_ref[...])
    lane = lax.broadcasted_iota(jnp.int32, (tm, 2 * HEAD_DIM), 1)
    low_head = lane < HEAD_DIM
    first_half = jnp.bitwise_and(lane, 31) < 16
    cos = cos_ref[...]
    sin = sin_ref[...]
    rope = lambda z: _rope(z, cos, sin, first_half)
    k_win = rope(p[:, 0:128])
    v_win = p[:, 128:256]
    u_ref[0] = p[:, 256:512]
    k_glb = rope(_head_rms(p[:, 512:640], kg_ref[...], low_head))
    v_glb = p[:, 640:768]
    heads = lambda z: (z, pltpu.roll(z, HEAD_DIM, 1))
    for c in range(2):
        qw = rope(p[:, 768 + 128 * c:896 + 128 * c]) * QK_SCALE
        qw_ref[0, :, 128 * c:128 * (c + 1)] = qw.astype(BF16)
        qd = rope(_head_rms(p[:, 1024 + 128 * c:1152 + 128 * c], qg_ref[...], low_head)) * (QK_SCALE * LOG2E)
        for g, z in enumerate(heads(qd)):
            qt_ref[0, 2 * c + g] = jnp.where(low_head, z, qb_ref[...]).T.astype(BF16)
    for j, (kz, vz) in enumerate(zip(heads(k_glb), heads(v_glb))):
        ka_ref[0, j] = jnp.where(low_head, kz, kvb_ref[...]).astype(BF16)
        vt_ref[0, j] = jnp.where(low_head, vz, kvb_ref[...]).T[0:V_ROWS].astype(BF16)
    xa_ref[0] = p[:, 1280:1536]
    _store_padded(kw_ref, k_win, low_head)
    _store_padded(vw_ref, v_win, low_head)


def _proj(x, mod, layer, row_of, norm_g, w, cos, sin, q_g, k_g, score_bound):
    bsz, t_all, _ = x.shape
    tm = TOKEN_TILE
    tok = lambda width: pl.BlockSpec((1, tm, width), lambda b, t: (b, t, 0))
    pad = pl.BlockSpec((1, 2, 2, tm, 128), lambda b, t: (b, 0, 0, t, 0))
    tab = pl.BlockSpec((tm, 128), lambda b, t: (t, 0))
    pad_shape = jax.ShapeDtypeStruct((bsz, 2, 2, t_all, 128), BF16)
    marker = jnp.zeros((1, 128), F32).at[0, HEAD_DIM].set(1.0)
    return pl.pallas_call(
        _proj_kernel,
        grid=(bsz, t_all // tm),
        in_specs=[
            tok(D_MODEL),
            _mod_spec(layer, 3, row_of),
            _mod_spec(layer, 4, row_of),
            _const_spec((1, D_MODEL)),
            pl.BlockSpec((D_MODEL, PROJ_W), lambda b, t: (0, 0), pipeline_mode=pl.Buffered(1)),
            tab, tab,
            _const_spec((1, 128)), _const_spec((1, 128)), _const_spec((1, 128)), _const_spec((1, 128)),
        ],
        out_specs=[
            tok(256),
            pl.BlockSpec((1, 4, 128, tm), lambda b, t: (b, 0, 0, t)),
            pad, pad,
            pl.BlockSpec((1, 2, tm, 128), lambda b, t: (b, 0, t, 0)),
            pl.BlockSpec((1, 2, V_ROWS, tm), lambda b, t: (b, 0, 0, t)),
            tok(256), tok(256),
        ],
        out_shape=[
            jax.ShapeDtypeStruct((bsz, t_all, 256), BF16),
            jax.ShapeDtypeStruct((bsz, 4, 128, t_all), BF16),
            pad_shape, pad_shape,
            jax.ShapeDtypeStruct((bsz, 2, t_all, 128), BF16),
            jax.ShapeDtypeStruct((bsz, 2, V_ROWS, t_all), BF16),
            jax.ShapeDtypeStruct((bsz, t_all, 256), F32),
            jax.ShapeDtypeStruct((bsz, t_all, 256), F32),
        ],
        compiler_params=_cparams(("parallel", "parallel")),
        name="proj",
    )(x, mod, mod, norm_g.reshape(1, D_MODEL), w, cos, sin,
      jnp.tile(q_g, 2).reshape(1, 128), jnp.tile(k_g, 2).reshape(1, 128),
      marker * (-score_bound * LOG2E), marker)


def _win_kernel(sink_ref, q_ref, kp_ref, kc_ref, kn_ref, kx_ref, vp_ref, vc_ref, vn_ref, vx_ref,
                o_ref, *, n_lat_blocks):
    i = pl.program_id(1)
    r = lax.broadcasted_iota(jnp.int32, (BLOCK, BLOCK), 0)
    c = lax.broadcasted_iota(jnp.int32, (BLOCK, BLOCK), 1)
    d = c - r
    is_lat = i < n_lat_blocks
    far = 4 * BLOCK
    m_prev = d >= jnp.where(jnp.logical_and(is_lat, i >= 1), 0, far)
    m_cur = d >= jnp.where(is_lat, -far, far)
    m_next = (-d) >= jnp.where(jnp.logical_and(is_lat, i + 1 < n_lat_blocks), 0, far)
    for j in range(2):
        q = q_ref[0, :, 128 * j:128 * (j + 1)]
        acc = jnp.zeros((BLOCK, 128), F32)
        for g in range(2):
            s_p = jnp.where(m_prev, _dot_nt(q, kp_ref[0, j, g]), NEG)
            s_c = jnp.where(m_cur, _dot_nt(q, kc_ref[0, j, g]), NEG)
            s_n = jnp.where(m_next, _dot_nt(q, kn_ref[0, j, g]), NEG)
            s_x = _dot_nt(q, kx_ref[0, j, g])
            sink = sink_ref[2 * j + g]
            m = jnp.maximum(jnp.maximum(s_p.max(-1, keepdims=True), s_c.max(-1, keepdims=True)),
                            jnp.maximum(s_n.max(-1, keepdims=True), s_x.max(-1, keepdims=True)))
            m = jnp.maximum(m, sink)
            p_p = jnp.exp(s_p - m)
            p_c = jnp.exp(s_c - m)
            p_n = jnp.exp(s_n - m)
            p_x = jnp.exp(s_x - m)
            den = (p_p.sum(-1, keepdims=True) + p_c.sum(-1, keepdims=True)
                   + p_n.sum(-1, keepdims=True) + p_x.sum(-1, keepdims=True) + jnp.exp(sink - m))
            pv = (_dot(p_p.astype(BF16), vp_ref[0, j, g]) + _dot(p_c.astype(BF16), vc_ref[0, j, g])
                  + _dot(p_n.astype(BF16), vn_ref[0, j, g]) + _dot(p_x.astype(BF16), vx_ref[0, j, g]))
            acc = acc + pv / den
        o_ref[0, :, 128 * j:128 * (j + 1)] = acc.astype(o_ref.dtype)


def _window_attention(q, k_pad, v_pad, sink, n_lat, n_ctx, with_ctx):
    bsz, t_all, _ = q.shape
    n_lat_blocks = n_lat // BLOCK
    n_blocks = t_all // BLOCK
    n_q_blocks = n_blocks if with_ctx else n_lat_blocks
    band = lambda off: pl.BlockSpec(
        (1, 2, 2, BLOCK, 128), lambda b, i: (b, 0, 0, jnp.clip(i + off, 0, n_blocks - 1), 0))
    ctx = pl.BlockSpec((1, 2, 2, n_ctx, 128), lambda b, i: (b, 0, 0, n_lat // n_ctx, 0))
    return pl.pallas_call(
        functools.partial(_win_kernel, n_lat_blocks=n_lat_blocks),
        grid=(bsz, n_q_blocks),
        in_specs=[
            pl.BlockSpec(memory_space=pltpu.SMEM),
            pl.BlockSpec((1, BLOCK, 256), lambda b, i: (b, i, 0)),
            band(-1), band(0), band(1), ctx,
            band(-1), band(0), band(1), ctx,
        ],
        out_specs=pl.BlockSpec((1, BLOCK, 256), lambda b, i: (b, i, 0)),
        out_shape=jax.ShapeDtypeStruct((bsz, n_q_blocks * BLOCK, 256), BF16),
        compiler_params=_cparams(("parallel", "parallel")),
        name="window_attention",
    )(sink, q, k_pad, k_pad, k_pad, k_pad, v_pad, v_pad, v_pad, v_pad)


def _glb_finish(acc_ref, o_ref):
    o_t = jnp.concatenate([acc_ref[g, 0:HEAD_DIM] / acc_ref[g, HEAD_DIM:HEAD_DIM + 1] for g in range(2)],
                          axis=0)
    o_ref[0] = o_t.T.astype(o_ref.dtype)


def _glb_bounded_kernel(qt_ref, k_ref, vt_ref, o_ref, acc_ref):
    ki = pl.program_id(3)

    @pl.when(ki == 0)
    def _():
        acc_ref[...] = jnp.zeros_like(acc_ref)

    for g in range(2):
        p = jnp.exp2(_dot(k_ref[0, 0], qt_ref[0, g])).astype(BF16)
        acc_ref[g] += _dot(vt_ref[0, 0], p)

    @pl.when(ki == pl.num_programs(3) - 1)
    def _():
        _glb_finish(acc_ref, o_ref)


def _glb_online_kernel(qt_ref, k_ref, vt_ref, o_ref, acc_ref, m_ref):
    ki = pl.program_id(3)

    @pl.when(ki == 0)
    def _():
        acc_ref[...] = jnp.zeros_like(acc_ref)
        m_ref[...] = jnp.full_like(m_ref, NEG)

    for g in range(2):
        s = _dot(k_ref[0, 0], qt_ref[0, g])
        m_prev = m_ref[g, 0:1]
        m_new = jnp.maximum(m_prev, s.max(axis=0, keepdims=True))
        p = jnp.exp2(s - m_new).astype(BF16)
        acc_ref[g] = jnp.exp2(m_prev - m_new) * acc_ref[g] + _dot(vt_ref[0, 0], p)
        m_ref[g] = jnp.broadcast_to(m_new, m_ref.shape[1:])

    @pl.when(ki == pl.num_programs(3) - 1)
    def _():
        _glb_finish(acc_ref, o_ref)


def _global_attention(qt, k_aug, vt_aug, q_row0, n_q, k_row0, n_k, tq, tk, bounded):
    bsz = qt.shape[0]
    q0, k0 = q_row0 // tq, k_row0 // tk
    scratch = [pltpu.VMEM((2, V_ROWS, tq), F32)]
    if not bounded:
        scratch.append(pltpu.VMEM((2, 8, tq), F32))
    return pl.pallas_call(
        _glb_bounded_kernel if bounded else _glb_online_kernel,
        grid=(bsz, 2, n_q // tq, n_k // tk),
        in_specs=[
            pl.BlockSpec((1, 2, 128, tq), lambda b, j, qi, ki: (b, j, 0, q0 + qi)),
            pl.BlockSpec((1, 1, tk, 128), lambda b, j, qi, ki: (b, j, k0 + ki, 0)),
            pl.BlockSpec((1, 1, V_ROWS, tk), lambda b, j, qi, ki: (b, j, 0, k0 + ki)),
        ],
        out_specs=pl.BlockSpec((1, tq, 128), lambda b, j, qi, ki: (b, qi, j)),
        out_shape=jax.ShapeDtypeStruct((bsz, n_q, 256), BF16),
        scratch_shapes=scratch,
        compiler_params=_cparams(("parallel", "parallel", "parallel", "arbitrary")),
        name="global_attention" if bounded else "global_attention_online",
    )(qt, k_aug, vt_aug)


def _s5_tile(u, bmat_ref, cmat_ref, a_ref, pc_ref, carry_ref, st_ref, reverse):
    rows = u.shape[0]
    n_groups = rows // SCAN_ROWS
    ns = N_STATE
    st_ref[0:rows, :] = _dot(u.astype(BF16), bmat_ref[0])
    pr = pc_ref[0, :, :ns]
    pi = pc_ref[0, :, ns:]

    def body(n, carry):
        cr, ci = carry
        g = (n_groups - 1 - n) if reverse else n
        row0 = pl.multiple_of(g * SCAN_ROWS, SCAN_ROWS)
        vr = st_ref[pl.ds(row0, SCAN_ROWS), :ns]
        vi = st_ref[pl.ds(row0, SCAN_ROWS), ns:]
        for s, k in enumerate((1, 2, 4)):
            ar = a_ref[0, s, :, :ns]
            ai = a_ref[0, s, :, ns:]
            shift = SCAN_ROWS - k if reverse else k
            sr = pltpu.roll(vr, shift, 0)
            si = pltpu.roll(vi, shift, 0)
            vr, vi = vr + ar * sr - ai * si, vi + ar * si + ai * sr
        vr, vi = vr + pr * cr - pi * ci, vi + pr * ci + pi * cr
        st_ref[pl.ds(row0, SCAN_ROWS), :ns] = vr
        st_ref[pl.ds(row0, SCAN_ROWS), ns:] = vi
        last = 0 if reverse else SCAN_ROWS - 1
        return (jnp.broadcast_to(vr[last:last + 1], (SCAN_ROWS, ns)),
                jnp.broadcast_to(vi[last:last + 1], (SCAN_ROWS, ns)))

    cr, ci = lax.fori_loop(0, n_groups, body, (carry_ref[:, :ns], carry_ref[:, ns:]))
    carry_ref[:, :ns] = cr
    carry_ref[:, ns:] = ci
    return _dot(st_ref[0:rows, :].astype(BF16), cmat_ref[0])


def _s5_kernel(ul_ref, uc_ref, bmat_ref, cmat_ref, a_ref, pc_ref, y_ref, yc_ref, carry_ref, st_ref):
    direction = pl.program_id(0)
    t = pl.program_id(2)
    for reverse in (False, True):
        @pl.when(direction == int(reverse))
        def _(reverse=reverse):
            tile = functools.partial(_s5_tile, bmat_ref=bmat_ref, cmat_ref=cmat_ref, a_ref=a_ref,
                                     pc_ref=pc_ref, carry_ref=carry_ref, st_ref=st_ref, reverse=reverse)

            @pl.when(t == 0)
            def _():
                carry_ref[...] = jnp.zeros_like(carry_ref)
                yc_ref[0, 0] = tile(uc_ref[0])

            y_ref[0, 0] = tile(ul_ref[0])


def _s5_tables(a_re, a_im, log_dt, b_re, b_im, c_re, c_im):
    dt = jnp.exp(log_dt)[..., None]
    re_dt, im_dt = a_re * dt, a_im * dt
    lb_re, lb_im = jnp.exp(re_dt) * jnp.cos(im_dt), jnp.exp(re_dt) * jnp.sin(im_dt)
    den = a_re * a_re + a_im * a_im
    f_re = ((lb_re - 1.0) * a_re + lb_im * a_im) / den
    f_im = (lb_im * a_re - (lb_re - 1.0) * a_im) / den
    bb_re = f_re[..., None] * b_re - f_im[..., None] * b_im
    bb_im = f_re[..., None] * b_im + f_im[..., None] * b_re
    eye = jnp.eye(SSM_GROUPS, dtype=F32)
    in_w, st_w = SSM_GROUPS * SSM_GROUP, N_STATE
    b_blk = lambda z: jnp.einsum('dgph,gk->dghkp', z, eye).reshape(2, in_w, st_w)
    bmat = jnp.concatenate([b_blk(bb_re), b_blk(bb_im)], axis=-1).astype(BF16)
    c_blk = lambda z: jnp.einsum('dghp,gk->dgpkh', z, eye).reshape(2, st_w, in_w)
    cmat = jnp.concatenate([c_blk(c_re), -c_blk(c_im)], axis=1).astype(BF16)
    steps = jnp.arange(1, SCAN_ROWS + 1, dtype=F32)[None, :, None, None]
    mag = jnp.exp(re_dt[:, None] * steps)
    ang = im_dt[:, None] * steps
    powers = jnp.concatenate([(mag * jnp.cos(ang)).reshape(2, SCAN_ROWS, st_w),
                              (mag * jnp.sin(ang)).reshape(2, SCAN_ROWS, st_w)], axis=-1)
    row = jnp.arange(SCAN_ROWS)
    pc = jnp.stack([powers[0], powers[1][::-1]])
    a_tabs = []
    for k in (1, 2, 4):
        fwd = jnp.where((row >= k)[:, None], powers[0, k - 1][None], 0.0)
        bwd = jnp.where((row <= SCAN_ROWS - 1 - k)[:, None], powers[1, k - 1][None], 0.0)
        a_tabs.append(jnp.stack([fwd, bwd]))
    return bmat, cmat, jnp.stack(a_tabs, axis=1), pc


def _s5_scan(u, tables, n_lat, n_ctx):
    bsz = u.shape[0]
    tm = TOKEN_TILE
    assert n_ctx <= tm and n_ctx % SCAN_ROWS == 0
    nt = n_lat // tm
    bmat, cmat, a_tab, pc = tables
    ns2 = 2 * N_STATE
    tile_of = lambda d, t: t + d * (nt - 1 - 2 * t)
    return pl.pallas_call(
        _s5_kernel,
        grid=(2, bsz, nt),
        in_specs=[
            pl.BlockSpec((1, tm, 256), lambda d, b, t: (b, tile_of(d, t), 0)),
            pl.BlockSpec((1, n_ctx, 256), lambda d, b, t: (b, n_lat // n_ctx, 0)),
            pl.BlockSpec((1, 256, ns2), lambda d, b, t: (d, 0, 0)),
            pl.BlockSpec((1, ns2, 256), lambda d, b, t: (d, 0, 0)),
            pl.BlockSpec((1, 3, SCAN_ROWS, ns2), lambda d, b, t: (d, 0, 0, 0)),
            pl.BlockSpec((1, SCAN_ROWS, ns2), lambda d, b, t: (d, 0, 0)),
        ],
        out_specs=[
            pl.BlockSpec((1, 1, tm, 256), lambda d, b, t: (d, b, tile_of(d, t), 0)),
            pl.BlockSpec((1, 1, n_ctx, 256), lambda d, b, t: (d, b, 0, 0)),
        ],
        out_shape=[
            jax.ShapeDtypeStruct((2, bsz, n_lat, 256), F32),
            jax.ShapeDtypeStruct((2, bsz, n_ctx, 256), F32),
        ],
        scratch_shapes=[pltpu.VMEM((SCAN_ROWS, ns2), F32), pltpu.VMEM((tm, ns2), F32)],
        compiler_params=_cparams(("parallel", "parallel", "arbitrary")),
        name="s5_scan",
    )(u, u, bmat, cmat, a_tab, pc)


def _gelu_tanh(x):
    c = math.sqrt(2.0 / math.pi)
    return x * (0.5 * (1.0 + jnp.tanh(c * (x + 0.044715 * (x * x * x)))))


def _merge_kernel(x_ref, sh_ref, sc_ref, gt_ref, ng_ref, xa_ref, xap_ref, xan_ref, yb_ref, yf_ref, yr_ref,
                  u_ref, yd_ref, wg_ref, pw_ref, ps_ref, dsk_ref, glu_ref, wb_ref, wo_ref, o_ref, ext_ref,
                  *, n_lat_tiles, n_lat, n_ctx):
    t = pl.program_id(1)
    x = x_ref[0]
    tm = x.shape[0]
    bw = BRANCH_W
    is_ctx = t >= n_lat_tiles
    seg_tile = jnp.where(is_ctx, t - n_lat_tiles, t)
    seg_tiles = jnp.where(is_ctx, n_ctx // tm, n_lat_tiles)
    seg_len = jnp.where(is_ctx, n_ctx, n_lat)
    xa = xa_ref[0]
    ext_ref[0:POOL_HALO, :] = jnp.where(seg_tile > 0, xap_ref[0], 0.0)
    ext_ref[POOL_HALO:POOL_HALO + tm, :] = xa
    ext_ref[POOL_HALO + tm:, :] = jnp.where(seg_tile + 1 < seg_tiles, xan_ref[0], 0.0)
    at = lambda off: ext_ref[POOL_HALO + off:POOL_HALO + off + tm, :]
    sums = []
    acc = at(-1) + xa
    sums.append(acc)
    for half in (2, 4, 8):
        for off in range(half // 2, half):
            acc = acc + at(-off - 1) + at(off)
        sums.append(acc)
    lane = lax.broadcasted_iota(jnp.int32, (tm, bw), 1)
    pos = seg_tile * tm + lax.broadcasted_iota(jnp.int32, (tm, bw), 0)
    group = lane // (bw // len(POOL_WINDOWS))
    half_w = jnp.left_shift(1, group)
    count = jnp.minimum(pos + half_w, seg_len) - jnp.maximum(pos - half_w, 0)
    window_sum = jnp.where(group == 0, sums[0], jnp.where(group == 1, sums[1],
                                                           jnp.where(group == 2, sums[2], sums[3])))
    diff = window_sum / count.astype(F32) - xa
    y_a = _dot(diff.astype(BF16), pw_ref[...]) * ps_ref[...]
    y_s = yf_ref[0, 0] + yr_ref[0, 0] + dsk_ref[...] * u_ref[0]
    z = _dot(_gelu_tanh(y_s).astype(BF16), glu_ref[...])
    y_c = z[:, :bw] * jax.nn.sigmoid(z[:, bw:])
    h = (_rms(x, ng_ref[...]) * (1.0 + sc_ref[...]) + sh_ref[...]).astype(BF16)
    branches = (y_a.astype(BF16), yb_ref[0], y_c.astype(BF16), yd_ref[0])
    total = None
    for k, y_k in enumerate(branches):
        gate = jax.nn.sigmoid(_dot(h, wg_ref[:, D_MODEL * k:D_MODEL * (k + 1)]))
        term = gate * _dot(y_k, wb_ref[k])
        total = term if total is None else total + term
    o_ref[0] = x + gt_ref[...] * _dot(total.astype(BF16), wo_ref[...])


def _merge(x, mod, layer, row_of, norm_g, xa, y_b, y_s5, u, y_d, w_gate, pool_bd, pool_scale, d_skip,
           glu_w, branch_w, out_w, n_rows, n_lat, n_ctx):
    bsz = x.shape[0]
    tm = TOKEN_TILE
    assert n_ctx % tm == 0 or n_rows == n_lat
    halo_blocks = xa.shape[1] // POOL_HALO
    per_tile = tm // POOL_HALO
    tok = lambda width: pl.BlockSpec((1, tm, width), lambda b, t: (b, t, 0))
    single = dict(pipeline_mode=pl.Buffered(1))
    return pl.pallas_call(
        functools.partial(_merge_kernel, n_lat_tiles=n_lat // tm, n_lat=n_lat, n_ctx=n_ctx),
        grid=(bsz, n_rows // tm),
        in_specs=[
            tok(D_MODEL),
            _mod_spec(layer, 3, row_of), _mod_spec(layer, 4, row_of), _mod_spec(layer, 5, row_of),
            _const_spec((1, D_MODEL)),
            tok(256),
            pl.BlockSpec((1, POOL_HALO, 256), lambda b, t: (b, jnp.maximum(t * per_tile - 1, 0), 0)),
            pl.BlockSpec((1, POOL_HALO, 256),
                         lambda b, t: (b, jnp.minimum((t + 1) * per_tile, halo_blocks - 1), 0)),
            tok(256),
            pl.BlockSpec((1, 1, tm, 256), lambda b, t: (0, b, t, 0)),
            pl.BlockSpec((1, 1, tm, 256), lambda b, t: (1, b, t, 0)),
            tok(256), tok(256),
            pl.BlockSpec((D_MODEL, N_BRANCH * D_MODEL), lambda b, t: (0, 0), **single),
            _const_spec((256, 256)), _const_spec((1, 256)), _const_spec((1, 256)),
            _const_spec((256, 512)),
            pl.BlockSpec((N_BRANCH, 256, D_MODEL), lambda b, t: (0, 0, 0), **single),
            pl.BlockSpec((D_MODEL, D_MODEL), lambda b, t: (0, 0), **single),
        ],
        out_specs=tok(D_MODEL),
        out_shape=jax.ShapeDtypeStruct((bsz, n_rows, D_MODEL), F32),
        scratch_shapes=[pltpu.VMEM((tm + 2 * POOL_HALO, 256), F32)],
        compiler_params=_cparams(("parallel", "parallel")),
        name="merge",
    )(x, mod, mod, mod, norm_g.reshape(1, D_MODEL), xa, xa, xa, y_b, y_s5, y_s5, u, y_d,
      w_gate, pool_bd, pool_scale.reshape(1, 256), d_skip.reshape(1, 256), glu_w, branch_w, out_w)


def _rope_tables(n_lat, n_ctx):
    rows = n_lat // GRID_W
    n_freq = HEAD_DIM // 4
    row = jnp.repeat(jnp.arange(rows), GRID_W)
    col = jnp.tile(jnp.arange(GRID_W), rows)
    inv = ROPE_THETA ** (-jnp.arange(n_freq, dtype=F32) / n_freq)
    ang = jnp.stack([row, col], axis=-1).astype(F32)[..., None] * inv
    cos = jnp.broadcast_to(jnp.cos(ang)[:, :, None, :], (n_lat, 2, 2, n_freq)).reshape(n_lat, HEAD_DIM)
    sign = jnp.array([-1.0, 1.0], F32)[None, None, :, None]
    sin = (jnp.sin(ang)[:, :, None, :] * sign).reshape(n_lat, HEAD_DIM)
    cos = jnp.concatenate([cos, jnp.ones((n_ctx, HEAD_DIM), F32)], axis=0)
    sin = jnp.concatenate([sin, jnp.zeros((n_ctx, HEAD_DIM), F32)], axis=0)
    return jnp.tile(cos, (1, 2)), jnp.tile(sin, (1, 2))


def _largest_divisor(n, candidates):
    for c in candidates:
        if n % c == 0:
            return c
    raise ValueError(f"no tile in {candidates} divides {n}")


def _pool_block_diag(pool_w):
    groups, width, _ = pool_w.shape
    eye = jnp.eye(groups, dtype=pool_w.dtype)
    return jnp.einsum('gcd,gk->gckd', pool_w, eye).reshape(groups * width, groups * width)


def kernel(x, c, ctx, c_ctx, w_mod, b_mod, norm_g, ffn_in, ffn_out, w_in, win_sink, qk_norm, pool_w, pool_scale, ssm_a_re, ssm_a_im, ssm_log_dt, ssm_b_re, ssm_b_im, ssm_c_re, ssm_c_im, ssm_d, glu_w, branch_w, out_w, final_g):
    bsz, n_lat, d = x.shape
    n_ctx = ctx.shape[1]
    depth = w_mod.shape[0]
    t_all = n_lat + n_ctx
    assert d == D_MODEL and bsz < MOD_ROWS and n_lat % TOKEN_TILE == 0 and n_ctx == TOKEN_TILE
    assert n_lat % GRID_W == 0

    cond_rows = jnp.zeros((MOD_ROWS, d), F32).at[:bsz].set(c).at[bsz].set(c_ctx)
    mod = _modulation(cond_rows, w_mod, b_mod)
    n_lat_tiles = n_lat // TOKEN_TILE
    row_of = lambda b, t: jnp.where(t >= n_lat_tiles, bsz, b)
    lat_row = lambda b, t: b

    cos, sin = _rope_tables(n_lat, n_ctx)
    tq = _largest_divisor(n_lat, (1024, 512, 256))
    tk = _largest_divisor(t_all, (1280, 1024, 768, 512, 256))

    xs = jnp.concatenate([x, ctx], axis=1)
    for l in range(depth):
        last = l == depth - 1
        bf = lambda w: w.astype(BF16)
        xs = _ffn(xs, mod, l, 0, row_of, norm_g[l, 0], bf(ffn_in[l, 0]), bf(ffn_out[l, 0]), t_all)

        score_bound = 1.02 * HEAD_DIM ** 0.5 * jnp.max(jnp.abs(qk_norm[l, 0])) * jnp.max(jnp.abs(qk_norm[l, 1]))
        qw, qt, kw, vw, ka, vt, u, xa = _proj(xs, mod, l, row_of, norm_g[l, 1], bf(w_in[l, :, :PROJ_W]),
                                              cos, sin, qk_norm[l, 0], qk_norm[l, 1], score_bound)
        y_b = _window_attention(qw, kw, vw, win_sink[l], n_lat, n_ctx, not last)

        def global_attention(q_row0, n_q, k_row0, n_k, tq_, tk_):
            run = lambda bounded: lambda: _global_attention(qt, ka, vt, q_row0, n_q, k_row0, n_k, tq_, tk_,
                                                            bounded)
            return lax.cond(score_bound <= MAX_SCORE_BOUND, run(True), run(False))

        y_d = global_attention(0, n_lat, 0, t_all, tq, tk)
        tables = _s5_tables(ssm_a_re[l], ssm_a_im[l], ssm_log_dt[l], ssm_b_re[l], ssm_b_im[l],
                            ssm_c_re[l], ssm_c_im[l])
        y_s5, y_s5_ctx = _s5_scan(u, tables, n_lat, n_ctx)
        n_rows = n_lat
        if not last:
            y_d_ctx = global_attention(n_lat, n_ctx, n_lat, n_ctx, n_ctx, n_ctx)
            y_d = jnp.concatenate([y_d, y_d_ctx], axis=1)
            y_s5 = jnp.concatenate([y_s5, y_s5_ctx], axis=2)
            n_rows = t_all
        xs = _merge(xs, mod, l, row_of, norm_g[l, 1], xa, y_b, y_s5, u, y_d, bf(w_in[l, :, PROJ_W:]),
                    bf(_pool_block_diag(pool_w[l])), pool_scale[l], ssm_d[l], bf(glu_w[l]), bf(branch_w[l]),
                    bf(out_w[l]), n_rows, n_lat, n_ctx)
        xs = _ffn(xs, mod, l, 2, row_of if not last else lat_row, norm_g[l, 2], bf(ffn_in[l, 1]),
                  bf(ffn_out[l, 1]), n_rows, final_g=final_g if last else None)
    return xs
```

```python
import functools
import math

import jax
import jax.numpy as jnp
import numpy as np
from jax import lax
from jax.experimental import pallas as pl
from jax.experimental.pallas import tpu as pltpu

F32 = jnp.float32
BF16 = jnp.bfloat16

D_MODEL = 1024
D_FF = 2816
N_SUB = 3
BRANCH_W = 256
HEAD_DIM = 64
GRID_W = 64
BLOCK = 128
EPS = 1e-6
ROPE_THETA = 10000.0
N_BRANCH = 4
POOL_WINDOWS = (2, 4, 8, 16)
SSM_GROUPS = 16
SSM_GROUP = 16
SSM_STATE = 64
N_STATE = SSM_GROUPS * SSM_STATE
PROJ_W = 6 * BRANCH_W
MOD_ROWS = 8
POOL_HALO = 8
SCAN_ROWS = 8
STATE_CHUNK = 128
WIN_BLOCKS = 2
NEG = -1e30
QK_SCALE = HEAD_DIM ** -0.5
LOG2E = math.log2(math.e)
V_ROWS = HEAD_DIM + 16
MAX_SCORE_BOUND = 40.0
TOKEN_TILE = 256
VMEM_LIMIT = 56 * 1024 * 1024


def _cparams(sem):
    return pltpu.CompilerParams(dimension_semantics=sem, vmem_limit_bytes=VMEM_LIMIT)


def _dot(a, b):
    return jnp.dot(a, b, preferred_element_type=F32)


def _dot_nt(a, b):
    return lax.dot_general(a, b, (((1,), (1,)), ((), ())), preferred_element_type=F32)


def _rms(x, g):
    ms = jnp.mean(x * x, axis=-1, keepdims=True)
    return x * lax.rsqrt(ms + EPS) * g


def _silu(x):
    return x * jax.nn.sigmoid(x)


def _const_spec(shape):
    nd = len(shape)
    return pl.BlockSpec(shape, lambda *_: (0,) * nd)


def _mod_kernel(s_ref, w_ref, b_ref, o_ref):
    s = _silu(s_ref[...])
    w = w_ref[0]
    s_hi = s.astype(BF16)
    s_lo = (s - s_hi.astype(F32)).astype(BF16)
    w_hi = w.astype(BF16)
    w_lo = (w - w_hi.astype(F32)).astype(BF16)
    o_ref[0] = _dot(s_hi, w_hi) + _dot(s_hi, w_lo) + _dot(s_lo, w_hi) + b_ref[0]


def _modulation(rows, w_mod, b_mod):
    depth, d, width = w_mod.shape
    tn = 1024
    out = pl.pallas_call(
        _mod_kernel,
        grid=(depth, width // tn),
        in_specs=[
            pl.BlockSpec((MOD_ROWS, d), lambda l, j: (0, 0)),
            pl.BlockSpec((1, d, tn), lambda l, j: (l, 0, j)),
            pl.BlockSpec((1, 1, tn), lambda l, j: (l, 0, j)),
        ],
        out_specs=pl.BlockSpec((1, MOD_ROWS, tn), lambda l, j: (l, 0, j)),
        out_shape=jax.ShapeDtypeStruct((depth, MOD_ROWS, width), F32),
        compiler_params=_cparams(("parallel", "parallel")),
        name="modulation",
    )(rows, w_mod, b_mod.reshape(depth, 1, width))
    return out.reshape(depth, MOD_ROWS, N_SUB * 3, 1, d).transpose(0, 2, 1, 3, 4)


def _mod_spec(layer, vec, row_of):
    return pl.BlockSpec((None, None, None, 1, D_MODEL), lambda b, t: (layer, vec, row_of(b, t), 0, 0))


def _ffn_kernel(x_ref, sh_ref, sc_ref, gt_ref, ng_ref, win_ref, wout_ref, *rest, final):
    o_ref = rest[-1]
    x = x_ref[0]
    h = (_rms(x, ng_ref[...]) * (1.0 + sc_ref[...]) + sh_ref[...]).astype(BF16)
    gu = _dot(h, win_ref[...])
    a = (_silu(gu[:, :D_FF]) * gu[:, D_FF:]).astype(BF16)
    xn = x + (0.5 * gt_ref[...]) * _dot(a, wout_ref[...])
    if final:
        xn = _rms(xn, rest[0][...])
    o_ref[0] = xn


def _ffn(x, mod, layer, sub, row_of, norm_g, w_in, w_out, n_rows, final_g=None):
    bsz = x.shape[0]
    tm = TOKEN_TILE
    final = final_g is not None
    in_specs = [
        pl.BlockSpec((1, tm, D_MODEL), lambda b, t: (b, t, 0)),
        _mod_spec(layer, sub * 3 + 0, row_of),
        _mod_spec(layer, sub * 3 + 1, row_of),
        _mod_spec(layer, sub * 3 + 2, row_of),
        _const_spec((1, D_MODEL)),
        pl.BlockSpec((D_MODEL, 2 * D_FF), lambda b, t: (0, 0), pipeline_mode=pl.Buffered(1)),
        pl.BlockSpec((D_FF, D_MODEL), lambda b, t: (0, 0), pipeline_mode=pl.Buffered(1)),
    ]
    args = [x, mod, mod, mod, norm_g.reshape(1, D_MODEL), w_in, w_out]
    if final:
        in_specs.append(_const_spec((1, D_MODEL)))
        args.append(final_g.reshape(1, D_MODEL))
    return pl.pallas_call(
        functools.partial(_ffn_kernel, final=final),
        grid=(bsz, n_rows // tm),
        in_specs=in_specs,
        out_specs=pl.BlockSpec((1, tm, D_MODEL), lambda b, t: (b, t, 0)),
        out_shape=jax.ShapeDtypeStruct((bsz, n_rows, D_MODEL), F32),
        compiler_params=_cparams(("parallel", "parallel")),
        name="ffn_final" if final else "ffn",
    )(*args)


def _rope(x, cos, sin, first_half):
    w = x.shape[-1]
    swapped = jnp.where(first_half, pltpu.roll(x, w - 16, 1), pltpu.roll(x, 16, 1))
    return x * cos + swapped * sin


def _head_rms(x, g, low_head):
    x2 = x * x
    lo = jnp.sum(jnp.where(low_head, x2, 0.0), axis=-1, keepdims=True)
    hi = jnp.sum(jnp.where(low_head, 0.0, x2), axis=-1, keepdims=True)
    ms = jnp.where(low_head, lo, hi) * (1.0 / HEAD_DIM)
    return x * lax.rsqrt(ms + EPS) * g


def _store_padded(ref, x, low_head):
    sw = pltpu.roll(x, HEAD_DIM, 1)
    zero = jnp.zeros_like(x)
    ref[0, 0, 0] = jnp.where(low_head, x, zero).astype(ref.dtype)
    ref[0, 0, 1] = jnp.where(low_head, zero, sw).astype(ref.dtype)
    ref[0, 1, 0] = jnp.where(low_head, sw, zero).astype(ref.dtype)
    ref[0, 1, 1] = jnp.where(low_head, zero, x).astype(ref.dtype)


def _proj_kernel(x_ref, sh_ref, sc_ref, ng_ref, w_ref, cos_ref, sin_ref, qg_ref, kg_ref, qb_ref, kvb_ref,
                 qw_ref, qt_ref, kw_ref, vw_ref, ka_ref, vt_ref, u_ref, xa_ref):
    x = x_ref[0]
    tm = x.shape[0]
    h = (_rms(x, ng_ref[...]) * (1.0 + sc_ref[...]) + sh_ref[...]).astype(BF16)
    p = _dot(h, w_ref[...])
    lane = lax.broadcasted_iota(jnp.int32, (tm, 2 * HEAD_DIM), 1)
    low_head = lane < HEAD_DIM
    first_half = jnp.bitwise_and(lane, 31) < 16
    cos = cos_ref[...]
    sin = sin_ref[...]
    rope = lambda z: _rope(z, cos, sin, first_half)
    k_win = rope(p[:, 0:128])
    v_win = p[:, 128:256]
    u_ref[0] = p[:, 256:512]
    k_glb = rope(_head_rms(p[:, 512:640], kg_ref[...], low_head))
    v_glb = p[:, 640:768]
    heads = lambda z: (z, pltpu.roll(z, HEAD_DIM, 1))
    for c in range(2):
        qw = rope(p[:, 768 + 128 * c:896 + 128 * c]) * QK_SCALE
        qw_ref[0, :, 128 * c:128 * (c + 1)] = qw.astype(BF16)
        qd = rope(_head_rms(p[:, 1024 + 128 * c:1152 + 128 * c], qg_ref[...], low_head)) * (QK_SCALE * LOG2E)
        for g, z in enumerate(heads(qd)):
            qt_ref[0, 2 * c + g] = jnp.where(low_head, z, qb_ref[...]).T.astype(BF16)
    for j, (kz, vz) in enumerate(zip(heads(k_glb), heads(v_glb))):
        ka_ref[0, j] = jnp.where(low_head, kz, kvb_ref[...]).astype(BF16)
        vt_ref[0, j] = jnp.where(low_head, vz, kvb_ref[...]).T[0:V_ROWS].astype(BF16)
    xa_ref[0] = p[:, 1280:1536]
    _store_padded(kw_ref, k_win, low_head)
    _store_padded(vw_ref, v_win, low_head)


def _proj(x, mod, layer, row_of, norm_g, w, cos, sin, q_g, k_g, score_bound):
    bsz, t_all, _ = x.shape
    tm = TOKEN_TILE
    tok = lambda width: pl.BlockSpec((1, tm, width), lambda b, t: (b, t, 0))
    pad = pl.BlockSpec((1, 2, 2, tm, 128), lambda b, t: (b, 0, 0, t, 0))
    tab = pl.BlockSpec((tm, 128), lambda b, t: (t, 0))
    pad_shape = jax.ShapeDtypeStruct((bsz, 2, 2, t_all, 128), BF16)
    marker = jnp.zeros((1, 128), F32).at[0, HEAD_DIM].set(1.0)
    return pl.pallas_call(
        _proj_kernel,
        grid=(bsz, t_all // tm),
        in_specs=[
            tok(D_MODEL),
            _mod_spec(layer, 3, row_of),
            _mod_spec(layer, 4, row_of),
            _const_spec((1, D_MODEL)),
            pl.BlockSpec((D_MODEL, PROJ_W), lambda b, t: (0, 0), pipeline_mode=pl.Buffered(1)),
            tab, tab,
            _const_spec((1, 128)), _const_spec((1, 128)), _const_spec((1, 128)), _const_spec((1, 128)),
        ],
        out_specs=[
            tok(256),
            pl.BlockSpec((1, 4, 128, tm), lambda b, t: (b, 0, 0, t)),
            pad, pad,
            pl.BlockSpec((1, 2, tm, 128), lambda b, t: (b, 0, t, 0)),
            pl.BlockSpec((1, 2, V_ROWS, tm), lambda b, t: (b, 0, 0, t)),
            tok(256), tok(256),
        ],
        out_shape=[
            jax.ShapeDtypeStruct((bsz, t_all, 256), BF16),
            jax.ShapeDtypeStruct((bsz, 4, 128, t_all), BF16),
            pad_shape, pad_shape,
            jax.ShapeDtypeStruct((bsz, 2, t_all, 128), BF16),
            jax.ShapeDtypeStruct((bsz, 2, V_ROWS, t_all), BF16),
            jax.ShapeDtypeStruct((bsz, t_all, 256), F32),
            jax.ShapeDtypeStruct((bsz, t_all, 256), F32),
        ],
        compiler_params=_cparams(("parallel", "parallel")),
        name="proj",
    )(x, mod, mod, norm_g.reshape(1, D_MODEL), w, cos, sin,
      jnp.tile(q_g, 2).reshape(1, 128), jnp.tile(k_g, 2).reshape(1, 128),
      marker * (-score_bound * LOG2E), marker)


def _win_kernel(sink_ref, q_ref, kp_ref, kc_ref, kn_ref, kx_ref, vp_ref, vc_ref, vn_ref, vx_ref,
                o_ref, *, n_lat_blocks):
    i = pl.program_id(1)
    r = lax.broadcasted_iota(jnp.int32, (BLOCK, BLOCK), 0)
    c = lax.broadcasted_iota(jnp.int32, (BLOCK, BLOCK), 1)
    d = c - r
    far = 4 * BLOCK
    for sub in range(WIN_BLOCKS):
        blk = i * WIN_BLOCKS + sub
        rows = slice(BLOCK * sub, BLOCK * (sub + 1))
        is_lat = blk < n_lat_blocks
        m_prev = d >= jnp.where(jnp.logical_and(is_lat, blk >= 1), 0, far)
        m_cur = d >= jnp.where(is_lat, -far, far)
        m_next = (-d) >= jnp.where(jnp.logical_and(is_lat, blk + 1 < n_lat_blocks), 0, far)
        before = slice(BLOCK * (sub - 1), BLOCK * sub)
        after = slice(BLOCK * (sub + 1), BLOCK * (sub + 2))
        for j in range(2):
            q = q_ref[0, rows, 128 * j:128 * (j + 1)]
            acc = jnp.zeros((BLOCK, 128), F32)
            for g in range(2):
                prev = lambda ref, edge: edge[0, j, g] if sub == 0 else ref[0, j, g, before]
                nxt = lambda ref, edge: edge[0, j, g] if sub == WIN_BLOCKS - 1 else ref[0, j, g, after]
                s = jnp.concatenate([
                    jnp.where(m_prev, _dot_nt(q, prev(kc_ref, kp_ref)), NEG),
                    jnp.where(m_cur, _dot_nt(q, kc_ref[0, j, g, rows]), NEG),
                    jnp.where(m_next, _dot_nt(q, nxt(kc_ref, kn_ref)), NEG),
                    _dot_nt(q, kx_ref[0, j, g])], axis=1)
                v = jnp.concatenate([prev(vc_ref, vp_ref), vc_ref[0, j, g, rows], nxt(vc_ref, vn_ref),
                                     vx_ref[0, j, g]], axis=0)
                sink = sink_ref[2 * j + g]
                m = jnp.maximum(s.max(-1, keepdims=True), sink)
                p = jnp.exp(s - m)
                den = p.sum(-1, keepdims=True) + jnp.exp(sink - m)
                acc = acc + _dot(p.astype(BF16), v) / den
            o_ref[0, rows, 128 * j:128 * (j + 1)] = acc.astype(o_ref.dtype)


def _window_attention(q, k_pad, v_pad, sink, n_lat, n_ctx, with_ctx):
    bsz, t_all, _ = q.shape
    n_lat_blocks = n_lat // BLOCK
    n_blocks = t_all // BLOCK
    n_q_blocks = n_blocks if with_ctx else n_lat_blocks
    assert n_q_blocks % WIN_BLOCKS == 0
    tile = WIN_BLOCKS * BLOCK
    edge = lambda off: pl.BlockSpec(
        (1, 2, 2, BLOCK, 128), lambda b, i: (b, 0, 0, jnp.clip(i * WIN_BLOCKS + off, 0, n_blocks - 1), 0))
    cur = pl.BlockSpec((1, 2, 2, tile, 128), lambda b, i: (b, 0, 0, i, 0))
    ctx = pl.BlockSpec((1, 2, 2, n_ctx, 128), lambda b, i: (b, 0, 0, n_lat // n_ctx, 0))
    return pl.pallas_call(
        functools.partial(_win_kernel, n_lat_blocks=n_lat_blocks),
        grid=(bsz, n_q_blocks // WIN_BLOCKS),
        in_specs=[
            pl.BlockSpec(memory_space=pltpu.SMEM),
            pl.BlockSpec((1, tile, 256), lambda b, i: (b, i, 0)),
            edge(-1), cur, edge(WIN_BLOCKS), ctx,
            edge(-1), cur, edge(WIN_BLOCKS), ctx,
        ],
        out_specs=pl.BlockSpec((1, tile, 256), lambda b, i: (b, i, 0)),
        out_shape=jax.ShapeDtypeStruct((bsz, n_q_blocks * BLOCK, 256), BF16),
        compiler_params=_cparams(("parallel", "parallel")),
        name="window_attention",
    )(sink, q, k_pad, k_pad, k_pad, k_pad, v_pad, v_pad, v_pad, v_pad)


def _glb_finish(acc_ref, o_ref):
    o_t = jnp.concatenate([acc_ref[g, 0:HEAD_DIM] / acc_ref[g, HEAD_DIM:HEAD_DIM + 1] for g in range(2)],
                          axis=0)
    o_ref[0] = o_t.T.astype(o_ref.dtype)


def _glb_bounded_kernel(qt_ref, k_ref, vt_ref, o_ref, acc_ref):
    ki = pl.program_id(3)

    @pl.when(ki == 0)
    def _():
        acc_ref[...] = jnp.zeros_like(acc_ref)

    for g in range(2):
        p = jnp.exp2(_dot(k_ref[0, 0], qt_ref[0, g])).astype(BF16)
        acc_ref[g] += _dot(vt_ref[0, 0], p)

    @pl.when(ki == pl.num_programs(3) - 1)
    def _():
        _glb_finish(acc_ref, o_ref)


def _glb_online_kernel(qt_ref, k_ref, vt_ref, o_ref, acc_ref, m_ref):
    ki = pl.program_id(3)

    @pl.when(ki == 0)
    def _():
        acc_ref[...] = jnp.zeros_like(acc_ref)
        m_ref[...] = jnp.full_like(m_ref, NEG)

    for g in range(2):
        s = _dot(k_ref[0, 0], qt_ref[0, g])
        m_prev = m_ref[g, 0:1]
        m_new = jnp.maximum(m_prev, s.max(axis=0, keepdims=True))
        p = jnp.exp2(s - m_new).astype(BF16)
        acc_ref[g] = jnp.exp2(m_prev - m_new) * acc_ref[g] + _dot(vt_ref[0, 0], p)
        m_ref[g] = jnp.broadcast_to(m_new, m_ref.shape[1:])

    @pl.when(ki == pl.num_programs(3) - 1)
    def _():
        _glb_finish(acc_ref, o_ref)


def _global_attention(qt, k_aug, vt_aug, q_row0, n_q, k_row0, n_k, tq, tk, bounded):
    bsz = qt.shape[0]
    q0, k0 = q_row0 // tq, k_row0 // tk
    scratch = [pltpu.VMEM((2, V_ROWS, tq), F32)]
    if not bounded:
        scratch.append(pltpu.VMEM((2, 8, tq), F32))
    return pl.pallas_call(
        _glb_bounded_kernel if bounded else _glb_online_kernel,
        grid=(bsz, 2, n_q // tq, n_k // tk),
        in_specs=[
            pl.BlockSpec((1, 2, 128, tq), lambda b, j, qi, ki: (b, j, 0, q0 + qi)),
            pl.BlockSpec((1, 1, tk, 128), lambda b, j, qi, ki: (b, j, k0 + ki, 0)),
            pl.BlockSpec((1, 1, V_ROWS, tk), lambda b, j, qi, ki: (b, j, 0, k0 + ki)),
        ],
        out_specs=pl.BlockSpec((1, tq, 128), lambda b, j, qi, ki: (b, qi, j)),
        out_shape=jax.ShapeDtypeStruct((bsz, n_q, 256), BF16),
        scratch_shapes=scratch,
        compiler_params=_cparams(("parallel", "parallel", "parallel", "arbitrary")),
        name="global_attention" if bounded else "global_attention_online",
    )(qt, k_aug, vt_aug)


def _s5_tile(u, bmat_ref, cmat_ref, a_ref, pc_ref, carry_ref, reverse):
    rows = u.shape[0]
    n_groups = rows // SCAN_ROWS
    w = STATE_CHUNK
    ub = u.astype(BF16)
    order = range(n_groups - 1, -1, -1) if reverse else range(n_groups)
    last = 0 if reverse else SCAN_ROWS - 1
    y = None
    for c in range(N_STATE // w):
        bu = _dot(ub, bmat_ref[0, c])
        pr, pi = pc_ref[0, c, :, :w], pc_ref[0, c, :, w:]
        cr, ci = carry_ref[c, :, :w], carry_ref[c, :, w:]
        states = [None] * n_groups
        for g in order:
            vr = bu[SCAN_ROWS * g:SCAN_ROWS * (g + 1), :w]
            vi = bu[SCAN_ROWS * g:SCAN_ROWS * (g + 1), w:]
            for s, k in enumerate((1, 2, 4)):
                ar, ai = a_ref[0, s, c, :, :w], a_ref[0, s, c, :, w:]
                shift = SCAN_ROWS - k if reverse else k
                sr = pltpu.roll(vr, shift, 0)
                si = pltpu.roll(vi, shift, 0)
                vr, vi = vr + ar * sr - ai * si, vi + ar * si + ai * sr
            vr, vi = vr + pr * cr - pi * ci, vi + pr * ci + pi * cr
            cr = jnp.broadcast_to(vr[last:last + 1], (SCAN_ROWS, w))
            ci = jnp.broadcast_to(vi[last:last + 1], (SCAN_ROWS, w))
            states[g] = jnp.concatenate([vr, vi], axis=1)
        carry_ref[c, :, :w] = cr
        carry_ref[c, :, w:] = ci
        part = _dot(jnp.concatenate(states, axis=0).astype(BF16), cmat_ref[0, c])
        y = part if y is None else y + part
    return y


def _s5_kernel(u_ref, bmat_ref, cmat_ref, a_ref, pc_ref, y_ref, carry_ref):
    direction = pl.program_id(0)
    step = pl.program_id(2)

    @pl.when(step == 0)
    def _():
        carry_ref[...] = jnp.zeros_like(carry_ref)

    for reverse in (False, True):
        @pl.when(direction == int(reverse))
        def _(reverse=reverse):
            y_ref[0, 0] = _s5_tile(u_ref[0], bmat_ref, cmat_ref, a_ref, pc_ref, carry_ref, reverse)


def _s5_tables(a_re, a_im, log_dt, b_re, b_im, c_re, c_im):
    dt = jnp.exp(log_dt)[..., None]
    re_dt, im_dt = a_re * dt, a_im * dt
    lb_re, lb_im = jnp.exp(re_dt) * jnp.cos(im_dt), jnp.exp(re_dt) * jnp.sin(im_dt)
    den = a_re * a_re + a_im * a_im
    f_re = ((lb_re - 1.0) * a_re + lb_im * a_im) / den
    f_im = (lb_im * a_re - (lb_re - 1.0) * a_im) / den
    bb_re = f_re[..., None] * b_re - f_im[..., None] * b_im
    bb_im = f_re[..., None] * b_im + f_im[..., None] * b_re
    eye = jnp.eye(SSM_GROUPS, dtype=F32)
    in_w, st_w, n_chunks = SSM_GROUPS * SSM_GROUP, N_STATE, N_STATE // STATE_CHUNK
    b_blk = lambda z: jnp.einsum('dgph,gk->dghkp', z, eye).reshape(2, in_w, n_chunks, STATE_CHUNK)
    bmat = jnp.concatenate([b_blk(bb_re), b_blk(bb_im)], axis=-1).transpose(0, 2, 1, 3).astype(BF16)
    c_blk = lambda z: jnp.einsum('dghp,gk->dgpkh', z, eye).reshape(2, n_chunks, STATE_CHUNK, in_w)
    cmat = jnp.concatenate([c_blk(c_re), -c_blk(c_im)], axis=2).astype(BF16)
    steps = jnp.arange(1, SCAN_ROWS + 1, dtype=F32)[None, :, None, None]
    mag = jnp.exp(re_dt[:, None] * steps)
    ang = im_dt[:, None] * steps
    chunked = lambda z: z.reshape(2, SCAN_ROWS, n_chunks, STATE_CHUNK)
    powers = jnp.concatenate([chunked(mag * jnp.cos(ang)), chunked(mag * jnp.sin(ang))], axis=-1)
    row = jnp.arange(SCAN_ROWS)[:, None, None]
    pc = jnp.stack([powers[0], powers[1][::-1]]).transpose(0, 2, 1, 3)
    a_tabs = []
    for k in (1, 2, 4):
        fwd = jnp.where(row >= k, powers[0, k - 1][None], 0.0)
        bwd = jnp.where(row <= SCAN_ROWS - 1 - k, powers[1, k - 1][None], 0.0)
        a_tabs.append(jnp.stack([fwd, bwd]).transpose(0, 2, 1, 3))
    return bmat, cmat, jnp.stack(a_tabs, axis=1), pc


def _s5_scan(u, tables, n_lat, n_ctx):
    bsz, t_all, width = u.shape
    tm = TOKEN_TILE
    assert n_ctx == tm
    nt = n_lat // tm
    bmat, cmat, a_tab, pc = tables
    n_chunks, lanes = N_STATE // STATE_CHUNK, 2 * STATE_CHUNK
    tile_of = lambda d, s: jnp.where(s == 0, nt, jnp.where(d == 0, s - 1, nt - s))
    return pl.pallas_call(
        _s5_kernel,
        grid=(2, bsz, nt + 1),
        in_specs=[
            pl.BlockSpec((1, tm, width), lambda d, b, s: (b, tile_of(d, s), 0)),
            pl.BlockSpec((1, n_chunks, width, lanes), lambda d, b, s: (d, 0, 0, 0)),
            pl.BlockSpec((1, n_chunks, lanes, width), lambda d, b, s: (d, 0, 0, 0)),
            pl.BlockSpec((1, 3, n_chunks, SCAN_ROWS, lanes), lambda d, b, s: (d, 0, 0, 0, 0)),
            pl.BlockSpec((1, n_chunks, SCAN_ROWS, lanes), lambda d, b, s: (d, 0, 0, 0)),
        ],
        out_specs=pl.BlockSpec((1, 1, tm, width), lambda d, b, s: (d, b, tile_of(d, s), 0)),
        out_shape=jax.ShapeDtypeStruct((2, bsz, t_all, width), F32),
        scratch_shapes=[pltpu.VMEM((n_chunks, SCAN_ROWS, lanes), F32)],
        compiler_params=_cparams(("parallel", "parallel", "arbitrary")),
        name="s5_scan",
    )(u, bmat, cmat, a_tab, pc)


def _gelu_tanh(x):
    c = math.sqrt(2.0 / math.pi)
    return x * (0.5 * (1.0 + jnp.tanh(c * (x + 0.044715 * (x * x * x)))))


def _merge_kernel(x_ref, sh_ref, sc_ref, gt_ref, ng_ref, xa_ref, xap_ref, xan_ref, yb_ref, yf_ref, yr_ref,
                  u_ref, yd_ref, wg_ref, pw_ref, ps_ref, dsk_ref, glu_ref, wb_ref, wo_ref, o_ref, ext_ref,
                  *, n_lat_tiles, n_lat, n_ctx):
    t = pl.program_id(1)
    x = x_ref[0]
    tm = x.shape[0]
    bw = BRANCH_W
    h = (_rms(x, ng_ref[...]) * (1.0 + sc_ref[...]) + sh_ref[...]).astype(BF16)
    gates = [jax.nn.sigmoid(_dot(h, wg_ref[:, D_MODEL * k:D_MODEL * (k + 1)])) for k in range(N_BRANCH)]
    is_ctx = t >= n_lat_tiles
    seg_tile = jnp.where(is_ctx, t - n_lat_tiles, t)
    seg_tiles = jnp.where(is_ctx, n_ctx // tm, n_lat_tiles)
    seg_len = jnp.where(is_ctx, n_ctx, n_lat)
    xa = xa_ref[0]
    ext_ref[0:POOL_HALO, :] = jnp.where(seg_tile > 0, xap_ref[0], 0.0)
    ext_ref[POOL_HALO:POOL_HALO + tm, :] = xa
    ext_ref[POOL_HALO + tm:, :] = jnp.where(seg_tile + 1 < seg_tiles, xan_ref[0], 0.0)
    at = lambda off: ext_ref[POOL_HALO + off:POOL_HALO + off + tm, :]
    sums = []
    acc = at(-1) + xa
    sums.append(acc)
    for half in (2, 4, 8):
        for off in range(half // 2, half):
            acc = acc + at(-off - 1) + at(off)
        sums.append(acc)
    lane = lax.broadcasted_iota(jnp.int32, (tm, bw), 1)
    pos = seg_tile * tm + lax.broadcasted_iota(jnp.int32, (tm, bw), 0)
    group = lane // (bw // len(POOL_WINDOWS))
    half_w = jnp.left_shift(1, group)
    count = jnp.minimum(pos + half_w, seg_len) - jnp.maximum(pos - half_w, 0)
    window_sum = jnp.where(group == 0, sums[0], jnp.where(group == 1, sums[1],
                                                           jnp.where(group == 2, sums[2], sums[3])))
    diff = window_sum / count.astype(F32) - xa
    y_a = _dot(diff.astype(BF16), pw_ref[...]) * ps_ref[...]
    y_s = yf_ref[0, 0] + yr_ref[0, 0] + dsk_ref[...] * u_ref[0]
    z = _dot(_gelu_tanh(y_s).astype(BF16), glu_ref[...])
    y_c = z[:, :bw] * jax.nn.sigmoid(z[:, bw:])
    branches = (y_a.astype(BF16), yb_ref[0], y_c.astype(BF16), yd_ref[0])
    total = None
    for gate, y_k, k in zip(gates, branches, range(N_BRANCH)):
        term = gate * _dot(y_k, wb_ref[k])
        total = term if total is None else total + term
    o_ref[0] = x + gt_ref[...] * _dot(total.astype(BF16), wo_ref[...])


def _merge(x, mod, layer, row_of, norm_g, xa, y_b, y_s5, u, y_d, w_gate, pool_bd, pool_scale, d_skip,
           glu_w, branch_w, out_w, n_rows, n_lat, n_ctx):
    bsz = x.shape[0]
    tm = TOKEN_TILE
    assert n_ctx % tm == 0 or n_rows == n_lat
    halo_blocks = xa.shape[1] // POOL_HALO
    per_tile = tm // POOL_HALO
    tok = lambda width: pl.BlockSpec((1, tm, width), lambda b, t: (b, t, 0))
    single = dict(pipeline_mode=pl.Buffered(1))
    return pl.pallas_call(
        functools.partial(_merge_kernel, n_lat_tiles=n_lat // tm, n_lat=n_lat, n_ctx=n_ctx),
        grid=(bsz, n_rows // tm),
        in_specs=[
            tok(D_MODEL),
            _mod_spec(layer, 3, row_of), _mod_spec(layer, 4, row_of), _mod_spec(layer, 5, row_of),
            _const_spec((1, D_MODEL)),
            tok(256),
            pl.BlockSpec((1, POOL_HALO, 256), lambda b, t: (b, jnp.maximum(t * per_tile - 1, 0), 0)),
            pl.BlockSpec((1, POOL_HALO, 256),
                         lambda b, t: (b, jnp.minimum((t + 1) * per_tile, halo_blocks - 1), 0)),
            tok(256),
            pl.BlockSpec((1, 1, tm, 256), lambda b, t: (0, b, t, 0)),
            pl.BlockSpec((1, 1, tm, 256), lambda b, t: (1, b, t, 0)),
            tok(256), tok(256),
            pl.BlockSpec((D_MODEL, N_BRANCH * D_MODEL), lambda b, t: (0, 0), **single),
            _const_spec((256, 256)), _const_spec((1, 256)), _const_spec((1, 256)),
            _const_spec((256, 512)),
            pl.BlockSpec((N_BRANCH, 256, D_MODEL), lambda b, t: (0, 0, 0), **single),
            pl.BlockSpec((D_MODEL, D_MODEL), lambda b, t: (0, 0), **single),
        ],
        out_specs=tok(D_MODEL),
        out_shape=jax.ShapeDtypeStruct((bsz, n_rows, D_MODEL), F32),
        scratch_shapes=[pltpu.VMEM((tm + 2 * POOL_HALO, 256), F32)],
        compiler_params=_cparams(("parallel", "parallel")),
        name="merge",
    )(x, mod, mod, mod, norm_g.reshape(1, D_MODEL), xa, xa, xa, y_b, y_s5, y_s5, u, y_d,
      w_gate, pool_bd, pool_scale.reshape(1, 256), d_skip.reshape(1, 256), glu_w, branch_w, out_w)


def _rope_tables(n_lat, n_ctx):
    rows = n_lat // GRID_W
    n_freq = HEAD_DIM // 4
    row = jnp.repeat(jnp.arange(rows), GRID_W)
    col = jnp.tile(jnp.arange(GRID_W), rows)
    inv = ROPE_THETA ** (-jnp.arange(n_freq, dtype=F32) / n_freq)
    ang = jnp.stack([row, col], axis=-1).astype(F32)[..., None] * inv
    cos = jnp.broadcast_to(jnp.cos(ang)[:, :, None, :], (n_lat, 2, 2, n_freq)).reshape(n_lat, HEAD_DIM)
    sign = jnp.array([-1.0, 1.0], F32)[None, None, :, None]
    sin = (jnp.sin(ang)[:, :, None, :] * sign).reshape(n_lat, HEAD_DIM)
    cos = jnp.concatenate([cos, jnp.ones((n_ctx, HEAD_DIM), F32)], axis=0)
    sin = jnp.concatenate([sin, jnp.zeros((n_ctx, HEAD_DIM), F32)], axis=0)
    return jnp.tile(cos, (1, 2)), jnp.tile(sin, (1, 2))


def _largest_divisor(n, candidates):
    for c in candidates:
        if n % c == 0:
            return c
    raise ValueError(f"no tile in {candidates} divides {n}")


def _pool_block_diag(pool_w):
    groups, width, _ = pool_w.shape
    eye = jnp.eye(groups, dtype=pool_w.dtype)
    return jnp.einsum('gcd,gk->gckd', pool_w, eye).reshape(groups * width, groups * width)


def kernel(x, c, ctx, c_ctx, w_mod, b_mod, norm_g, ffn_in, ffn_out, w_in, win_sink, qk_norm, pool_w, pool_scale, ssm_a_re, ssm_a_im, ssm_log_dt, ssm_b_re, ssm_b_im, ssm_c_re, ssm_c_im, ssm_d, glu_w, branch_w, out_w, final_g):
    bsz, n_lat, d = x.shape
    n_ctx = ctx.shape[1]
    depth = w_mod.shape[0]
    t_all = n_lat + n_ctx
    assert d == D_MODEL and bsz < MOD_ROWS and n_lat % TOKEN_TILE == 0 and n_ctx == TOKEN_TILE
    assert n_lat % GRID_W == 0

    cond_rows = jnp.zeros((MOD_ROWS, d), F32).at[:bsz].set(c).at[bsz].set(c_ctx)
    mod = _modulation(cond_rows, w_mod, b_mod)
    n_lat_tiles = n_lat // TOKEN_TILE
    row_of = lambda b, t: jnp.where(t >= n_lat_tiles, bsz, b)
    lat_row = lambda b, t: b

    cos, sin = _rope_tables(n_lat, n_ctx)
    tq = _largest_divisor(n_lat, (2048, 1024, 512, 256))
    tk = _largest_divisor(t_all, (1280, 1024, 768, 512, 256))

    xs = jnp.concatenate([x, ctx], axis=1)
    for l in range(depth):
        last = l == depth - 1
        bf = lambda w: w.astype(BF16)
        xs = _ffn(xs, mod, l, 0, row_of, norm_g[l, 0], bf(ffn_in[l, 0]), bf(ffn_out[l, 0]), t_all)

        score_bound = 1.02 * HEAD_DIM ** 0.5 * jnp.max(jnp.abs(qk_norm[l, 0])) * jnp.max(jnp.abs(qk_norm[l, 1]))
        qw, qt, kw, vw, ka, vt, u, xa = _proj(xs, mod, l, row_of, norm_g[l, 1], bf(w_in[l, :, :PROJ_W]),
                                              cos, sin, qk_norm[l, 0], qk_norm[l, 1], score_bound)
        y_b = _window_attention(qw, kw, vw, win_sink[l], n_lat, n_ctx, not last)

        def global_attention(q_row0, n_q, k_row0, n_k, tq_, tk_):
            run = lambda bounded: lambda: _global_attention(qt, ka, vt, q_row0, n_q, k_row0, n_k, tq_, tk_,
                                                            bounded)
            return lax.cond(score_bound <= MAX_SCORE_BOUND, run(True), run(False))

        y_d = global_attention(0, n_lat, 0, t_all, tq, tk)
        tables = _s5_tables(ssm_a_re[l], ssm_a_im[l], ssm_log_dt[l], ssm_b_re[l], ssm_b_im[l],
                            ssm_c_re[l], ssm_c_im[l])
        y_s5 = _s5_scan(u, tables, n_lat, n_ctx)
        n_rows = n_lat
        if not last:
            y_d_ctx = global_attention(n_lat, n_ctx, n_lat, n_ctx, n_ctx, n_ctx)
            y_d = jnp.concatenate([y_d, y_d_ctx], axis=1)
            n_rows = t_all
        xs = _merge(xs, mod, l, row_of, norm_g[l, 1], xa, y_b, y_s5, u, y_d, bf(w_in[l, :, PROJ_W:]),
                    bf(_pool_block_diag(pool_w[l])), pool_scale[l], ssm_d[l], bf(glu_w[l]), bf(branch_w[l]),
                    bf(out_w[l]), n_rows, n_lat, n_ctx)
        xs = _ffn(xs, mod, l, 2, row_of if not last else lat_row, norm_g[l, 2], bf(ffn_in[l, 1]),
                  bf(ffn_out[l, 1]), n_rows, final_g=final_g if last else None)
    return xs
```

```python
import functools
import math

import jax
import jax.numpy as jnp
import numpy as np
from jax import lax
from jax.experimental import pallas as pl
from jax.experimental.pallas import tpu as pltpu

F32 = jnp.float32
BF16 = jnp.bfloat16

D_MODEL = 1024
D_FF = 2816
N_SUB = 3
BRANCH_W = 256
HEAD_DIM = 64
GRID_W = 64
BLOCK = 128
EPS = 1e-6
ROPE_THETA = 10000.0
N_BRANCH = 4
POOL_WINDOWS = (2, 4, 8, 16)
SSM_GROUPS = 16
SSM_GROUP = 16
SSM_STATE = 64
N_STATE = SSM_GROUPS * SSM_STATE
PROJ_W = 6 * BRANCH_W
MOD_ROWS = 8
POOL_HALO = 8
SCAN_ROWS = 8
STATE_CHUNK = 128
WIN_BLOCKS = 2
NEG = -1e30
QK_SCALE = HEAD_DIM ** -0.5
LOG2E = math.log2(math.e)
V_ROWS = HEAD_DIM + 16
MAX_SCORE_BOUND = 40.0
TOKEN_TILE = 256
VMEM_LIMIT = 56 * 1024 * 1024


def _cparams(sem):
    return pltpu.CompilerParams(dimension_semantics=sem, vmem_limit_bytes=VMEM_LIMIT)


def _dot(a, b):
    return jnp.dot(a, b, preferred_element_type=F32)


def _dot_nt(a, b):
    return lax.dot_general(a, b, (((1,), (1,)), ((), ())), preferred_element_type=F32)


def _rms(x, g):
    ms = jnp.mean(x * x, axis=-1, keepdims=True)
    return x * lax.rsqrt(ms + EPS) * g


def _silu(x):
    return x * jax.nn.sigmoid(x)


def _const_spec(shape):
    nd = len(shape)
    return pl.BlockSpec(shape, lambda *_: (0,) * nd)


def _mod_kernel(s_ref, w_ref, b_ref, o_ref):
    s = _silu(s_ref[...])
    w = w_ref[0]
    s_hi = s.astype(BF16)
    s_lo = (s - s_hi.astype(F32)).astype(BF16)
    w_hi = w.astype(BF16)
    w_lo = (w - w_hi.astype(F32)).astype(BF16)
    o_ref[0] = _dot(s_hi, w_hi) + _dot(s_hi, w_lo) + _dot(s_lo, w_hi) + b_ref[0]


def _modulation(rows, w_mod, b_mod):
    depth, d, width = w_mod.shape
    tn = 1024
    out = pl.pallas_call(
        _mod_kernel,
        grid=(depth, width // tn),
        in_specs=[
            pl.BlockSpec((MOD_ROWS, d), lambda l, j: (0, 0)),
            pl.BlockSpec((1, d, tn), lambda l, j: (l, 0, j)),
            pl.BlockSpec((1, 1, tn), lambda l, j: (l, 0, j)),
        ],
        out_specs=pl.BlockSpec((1, MOD_ROWS, tn), lambda l, j: (l, 0, j)),
        out_shape=jax.ShapeDtypeStruct((depth, MOD_ROWS, width), F32),
        compiler_params=_cparams(("parallel", "parallel")),
        name="modulation",
    )(rows, w_mod, b_mod.reshape(depth, 1, width))
    return out.reshape(depth, MOD_ROWS, N_SUB * 3, 1, d).transpose(0, 2, 1, 3, 4)


def _mod_spec(layer, vec, row_of):
    return pl.BlockSpec((None, None, None, 1, D_MODEL), lambda b, t: (layer, vec, row_of(b, t), 0, 0))


def _tail_rows(main_ref, tail_ref, n_main_tiles):
    return jnp.where(pl.program_id(1) >= n_main_tiles, tail_ref[0], main_ref[0])


def _ffn_kernel(x_ref, sh_ref, sc_ref, gt_ref, ng_ref, win_ref, wout_ref, *rest, final, n_main_tiles):
    o_ref = rest[-1]
    x = x_ref[0] if n_main_tiles is None else _tail_rows(x_ref, rest[0], n_main_tiles)
    h = (_rms(x, ng_ref[...]) * (1.0 + sc_ref[...]) + sh_ref[...]).astype(BF16)
    gu = _dot(h, win_ref[...])
    a = (_silu(gu[:, :D_FF]) * gu[:, D_FF:]).astype(BF16)
    xn = x + (0.5 * gt_ref[...]) * _dot(a, wout_ref[...])
    if final:
        xn = _rms(xn, rest[0][...])
    o_ref[0] = xn


def _tail_specs(tm, width, n_main_tiles):
    return (pl.BlockSpec((1, tm, width), lambda b, t: (b, jnp.minimum(t, n_main_tiles - 1), 0)),
            pl.BlockSpec((1, tm, width), lambda b, t: (b, jnp.maximum(t - n_main_tiles, 0), 0)))


def _ffn(x, mod, layer, sub, row_of, norm_g, w_in, w_out, n_rows, final_g=None, tail=None):
    bsz = x.shape[0]
    tm = TOKEN_TILE
    final = final_g is not None
    assert not (final and tail is not None)
    n_main_tiles = None if tail is None else x.shape[1] // tm
    x_spec = pl.BlockSpec((1, tm, D_MODEL), lambda b, t: (b, t, 0))
    if tail is not None:
        x_spec, tail_spec = _tail_specs(tm, D_MODEL, n_main_tiles)
    in_specs = [
        x_spec,
        _mod_spec(layer, sub * 3 + 0, row_of),
        _mod_spec(layer, sub * 3 + 1, row_of),
        _mod_spec(layer, sub * 3 + 2, row_of),
        _const_spec((1, D_MODEL)),
        pl.BlockSpec((D_MODEL, 2 * D_FF), lambda b, t: (0, 0), pipeline_mode=pl.Buffered(1)),
        pl.BlockSpec((D_FF, D_MODEL), lambda b, t: (0, 0), pipeline_mode=pl.Buffered(1)),
    ]
    args = [x, mod, mod, mod, norm_g.reshape(1, D_MODEL), w_in, w_out]
    if final:
        in_specs.append(_const_spec((1, D_MODEL)))
        args.append(final_g.reshape(1, D_MODEL))
    if tail is not None:
        in_specs.append(tail_spec)
        args.append(tail)
    return pl.pallas_call(
        functools.partial(_ffn_kernel, final=final, n_main_tiles=n_main_tiles),
        grid=(bsz, n_rows // tm),
        in_specs=in_specs,
        out_specs=pl.BlockSpec((1, tm, D_MODEL), lambda b, t: (b, t, 0)),
        out_shape=jax.ShapeDtypeStruct((bsz, n_rows, D_MODEL), F32),
        compiler_params=_cparams(("parallel", "parallel")),
        name="ffn_final" if final else "ffn",
    )(*args)


def _rope(x, cos, sin, first_half):
    w = x.shape[-1]
    swapped = jnp.where(first_half, pltpu.roll(x, w - 16, 1), pltpu.roll(x, 16, 1))
    return x * cos + swapped * sin


def _head_rms(x, g, low_head):
    x2 = x * x
    lo = jnp.sum(jnp.where(low_head, x2, 0.0), axis=-1, keepdims=True)
    hi = jnp.sum(jnp.where(low_head, 0.0, x2), axis=-1, keepdims=True)
    ms = jnp.where(low_head, lo, hi) * (1.0 / HEAD_DIM)
    return x * lax.rsqrt(ms + EPS) * g


def _store_padded(ref, x, low_head):
    sw = pltpu.roll(x, HEAD_DIM, 1)
    zero = jnp.zeros_like(x)
    ref[0, 0, 0] = jnp.where(low_head, x, zero).astype(ref.dtype)
    ref[0, 0, 1] = jnp.where(low_head, zero, sw).astype(ref.dtype)
    ref[0, 1, 0] = jnp.where(low_head, sw, zero).astype(ref.dtype)
    ref[0, 1, 1] = jnp.where(low_head, zero, x).astype(ref.dtype)


def _proj_kernel(x_ref, sh_ref, sc_ref, ng_ref, w_ref, cos_ref, sin_ref, qg_ref, kg_ref, qb_ref, kvb_ref,
                 qw_ref, qt_ref, kw_ref, vw_ref, ka_ref, vt_ref, u_ref, xa_ref):
    x = x_ref[0]
    tm = x.shape[0]
    h = (_rms(x, ng_ref[...]) * (1.0 + sc_ref[...]) + sh_ref[...]).astype(BF16)
    p = _dot(h, w_ref[...])
    lane = lax.broadcasted_iota(jnp.int32, (tm, 2 * HEAD_DIM), 1)
    low_head = lane < HEAD_DIM
    first_half = jnp.bitwise_and(lane, 31) < 16
    cos = cos_ref[...]
    sin = sin_ref[...]
    rope = lambda z: _rope(z, cos, sin, first_half)
    k_win = rope(p[:, 0:128])
    v_win = p[:, 128:256]
    u_ref[0] = p[:, 256:512]
    k_glb = rope(_head_rms(p[:, 512:640], kg_ref[...], low_head))
    v_glb = p[:, 640:768]
    heads = lambda z: (z, pltpu.roll(z, HEAD_DIM, 1))
    for c in range(2):
        qw = rope(p[:, 768 + 128 * c:896 + 128 * c]) * QK_SCALE
        qw_ref[0, :, 128 * c:128 * (c + 1)] = qw.astype(BF16)
        qd = rope(_head_rms(p[:, 1024 + 128 * c:1152 + 128 * c], qg_ref[...], low_head)) * (QK_SCALE * LOG2E)
        for g, z in enumerate(heads(qd)):
            qt_ref[0, 2 * c + g] = jnp.where(low_head, z, qb_ref[...]).T.astype(BF16)
    for j, (kz, vz) in enumerate(zip(heads(k_glb), heads(v_glb))):
        ka_ref[0, j] = jnp.where(low_head, kz, kvb_ref[...]).astype(BF16)
        vt_ref[0, j] = jnp.where(low_head, vz, kvb_ref[...]).T[0:V_ROWS].astype(BF16)
    xa_ref[0] = p[:, 1280:1536]
    _store_padded(kw_ref, k_win, low_head)
    _store_padded(vw_ref, v_win, low_head)


def _proj(x, mod, layer, row_of, norm_g, w, cos, sin, q_g, k_g, score_bound):
    bsz, t_all, _ = x.shape
    tm = TOKEN_TILE
    tok = lambda width: pl.BlockSpec((1, tm, width), lambda b, t: (b, t, 0))
    pad = pl.BlockSpec((1, 2, 2, tm, 128), lambda b, t: (b, 0, 0, t, 0))
    tab = pl.BlockSpec((tm, 128), lambda b, t: (t, 0))
    pad_shape = jax.ShapeDtypeStruct((bsz, 2, 2, t_all, 128), BF16)
    marker = jnp.zeros((1, 128), F32).at[0, HEAD_DIM].set(1.0)
    return pl.pallas_call(
        _proj_kernel,
        grid=(bsz, t_all // tm),
        in_specs=[
            tok(D_MODEL),
            _mod_spec(layer, 3, row_of),
            _mod_spec(layer, 4, row_of),
            _const_spec((1, D_MODEL)),
            pl.BlockSpec((D_MODEL, PROJ_W), lambda b, t: (0, 0), pipeline_mode=pl.Buffered(1)),
            tab, tab,
            _const_spec((1, 128)), _const_spec((1, 128)), _const_spec((1, 128)), _const_spec((1, 128)),
        ],
        out_specs=[
            tok(256),
            pl.BlockSpec((1, 4, 128, tm), lambda b, t: (b, 0, 0, t)),
            pad, pad,
            pl.BlockSpec((1, 2, tm, 128), lambda b, t: (b, 0, t, 0)),
            pl.BlockSpec((1, 2, V_ROWS, tm), lambda b, t: (b, 0, 0, t)),
            tok(256), tok(256),
        ],
        out_shape=[
            jax.ShapeDtypeStruct((bsz, t_all, 256), BF16),
            jax.ShapeDtypeStruct((bsz, 4, 128, t_all), BF16),
            pad_shape, pad_shape,
            jax.ShapeDtypeStruct((bsz, 2, t_all, 128), BF16),
            jax.ShapeDtypeStruct((bsz, 2, V_ROWS, t_all), BF16),
            jax.ShapeDtypeStruct((bsz, t_all, 256), F32),
            jax.ShapeDtypeStruct((bsz, t_all, 256), F32),
        ],
        compiler_params=_cparams(("parallel", "parallel")),
        name="proj",
    )(x, mod, mod, norm_g.reshape(1, D_MODEL), w, cos, sin,
      jnp.tile(q_g, 2).reshape(1, 128), jnp.tile(k_g, 2).reshape(1, 128),
      marker * (-score_bound * LOG2E), marker)


def _win_kernel(sink_ref, q_ref, kp_ref, kc_ref, kn_ref, kx_ref, vp_ref, vc_ref, vn_ref, vx_ref,
                o_ref, *, n_lat_blocks):
    i = pl.program_id(1)
    tile = WIN_BLOCKS * BLOCK
    band = tile + 2 * BLOCK
    r = lax.broadcasted_iota(jnp.int32, (tile, band), 0)
    c = lax.broadcasted_iota(jnp.int32, (tile, band), 1)
    blk0 = i * WIN_BLOCKS
    lo = jnp.where(blk0 >= 1, 0, BLOCK)
    hi = jnp.where(blk0 < n_lat_blocks, jnp.minimum((n_lat_blocks - blk0 + 1) * BLOCK, band), 0)
    in_window = (c - r).astype(jnp.uint32) <= 2 * BLOCK
    in_range = (c - lo).astype(jnp.uint32) < (hi - lo).astype(jnp.uint32)
    bias = jnp.where(in_window, jnp.where(in_range, 0.0, NEG), NEG)
    for j in range(2):
        q = q_ref[0, :, 128 * j:128 * (j + 1)]
        acc = jnp.zeros((tile, 128), F32)
        for g in range(2):
            k_band = jnp.concatenate([kp_ref[0, j, g], kc_ref[0, j, g], kn_ref[0, j, g]], axis=0)
            s = jnp.concatenate([_dot_nt(q, k_band) + bias, _dot_nt(q, kx_ref[0, j, g])], axis=1)
            v = jnp.concatenate([vp_ref[0, j, g], vc_ref[0, j, g], vn_ref[0, j, g], vx_ref[0, j, g]], axis=0)
            sink = sink_ref[2 * j + g]
            m = jnp.maximum(s.max(-1, keepdims=True), sink)
            p = jnp.exp(s - m)
            den = p.sum(-1, keepdims=True) + jnp.exp(sink - m)
            acc = acc + _dot(p.astype(BF16), v) / den
        o_ref[0, :, 128 * j:128 * (j + 1)] = acc.astype(o_ref.dtype)


def _window_attention(q, k_pad, v_pad, sink, n_lat, n_ctx, with_ctx):
    bsz, t_all, _ = q.shape
    n_lat_blocks = n_lat // BLOCK
    n_blocks = t_all // BLOCK
    n_q_blocks = n_blocks if with_ctx else n_lat_blocks
    assert n_q_blocks % WIN_BLOCKS == 0 and n_lat_blocks % WIN_BLOCKS == 0
    tile = WIN_BLOCKS * BLOCK
    edge = lambda off: pl.BlockSpec(
        (1, 2, 2, BLOCK, 128), lambda b, i: (b, 0, 0, jnp.clip(i * WIN_BLOCKS + off, 0, n_blocks - 1), 0))
    cur = pl.BlockSpec((1, 2, 2, tile, 128), lambda b, i: (b, 0, 0, i, 0))
    ctx = pl.BlockSpec((1, 2, 2, n_ctx, 128), lambda b, i: (b, 0, 0, n_lat // n_ctx, 0))
    return pl.pallas_call(
        functools.partial(_win_kernel, n_lat_blocks=n_lat_blocks),
        grid=(bsz, n_q_blocks // WIN_BLOCKS),
        in_specs=[
            pl.BlockSpec(memory_space=pltpu.SMEM),
            pl.BlockSpec((1, tile, 256), lambda b, i: (b, i, 0)),
            edge(-1), cur, edge(WIN_BLOCKS), ctx,
            edge(-1), cur, edge(WIN_BLOCKS), ctx,
        ],
        out_specs=pl.BlockSpec((1, tile, 256), lambda b, i: (b, i, 0)),
        out_shape=jax.ShapeDtypeStruct((bsz, n_q_blocks * BLOCK, 256), BF16),
        compiler_params=_cparams(("parallel", "parallel")),
        name="window_attention",
    )(sink, q, k_pad, k_pad, k_pad, k_pad, v_pad, v_pad, v_pad, v_pad)


def _glb_finish(acc_ref, o_ref):
    o_t = jnp.concatenate([acc_ref[g, 0:HEAD_DIM] / acc_ref[g, HEAD_DIM:HEAD_DIM + 1] for g in range(2)],
                          axis=0)
    o_ref[0] = o_t.T.astype(o_ref.dtype)


def _glb_bounded_kernel(qt_ref, k_ref, vt_ref, o_ref, acc_ref):
    ki = pl.program_id(3)

    @pl.when(ki == 0)
    def _():
        acc_ref[...] = jnp.zeros_like(acc_ref)

    for g in range(2):
        p = jnp.exp2(_dot(k_ref[0, 0], qt_ref[0, g])).astype(BF16)
        acc_ref[g] += _dot(vt_ref[0, 0], p)

    @pl.when(ki == pl.num_programs(3) - 1)
    def _():
        _glb_finish(acc_ref, o_ref)


def _glb_online_kernel(qt_ref, k_ref, vt_ref, o_ref, acc_ref, m_ref):
    ki = pl.program_id(3)

    @pl.when(ki == 0)
    def _():
        acc_ref[...] = jnp.zeros_like(acc_ref)
        m_ref[...] = jnp.full_like(m_ref, NEG)

    for g in range(2):
        s = _dot(k_ref[0, 0], qt_ref[0, g])
        m_prev = m_ref[g, 0:1]
        m_new = jnp.maximum(m_prev, s.max(axis=0, keepdims=True))
        p = jnp.exp2(s - m_new).astype(BF16)
        acc_ref[g] = jnp.exp2(m_prev - m_new) * acc_ref[g] + _dot(vt_ref[0, 0], p)
        m_ref[g] = jnp.broadcast_to(m_new, m_ref.shape[1:])

    @pl.when(ki == pl.num_programs(3) - 1)
    def _():
        _glb_finish(acc_ref, o_ref)


def _global_attention(qt, k_aug, vt_aug, q_row0, n_q, k_row0, n_k, tq, tk, bounded):
    bsz = qt.shape[0]
    q0, k0 = q_row0 // tq, k_row0 // tk
    scratch = [pltpu.VMEM((2, V_ROWS, tq), F32)]
    if not bounded:
        scratch.append(pltpu.VMEM((2, 8, tq), F32))
    return pl.pallas_call(
        _glb_bounded_kernel if bounded else _glb_online_kernel,
        grid=(bsz, 2, n_q // tq, n_k // tk),
        in_specs=[
            pl.BlockSpec((1, 2, 128, tq), lambda b, j, qi, ki: (b, j, 0, q0 + qi)),
            pl.BlockSpec((1, 1, tk, 128), lambda b, j, qi, ki: (b, j, k0 + ki, 0)),
            pl.BlockSpec((1, 1, V_ROWS, tk), lambda b, j, qi, ki: (b, j, 0, k0 + ki)),
        ],
        out_specs=pl.BlockSpec((1, tq, 128), lambda b, j, qi, ki: (b, qi, j)),
        out_shape=jax.ShapeDtypeStruct((bsz, n_q, 256), BF16),
        scratch_shapes=scratch,
        compiler_params=_cparams(("parallel", "parallel", "parallel", "arbitrary")),
        name="global_attention" if bounded else "global_attention_online",
    )(qt, k_aug, vt_aug)


def _s5_tile(u, bmat_ref, cmat_ref, a_ref, pc_ref, carry_ref, reverse):
    rows = u.shape[0]
    n_groups = rows // SCAN_ROWS
    w = STATE_CHUNK
    ub = u.astype(BF16)
    order = range(n_groups - 1, -1, -1) if reverse else range(n_groups)
    last = 0 if reverse else SCAN_ROWS - 1
    y = None
    for c in range(N_STATE // w):
        bu = _dot(ub, bmat_ref[0, c])
        pr, pi = pc_ref[0, c, :, :w], pc_ref[0, c, :, w:]
        cr, ci = carry_ref[c, :, :w], carry_ref[c, :, w:]
        states = [None] * n_groups
        for g in order:
            vr = bu[SCAN_ROWS * g:SCAN_ROWS * (g + 1), :w]
            vi = bu[SCAN_ROWS * g:SCAN_ROWS * (g + 1), w:]
            for s, k in enumerate((1, 2, 4)):
                ar, ai = a_ref[0, s, c, :, :w], a_ref[0, s, c, :, w:]
                shift = SCAN_ROWS - k if reverse else k
                sr = pltpu.roll(vr, shift, 0)
                si = pltpu.roll(vi, shift, 0)
                vr, vi = vr + ar * sr - ai * si, vi + ar * si + ai * sr
            vr, vi = vr + pr * cr - pi * ci, vi + pr * ci + pi * cr
            cr = jnp.broadcast_to(vr[last:last + 1], (SCAN_ROWS, w))
            ci = jnp.broadcast_to(vi[last:last + 1], (SCAN_ROWS, w))
            states[g] = jnp.concatenate([vr, vi], axis=1)
        carry_ref[c, :, :w] = cr
        carry_ref[c, :, w:] = ci
        part = _dot(jnp.concatenate(states, axis=0).astype(BF16), cmat_ref[0, c])
        y = part if y is None else y + part
    return y


def _s5_kernel(u_ref, bmat_ref, cmat_ref, a_ref, pc_ref, y_ref, carry_ref):
    direction = pl.program_id(0)
    step = pl.program_id(2)

    @pl.when(step == 0)
    def _():
        carry_ref[...] = jnp.zeros_like(carry_ref)

    for reverse in (False, True):
        @pl.when(direction == int(reverse))
        def _(reverse=reverse):
            y_ref[0, 0] = _s5_tile(u_ref[0], bmat_ref, cmat_ref, a_ref, pc_ref, carry_ref, reverse)


def _s5_tables(a_re, a_im, log_dt, b_re, b_im, c_re, c_im):
    dt = jnp.exp(log_dt)[..., None]
    re_dt, im_dt = a_re * dt, a_im * dt
    lb_re, lb_im = jnp.exp(re_dt) * jnp.cos(im_dt), jnp.exp(re_dt) * jnp.sin(im_dt)
    den = a_re * a_re + a_im * a_im
    f_re = ((lb_re - 1.0) * a_re + lb_im * a_im) / den
    f_im = (lb_im * a_re - (lb_re - 1.0) * a_im) / den
    bb_re = f_re[..., None] * b_re - f_im[..., None] * b_im
    bb_im = f_re[..., None] * b_im + f_im[..., None] * b_re
    eye = jnp.eye(SSM_GROUPS, dtype=F32)
    in_w, st_w, n_chunks = SSM_GROUPS * SSM_GROUP, N_STATE, N_STATE // STATE_CHUNK
    b_blk = lambda z: jnp.einsum('dgph,gk->dghkp', z, eye).reshape(2, in_w, n_chunks, STATE_CHUNK)
    bmat = jnp.concatenate([b_blk(bb_re), b_blk(bb_im)], axis=-1).transpose(0, 2, 1, 3).astype(BF16)
    c_blk = lambda z: jnp.einsum('dghp,gk->dgpkh', z, eye).reshape(2, n_chunks, STATE_CHUNK, in_w)
    cmat = jnp.concatenate([c_blk(c_re), -c_blk(c_im)], axis=2).astype(BF16)
    steps = jnp.arange(1, SCAN_ROWS + 1, dtype=F32)[None, :, None, None]
    mag = jnp.exp(re_dt[:, None] * steps)
    ang = im_dt[:, None] * steps
    chunked = lambda z: z.reshape(2, SCAN_ROWS, n_chunks, STATE_CHUNK)
    powers = jnp.concatenate([chunked(mag * jnp.cos(ang)), chunked(mag * jnp.sin(ang))], axis=-1)
    row = jnp.arange(SCAN_ROWS)[:, None, None]
    pc = jnp.stack([powers[0], powers[1][::-1]]).transpose(0, 2, 1, 3)
    a_tabs = []
    for k in (1, 2, 4):
        fwd = jnp.where(row >= k, powers[0, k - 1][None], 0.0)
        bwd = jnp.where(row <= SCAN_ROWS - 1 - k, powers[1, k - 1][None], 0.0)
        a_tabs.append(jnp.stack([fwd, bwd]).transpose(0, 2, 1, 3))
    return bmat, cmat, jnp.stack(a_tabs, axis=1), pc


def _s5_scan(u, tables, n_lat, n_ctx):
    bsz, t_all, width = u.shape
    tm = TOKEN_TILE
    assert n_ctx == tm
    nt = n_lat // tm
    bmat, cmat, a_tab, pc = tables
    n_chunks, lanes = N_STATE // STATE_CHUNK, 2 * STATE_CHUNK
    tile_of = lambda d, s: jnp.where(s == 0, nt, jnp.where(d == 0, s - 1, nt - s))
    return pl.pallas_call(
        _s5_kernel,
        grid=(2, bsz, nt + 1),
        in_specs=[
            pl.BlockSpec((1, tm, width), lambda d, b, s: (b, tile_of(d, s), 0)),
            pl.BlockSpec((1, n_chunks, width, lanes), lambda d, b, s: (d, 0, 0, 0)),
            pl.BlockSpec((1, n_chunks, lanes, width), lambda d, b, s: (d, 0, 0, 0)),
            pl.BlockSpec((1, 3, n_chunks, SCAN_ROWS, lanes), lambda d, b, s: (d, 0, 0, 0, 0)),
            pl.BlockSpec((1, n_chunks, SCAN_ROWS, lanes), lambda d, b, s: (d, 0, 0, 0)),
        ],
        out_specs=pl.BlockSpec((1, 1, tm, width), lambda d, b, s: (d, b, tile_of(d, s), 0)),
        out_shape=jax.ShapeDtypeStruct((2, bsz, t_all, width), F32),
        scratch_shapes=[pltpu.VMEM((n_chunks, SCAN_ROWS, lanes), F32)],
        compiler_params=_cparams(("parallel", "parallel", "arbitrary")),
        name="s5_scan",
    )(u, bmat, cmat, a_tab, pc)


def _gelu_tanh(x):
    c = math.sqrt(2.0 / math.pi)
    return x * (0.5 * (1.0 + jnp.tanh(c * (x + 0.044715 * (x * x * x)))))


def _merge_kernel(x_ref, sh_ref, sc_ref, gt_ref, ng_ref, xa_ref, xap_ref, xan_ref, yb_ref, yf_ref, yr_ref,
                  u_ref, yd_ref, ydt_ref, wg_ref, pw_ref, ps_ref, dsk_ref, glu_ref, wb_ref, wo_ref, o_ref, ext_ref,
                  *, n_lat_tiles, n_lat, n_ctx):
    t = pl.program_id(1)
    x = x_ref[0]
    tm = x.shape[0]
    bw = BRANCH_W
    h = (_rms(x, ng_ref[...]) * (1.0 + sc_ref[...]) + sh_ref[...]).astype(BF16)
    gates = [jax.nn.sigmoid(_dot(h, wg_ref[:, D_MODEL * k:D_MODEL * (k + 1)])) for k in range(N_BRANCH)]
    is_ctx = t >= n_lat_tiles
    seg_tile = jnp.where(is_ctx, t - n_lat_tiles, t)
    seg_tiles = jnp.where(is_ctx, n_ctx // tm, n_lat_tiles)
    seg_len = jnp.where(is_ctx, n_ctx, n_lat)
    xa = xa_ref[0]
    ext_ref[0:POOL_HALO, :] = jnp.where(seg_tile > 0, xap_ref[0], 0.0)
    ext_ref[POOL_HALO:POOL_HALO + tm, :] = xa
    ext_ref[POOL_HALO + tm:, :] = jnp.where(seg_tile + 1 < seg_tiles, xan_ref[0], 0.0)
    at = lambda off: ext_ref[POOL_HALO + off:POOL_HALO + off + tm, :]
    sums = []
    acc = at(-1) + xa
    sums.append(acc)
    for half in (2, 4, 8):
        for off in range(half // 2, half):
            acc = acc + at(-off - 1) + at(off)
        sums.append(acc)
    lane = lax.broadcasted_iota(jnp.int32, (tm, bw), 1)
    pos = seg_tile * tm + lax.broadcasted_iota(jnp.int32, (tm, bw), 0)
    group = lane // (bw // len(POOL_WINDOWS))
    half_w = jnp.left_shift(1, group)
    count = jnp.minimum(pos + half_w, seg_len) - jnp.maximum(pos - half_w, 0)
    window_sum = jnp.where(group == 0, sums[0], jnp.where(group == 1, sums[1],
                                                           jnp.where(group == 2, sums[2], sums[3])))
    diff = window_sum / count.astype(F32) - xa
    y_a = _dot(diff.astype(BF16), pw_ref[...]) * ps_ref[...]
    y_s = yf_ref[0, 0] + yr_ref[0, 0] + dsk_ref[...] * u_ref[0]
    z = _dot(_gelu_tanh(y_s).astype(BF16), glu_ref[...])
    y_c = z[:, :bw] * jax.nn.sigmoid(z[:, bw:])
    branches = (y_a.astype(BF16), yb_ref[0], y_c.astype(BF16), _tail_rows(yd_ref, ydt_ref, n_lat_tiles))
    total = None
    for gate, y_k, k in zip(gates, branches, range(N_BRANCH)):
        term = gate * _dot(y_k, wb_ref[k])
        total = term if total is None else total + term
    o_ref[0] = x + gt_ref[...] * _dot(total.astype(BF16), wo_ref[...])


def _merge(x, mod, layer, row_of, norm_g, xa, y_b, y_s5, u, y_d, y_d_tail, w_gate, pool_bd, pool_scale, d_skip,
           glu_w, branch_w, out_w, n_rows, n_lat, n_ctx):
    bsz = x.shape[0]
    tm = TOKEN_TILE
    assert n_ctx % tm == 0 or n_rows == n_lat
    halo_blocks = xa.shape[1] // POOL_HALO
    per_tile = tm // POOL_HALO
    tok = lambda width: pl.BlockSpec((1, tm, width), lambda b, t: (b, t, 0))
    single = dict(pipeline_mode=pl.Buffered(1))
    return pl.pallas_call(
        functools.partial(_merge_kernel, n_lat_tiles=n_lat // tm, n_lat=n_lat, n_ctx=n_ctx),
        grid=(bsz, n_rows // tm),
        in_specs=[
            tok(D_MODEL),
            _mod_spec(layer, 3, row_of), _mod_spec(layer, 4, row_of), _mod_spec(layer, 5, row_of),
            _const_spec((1, D_MODEL)),
            tok(256),
            pl.BlockSpec((1, POOL_HALO, 256), lambda b, t: (b, jnp.maximum(t * per_tile - 1, 0), 0)),
            pl.BlockSpec((1, POOL_HALO, 256),
                         lambda b, t: (b, jnp.minimum((t + 1) * per_tile, halo_blocks - 1), 0)),
            tok(256),
            pl.BlockSpec((1, 1, tm, 256), lambda b, t: (0, b, t, 0)),
            pl.BlockSpec((1, 1, tm, 256), lambda b, t: (1, b, t, 0)),
            tok(256), *_tail_specs(tm, 256, n_lat // tm),
            pl.BlockSpec((D_MODEL, N_BRANCH * D_MODEL), lambda b, t: (0, 0), **single),
            _const_spec((256, 256)), _const_spec((1, 256)), _const_spec((1, 256)),
            _const_spec((256, 512)),
            pl.BlockSpec((N_BRANCH, 256, D_MODEL), lambda b, t: (0, 0, 0), **single),
            pl.BlockSpec((D_MODEL, D_MODEL), lambda b, t: (0, 0), **single),
        ],
        out_specs=tok(D_MODEL),
        out_shape=jax.ShapeDtypeStruct((bsz, n_rows, D_MODEL), F32),
        scratch_shapes=[pltpu.VMEM((tm + 2 * POOL_HALO, 256), F32)],
        compiler_params=_cparams(("parallel", "parallel")),
        name="merge",
    )(x, mod, mod, mod, norm_g.reshape(1, D_MODEL), xa, xa, xa, y_b, y_s5, y_s5, u, y_d, y_d_tail,
      w_gate, pool_bd, pool_scale.reshape(1, 256), d_skip.reshape(1, 256), glu_w, branch_w, out_w)


def _rope_tables(n_lat, n_ctx):
    rows = n_lat // GRID_W
    n_freq = HEAD_DIM // 4
    row = jnp.repeat(jnp.arange(rows), GRID_W)
    col = jnp.tile(jnp.arange(GRID_W), rows)
    inv = ROPE_THETA ** (-jnp.arange(n_freq, dtype=F32) / n_freq)
    ang = jnp.stack([row, col], axis=-1).astype(F32)[..., None] * inv
    cos = jnp.broadcast_to(jnp.cos(ang)[:, :, None, :], (n_lat, 2, 2, n_freq)).reshape(n_lat, HEAD_DIM)
    sign = jnp.array([-1.0, 1.0], F32)[None, None, :, None]
    sin = (jnp.sin(ang)[:, :, None, :] * sign).reshape(n_lat, HEAD_DIM)
    cos = jnp.concatenate([cos, jnp.ones((n_ctx, HEAD_DIM), F32)], axis=0)
    sin = jnp.concatenate([sin, jnp.zeros((n_ctx, HEAD_DIM), F32)], axis=0)
    return jnp.tile(cos, (1, 2)), jnp.tile(sin, (1, 2))


def _largest_divisor(n, candidates):
    for c in candidates:
        if n % c == 0:
            return c
    raise ValueError(f"no tile in {candidates} divides {n}")


def _pool_block_diag(pool_w):
    groups, width, _ = pool_w.shape
    eye = jnp.eye(groups, dtype=pool_w.dtype)
    return jnp.einsum('gcd,gk->gckd', pool_w, eye).reshape(groups * width, groups * width)


def kernel(x, c, ctx, c_ctx, w_mod, b_mod, norm_g, ffn_in, ffn_out, w_in, win_sink, qk_norm, pool_w, pool_scale, ssm_a_re, ssm_a_im, ssm_log_dt, ssm_b_re, ssm_b_im, ssm_c_re, ssm_c_im, ssm_d, glu_w, branch_w, out_w, final_g):
    bsz, n_lat, d = x.shape
    n_ctx = ctx.shape[1]
    depth = w_mod.shape[0]
    t_all = n_lat + n_ctx
    assert d == D_MODEL and bsz < MOD_ROWS and n_lat % TOKEN_TILE == 0 and n_ctx == TOKEN_TILE
    assert n_lat % GRID_W == 0

    cond_rows = jnp.zeros((MOD_ROWS, d), F32).at[:bsz].set(c).at[bsz].set(c_ctx)
    mod = _modulation(cond_rows, w_mod, b_mod)
    n_lat_tiles = n_lat // TOKEN_TILE
    row_of = lambda b, t: jnp.where(t >= n_lat_tiles, bsz, b)
    lat_row = lambda b, t: b

    cos, sin = _rope_tables(n_lat, n_ctx)
    tq = _largest_divisor(n_lat, (2048, 1024, 512, 256))
    tk = _largest_divisor(t_all, (1280, 1024, 768, 512, 256))

    xs = x
    for l in range(depth):
        last = l == depth - 1
        bf = lambda w: w.astype(BF16)
        xs = _ffn(xs, mod, l, 0, row_of, norm_g[l, 0], bf(ffn_in[l, 0]), bf(ffn_out[l, 0]), t_all,
                  tail=ctx if l == 0 else None)

        score_bound = 1.02 * HEAD_DIM ** 0.5 * jnp.max(jnp.abs(qk_norm[l, 0])) * jnp.max(jnp.abs(qk_norm[l, 1]))
        qw, qt, kw, vw, ka, vt, u, xa = _proj(xs, mod, l, row_of, norm_g[l, 1], bf(w_in[l, :, :PROJ_W]),
                                              cos, sin, qk_norm[l, 0], qk_norm[l, 1], score_bound)
        y_b = _window_attention(qw, kw, vw, win_sink[l], n_lat, n_ctx, not last)

        def global_attention(q_row0, n_q, k_row0, n_k, tq_, tk_):
            run = lambda bounded: lambda: _global_attention(qt, ka, vt, q_row0, n_q, k_row0, n_k, tq_, tk_,
                                                            bounded)
            return lax.cond(score_bound <= MAX_SCORE_BOUND, run(True), run(False))

        y_d = global_attention(0, n_lat, 0, t_all, tq, tk)
        tables = _s5_tables(ssm_a_re[l], ssm_a_im[l], ssm_log_dt[l], ssm_b_re[l], ssm_b_im[l],
                            ssm_c_re[l], ssm_c_im[l])
        y_s5 = _s5_scan(u, tables, n_lat, n_ctx)
        n_rows, y_d_ctx = n_lat, y_d
        if not last:
            y_d_ctx = global_attention(n_lat, n_ctx, n_lat, n_ctx, n_ctx, n_ctx)
            n_rows = t_all
        xs = _merge(xs, mod, l, row_of, norm_g[l, 1], xa, y_b, y_s5, u, y_d, y_d_ctx, bf(w_in[l, :, PROJ_W:]),
                    bf(_pool_block_diag(pool_w[l])), pool_scale[l], ssm_d[l], bf(glu_w[l]), bf(branch_w[l]),
                    bf(out_w[l]), n_rows, n_lat, n_ctx)
        xs = _ffn(xs, mod, l, 2, row_of if not last else lat_row, norm_g[l, 2], bf(ffn_in[l, 1]),
                  bf(ffn_out[l, 1]), n_rows, final_g=final_g if last else None)
    return xs
```

```python
import functools
import math

import jax
import jax.numpy as jnp
import numpy as np
from jax import lax
from jax.experimental import pallas as pl
from jax.experimental.pallas import tpu as pltpu

F32 = jnp.float32
BF16 = jnp.bfloat16

D_MODEL = 1024
D_FF = 2816
N_SUB = 3
BRANCH_W = 256
HEAD_DIM = 64
GRID_W = 64
BLOCK = 128
EPS = 1e-6
ROPE_THETA = 10000.0
N_BRANCH = 4
POOL_WINDOWS = (2, 4, 8, 16)
SSM_GROUPS = 16
SSM_GROUP = 16
SSM_STATE = 64
N_STATE = SSM_GROUPS * SSM_STATE
PROJ_W = 6 * BRANCH_W
MOD_ROWS = 8
POOL_HALO = 8
SCAN_ROWS = 8
STATE_CHUNK = 128
WIN_BLOCKS = 2
NEG = -1e30
QK_SCALE = HEAD_DIM ** -0.5
LOG2E = math.log2(math.e)
V_ROWS = HEAD_DIM + 16
MAX_SCORE_BOUND = 40.0
TOKEN_TILE = 256
VMEM_LIMIT = 56 * 1024 * 1024


def _cparams(sem):
    return pltpu.CompilerParams(dimension_semantics=sem, vmem_limit_bytes=VMEM_LIMIT)


def _dot(a, b):
    return jnp.dot(a, b, preferred_element_type=F32)


def _dot_nt(a, b):
    return lax.dot_general(a, b, (((1,), (1,)), ((), ())), preferred_element_type=F32)


def _rms(x, g):
    ms = jnp.mean(x * x, axis=-1, keepdims=True)
    return x * lax.rsqrt(ms + EPS) * g


def _silu(x):
    return x * jax.nn.sigmoid(x)


def _const_spec(shape):
    nd = len(shape)
    return pl.BlockSpec(shape, lambda *_: (0,) * nd)


def _mod_kernel(s_ref, w_ref, b_ref, o_ref):
    s = _silu(s_ref[...])
    w = w_ref[0]
    s_hi = s.astype(BF16)
    s_lo = (s - s_hi.astype(F32)).astype(BF16)
    w_hi = w.astype(BF16)
    w_lo = (w - w_hi.astype(F32)).astype(BF16)
    o_ref[0] = _dot(s_hi, w_hi) + _dot(s_hi, w_lo) + _dot(s_lo, w_hi) + b_ref[0]


def _modulation(rows, w_mod, b_mod):
    depth, d, width = w_mod.shape
    tn = 1024
    out = pl.pallas_call(
        _mod_kernel,
        grid=(depth, width // tn),
        in_specs=[
            pl.BlockSpec((MOD_ROWS, d), lambda l, j: (0, 0)),
            pl.BlockSpec((1, d, tn), lambda l, j: (l, 0, j)),
            pl.BlockSpec((1, 1, tn), lambda l, j: (l, 0, j)),
        ],
        out_specs=pl.BlockSpec((1, MOD_ROWS, tn), lambda l, j: (l, 0, j)),
        out_shape=jax.ShapeDtypeStruct((depth, MOD_ROWS, width), F32),
        compiler_params=_cparams(("parallel", "parallel")),
        name="modulation",
    )(rows, w_mod, b_mod.reshape(depth, 1, width))
    return out.reshape(depth, MOD_ROWS, N_SUB * 3, 1, d).transpose(0, 2, 1, 3, 4)


def _mod_spec(layer, vec, row_of):
    return pl.BlockSpec((None, None, None, 1, D_MODEL), lambda b, t: (layer, vec, row_of(b, t), 0, 0))


def _tail_rows(main_ref, tail_ref, n_main_tiles):
    return jnp.where(pl.program_id(1) >= n_main_tiles, tail_ref[0], main_ref[0])


def _ffn_kernel(*refs, final, n_main_tiles):
    nx = 1 if n_main_tiles is None else 2
    x_cur, x_nxt = refs[:nx], refs[nx:2 * nx]
    sh_ref, sc_ref, gt_ref, shn_ref, scn_ref, ng_ref, win_ref, wout_ref = refs[2 * nx:2 * nx + 8]
    o_ref, h_even, h_odd = refs[-3:]
    t = pl.program_id(1)

    def rows(x_refs, tile):
        return x_refs[0][0] if nx == 1 else jnp.where(tile >= n_main_tiles, x_refs[1][0], x_refs[0][0])

    def normed(x, shift_ref, scale_ref):
        return (_rms(x, ng_ref[...]) * (1.0 + scale_ref[...]) + shift_ref[...]).astype(BF16)

    @pl.when(t == 0)
    def _():
        h_even[...] = normed(rows(x_cur, t), sh_ref, sc_ref)

    def step(h_ref, h_next_ref):
        h_next = normed(rows(x_nxt, t + 1), shn_ref, scn_ref)
        h_next_ref[...] = h_next
        bits = pltpu.bitcast(h_next.astype(F32), jnp.uint32)
        zero = pltpu.bitcast(lax.shift_right_logical(lax.shift_right_logical(bits, jnp.uint32(16)),
                                                     jnp.uint32(16)), F32)
        x = rows(x_cur, t)
        gu = _dot(h_ref[...], win_ref[...])
        act = _silu(gu[:, :D_FF]) * gu[:, D_FF:]
        a = jnp.concatenate([act[:, :D_MODEL] + zero, act[:, D_MODEL:]], axis=1).astype(BF16)
        xn = x + (0.5 * gt_ref[...]) * _dot(a, wout_ref[...])
        if final:
            xn = _rms(xn, refs[2 * nx + 8][...])
        o_ref[0] = xn

    pl.when(t % 2 == 0)(lambda: step(h_even, h_odd))
    pl.when(t % 2 == 1)(lambda: step(h_odd, h_even))


def _tail_specs(tm, width, n_main_tiles, n_tail_tiles=1, ahead=0):
    return (pl.BlockSpec((1, tm, width), lambda b, t: (b, jnp.minimum(t + ahead, n_main_tiles - 1), 0)),
            pl.BlockSpec((1, tm, width),
                         lambda b, t: (b, jnp.clip(t + ahead - n_main_tiles, 0, n_tail_tiles - 1), 0)))


def _ffn(x, mod, layer, sub, row_of, norm_g, w_in, w_out, n_rows, final_g=None, tail=None):
    bsz = x.shape[0]
    tm = TOKEN_TILE
    n_tiles = n_rows // tm
    final = final_g is not None
    n_main_tiles = None if tail is None else x.shape[1] // tm
    nxt = lambda t: jnp.minimum(t + 1, n_tiles - 1)
    if tail is None:
        x_args = [x, x]
        x_specs = [pl.BlockSpec((1, tm, D_MODEL), lambda b, t: (b, t, 0)),
                   pl.BlockSpec((1, tm, D_MODEL), lambda b, t: (b, nxt(t), 0))]
    else:
        n_tail_tiles = tail.shape[1] // tm
        x_args = [x, tail, x, tail]
        x_specs = [*_tail_specs(tm, D_MODEL, n_main_tiles, n_tail_tiles),
                   *_tail_specs(tm, D_MODEL, n_main_tiles, n_tail_tiles, ahead=1)]
    next_row = lambda b, t: row_of(b, nxt(t))
    in_specs = x_specs + [
        _mod_spec(layer, sub * 3 + 0, row_of),
        _mod_spec(layer, sub * 3 + 1, row_of),
        _mod_spec(layer, sub * 3 + 2, row_of),
        _mod_spec(layer, sub * 3 + 0, next_row),
        _mod_spec(layer, sub * 3 + 1, next_row),
        _const_spec((1, D_MODEL)),
        pl.BlockSpec((D_MODEL, 2 * D_FF), lambda b, t: (0, 0), pipeline_mode=pl.Buffered(1)),
        pl.BlockSpec((D_FF, D_MODEL), lambda b, t: (0, 0), pipeline_mode=pl.Buffered(1)),
    ]
    args = x_args + [mod, mod, mod, mod, mod, norm_g.reshape(1, D_MODEL), w_in, w_out]
    if final:
        in_specs.append(_const_spec((1, D_MODEL)))
        args.append(final_g.reshape(1, D_MODEL))
    return pl.pallas_call(
        functools.partial(_ffn_kernel, final=final, n_main_tiles=n_main_tiles),
        grid=(bsz, n_tiles),
        in_specs=in_specs,
        out_specs=pl.BlockSpec((1, tm, D_MODEL), lambda b, t: (b, t, 0)),
        out_shape=jax.ShapeDtypeStruct((bsz, n_rows, D_MODEL), F32),
        scratch_shapes=[pltpu.VMEM((tm, D_MODEL), BF16)] * 2,
        compiler_params=_cparams(("parallel", "arbitrary")),
        name="ffn_final" if final else "ffn",
    )(*args)


def _rope(x, cos, sin, first_half):
    w = x.shape[-1]
    swapped = jnp.where(first_half, pltpu.roll(x, w - 16, 1), pltpu.roll(x, 16, 1))
    return x * cos + swapped * sin


def _head_rms(x, g, low_head):
    x2 = x * x
    lo = jnp.sum(jnp.where(low_head, x2, 0.0), axis=-1, keepdims=True)
    hi = jnp.sum(jnp.where(low_head, 0.0, x2), axis=-1, keepdims=True)
    ms = jnp.where(low_head, lo, hi) * (1.0 / HEAD_DIM)
    return x * lax.rsqrt(ms + EPS) * g


def _store_padded(ref, x, low_head):
    sw = pltpu.roll(x, HEAD_DIM, 1)
    zero = jnp.zeros_like(x)
    ref[0, 0, 0] = jnp.where(low_head, x, zero).astype(ref.dtype)
    ref[0, 0, 1] = jnp.where(low_head, zero, sw).astype(ref.dtype)
    ref[0, 1, 0] = jnp.where(low_head, sw, zero).astype(ref.dtype)
    ref[0, 1, 1] = jnp.where(low_head, zero, x).astype(ref.dtype)


def _proj_kernel(x_ref, sh_ref, sc_ref, ng_ref, w_ref, cos_ref, sin_ref, qg_ref, kg_ref, qb_ref, kvb_ref,
                 qw_ref, qt_ref, kw_ref, vw_ref, ka_ref, vt_ref, u_ref, xa_ref):
    x = x_ref[0]
    tm = x.shape[0]
    h = (_rms(x, ng_ref[...]) * (1.0 + sc_ref[...]) + sh_ref[...]).astype(BF16)
    p = _dot(h, w_ref[...])
    lane = lax.broadcasted_iota(jnp.int32, (tm, 2 * HEAD_DIM), 1)
    low_head = lane < HEAD_DIM
    first_half = jnp.bitwise_and(lane, 31) < 16
    cos = cos_ref[...]
    sin = sin_ref[...]
    rope = lambda z: _rope(z, cos, sin, first_half)
    k_win = rope(p[:, 0:128])
    v_win = p[:, 128:256]
    u_ref[0] = p[:, 256:512]
    k_glb = rope(_head_rms(p[:, 512:640], kg_ref[...], low_head))
    v_glb = p[:, 640:768]
    heads = lambda z: (z, pltpu.roll(z, HEAD_DIM, 1))
    for c in range(2):
        qw = rope(p[:, 768 + 128 * c:896 + 128 * c]) * QK_SCALE
        qw_ref[0, :, 128 * c:128 * (c + 1)] = qw.astype(BF16)
        qd = rope(_head_rms(p[:, 1024 + 128 * c:1152 + 128 * c], qg_ref[...], low_head)) * (QK_SCALE * LOG2E)
        for g, z in enumerate(heads(qd)):
            qt_ref[0, 2 * c + g] = jnp.where(low_head, z, qb_ref[...]).T.astype(BF16)
    for j, (kz, vz) in enumerate(zip(heads(k_glb), heads(v_glb))):
        ka_ref[0, j] = jnp.where(low_head, kz, kvb_ref[...]).astype(BF16)
        vt_ref[0, j] = jnp.where(low_head, vz, kvb_ref[...]).T[0:V_ROWS].astype(BF16)
    xa_ref[0] = p[:, 1280:1536]
    _store_padded(kw_ref, k_win, low_head)
    _store_padded(vw_ref, v_win, low_head)


def _proj(x, mod, layer, row_of, norm_g, w, cos, sin, q_g, k_g, score_bound):
    bsz, t_all, _ = x.shape
    tm = TOKEN_TILE
    tok = lambda width: pl.BlockSpec((1, tm, width), lambda b, t: (b, t, 0))
    pad = pl.BlockSpec((1, 2, 2, tm, 128), lambda b, t: (b, 0, 0, t, 0))
    tab = pl.BlockSpec((tm, 128), lambda b, t: (t, 0))
    pad_shape = jax.ShapeDtypeStruct((bsz, 2, 2, t_all, 128), BF16)
    marker = jnp.zeros((1, 128), F32).at[0, HEAD_DIM].set(1.0)
    return pl.pallas_call(
        _proj_kernel,
        grid=(bsz, t_all // tm),
        in_specs=[
            tok(D_MODEL),
            _mod_spec(layer, 3, row_of),
            _mod_spec(layer, 4, row_of),
            _const_spec((1, D_MODEL)),
            pl.BlockSpec((D_MODEL, PROJ_W), lambda b, t: (0, 0), pipeline_mode=pl.Buffered(1)),
            tab, tab,
            _const_spec((1, 128)), _const_spec((1, 128)), _const_spec((1, 128)), _const_spec((1, 128)),
        ],
        out_specs=[
            tok(256),
            pl.BlockSpec((1, 4, 128, tm), lambda b, t: (b, 0, 0, t)),
            pad, pad,
            pl.BlockSpec((1, 2, tm, 128), lambda b, t: (b, 0, t, 0)),
            pl.BlockSpec((1, 2, V_ROWS, tm), lambda b, t: (b, 0, 0, t)),
            tok(256), tok(256),
        ],
        out_shape=[
            jax.ShapeDtypeStruct((bsz, t_all, 256), BF16),
            jax.ShapeDtypeStruct((bsz, 4, 128, t_all), BF16),
            pad_shape, pad_shape,
            jax.ShapeDtypeStruct((bsz, 2, t_all, 128), BF16),
            jax.ShapeDtypeStruct((bsz, 2, V_ROWS, t_all), BF16),
            jax.ShapeDtypeStruct((bsz, t_all, 256), F32),
            jax.ShapeDtypeStruct((bsz, t_all, 256), F32),
        ],
        compiler_params=_cparams(("parallel", "parallel")),
        name="proj",
    )(x, mod, mod, norm_g.reshape(1, D_MODEL), w, cos, sin,
      jnp.tile(q_g, 2).reshape(1, 128), jnp.tile(k_g, 2).reshape(1, 128),
      marker * (-score_bound * LOG2E), marker)


def _win_kernel(sink_ref, q_ref, kp_ref, kc_ref, kn_ref, kx_ref, vp_ref, vc_ref, vn_ref, vx_ref,
                o_ref, *, n_lat_blocks):
    i = pl.program_id(1)
    tile = WIN_BLOCKS * BLOCK
    band = tile + 2 * BLOCK
    r = lax.broadcasted_iota(jnp.int32, (tile, band), 0)
    c = lax.broadcasted_iota(jnp.int32, (tile, band), 1)
    blk0 = i * WIN_BLOCKS
    lo = jnp.where(blk0 >= 1, 0, BLOCK)
    hi = jnp.where(blk0 < n_lat_blocks, jnp.minimum((n_lat_blocks - blk0 + 1) * BLOCK, band), 0)
    in_window = (c - r).astype(jnp.uint32) <= 2 * BLOCK
    in_range = (c - lo).astype(jnp.uint32) < (hi - lo).astype(jnp.uint32)
    bias = jnp.where(in_window, jnp.where(in_range, 0.0, NEG), NEG)
    for j in range(2):
        q = q_ref[0, :, 128 * j:128 * (j + 1)]
        acc = jnp.zeros((tile, 128), F32)
        for g in range(2):
            k_band = jnp.concatenate([kp_ref[0, j, g], kc_ref[0, j, g], kn_ref[0, j, g]], axis=0)
            s = jnp.concatenate([_dot_nt(q, k_band) + bias, _dot_nt(q, kx_ref[0, j, g])], axis=1)
            v = jnp.concatenate([vp_ref[0, j, g], vc_ref[0, j, g], vn_ref[0, j, g], vx_ref[0, j, g]], axis=0)
            sink = sink_ref[2 * j + g]
            m = jnp.maximum(s.max(-1, keepdims=True), sink)
            p = jnp.exp(s - m)
            den = p.sum(-1, keepdims=True) + jnp.exp(sink - m)
            acc = acc + _dot(p.astype(BF16), v) / den
        o_ref[0, :, 128 * j:128 * (j + 1)] = acc.astype(o_ref.dtype)


def _window_attention(q, k_pad, v_pad, sink, n_lat, n_ctx, with_ctx):
    bsz, t_all, _ = q.shape
    n_lat_blocks = n_lat // BLOCK
    n_blocks = t_all // BLOCK
    n_q_blocks = n_blocks if with_ctx else n_lat_blocks
    assert n_q_blocks % WIN_BLOCKS == 0 and n_lat_blocks % WIN_BLOCKS == 0
    tile = WIN_BLOCKS * BLOCK
    edge = lambda off: pl.BlockSpec(
        (1, 2, 2, BLOCK, 128), lambda b, i: (b, 0, 0, jnp.clip(i * WIN_BLOCKS + off, 0, n_blocks - 1), 0))
    cur = pl.BlockSpec((1, 2, 2, tile, 128), lambda b, i: (b, 0, 0, i, 0))
    ctx = pl.BlockSpec((1, 2, 2, n_ctx, 128), lambda b, i: (b, 0, 0, n_lat // n_ctx, 0))
    return pl.pallas_call(
        functools.partial(_win_kernel, n_lat_blocks=n_lat_blocks),
        grid=(bsz, n_q_blocks // WIN_BLOCKS),
        in_specs=[
            pl.BlockSpec(memory_space=pltpu.SMEM),
            pl.BlockSpec((1, tile, 256), lambda b, i: (b, i, 0)),
            edge(-1), cur, edge(WIN_BLOCKS), ctx,
            edge(-1), cur, edge(WIN_BLOCKS), ctx,
        ],
        out_specs=pl.BlockSpec((1, tile, 256), lambda b, i: (b, i, 0)),
        out_shape=jax.ShapeDtypeStruct((bsz, n_q_blocks * BLOCK, 256), BF16),
        compiler_params=_cparams(("parallel", "parallel")),
        name="window_attention",
    )(sink, q, k_pad, k_pad, k_pad, k_pad, v_pad, v_pad, v_pad, v_pad)


def _glb_finish(acc_ref, o_ref):
    o_t = jnp.concatenate([acc_ref[g, 0:HEAD_DIM] / acc_ref[g, HEAD_DIM:HEAD_DIM + 1] for g in range(2)],
                          axis=0)
    o_ref[0] = o_t.T.astype(o_ref.dtype)


def _glb_bounded_kernel(qt_ref, k_ref, vt_ref, vtp_ref, o_ref, acc_ref, p_ref):
    ki = pl.program_id(3)
    last = pl.num_programs(3) - 1

    @pl.when(ki == 0)
    def _():
        acc_ref[...] = jnp.zeros_like(acc_ref)
        p_ref[...] = jnp.zeros_like(p_ref)

    @pl.when(ki < last)
    def _():
        acc_ref[1] += _dot(vtp_ref[0, 0], p_ref[...])
        p = jnp.exp2(_dot(k_ref[0, 0], qt_ref[0, 0])).astype(BF16)
        acc_ref[0] += _dot(vt_ref[0, 0], p)
        p_ref[...] = jnp.exp2(_dot(k_ref[0, 0], qt_ref[0, 1])).astype(BF16)

    @pl.when(ki == last)
    def _():
        acc_ref[1] += _dot(vtp_ref[0, 0], p_ref[...])
        _glb_finish(acc_ref, o_ref)


def _glb_online_kernel(qt_ref, k_ref, vt_ref, o_ref, acc_ref, m_ref):
    ki = pl.program_id(3)

    @pl.when(ki == 0)
    def _():
        acc_ref[...] = jnp.zeros_like(acc_ref)
        m_ref[...] = jnp.full_like(m_ref, NEG)

    for g in range(2):
        s = _dot(k_ref[0, 0], qt_ref[0, g])
        m_prev = m_ref[g, 0:1]
        m_new = jnp.maximum(m_prev, s.max(axis=0, keepdims=True))
        p = jnp.exp2(s - m_new).astype(BF16)
        acc_ref[g] = jnp.exp2(m_prev - m_new) * acc_ref[g] + _dot(vt_ref[0, 0], p)
        m_ref[g] = jnp.broadcast_to(m_new, m_ref.shape[1:])

    @pl.when(ki == pl.num_programs(3) - 1)
    def _():
        _glb_finish(acc_ref, o_ref)


def _global_attention(qt, k_aug, vt_aug, q_row0, n_q, k_row0, n_k, tq, tk, bounded):
    bsz = qt.shape[0]
    q0, k0, nk = q_row0 // tq, k_row0 // tk, n_k // tk
    key = lambda ki, back=0: k0 + jnp.clip(ki - back, 0, nk - 1)
    in_specs = [
        pl.BlockSpec((1, 2, 128, tq), lambda b, j, qi, ki: (b, j, 0, q0 + qi)),
        pl.BlockSpec((1, 1, tk, 128), lambda b, j, qi, ki: (b, j, key(ki), 0)),
        pl.BlockSpec((1, 1, V_ROWS, tk), lambda b, j, qi, ki: (b, j, 0, key(ki))),
    ]
    args = [qt, k_aug, vt_aug]
    scratch = [pltpu.VMEM((2, V_ROWS, tq), F32)]
    if bounded:
        in_specs.append(pl.BlockSpec((1, 1, V_ROWS, tk), lambda b, j, qi, ki: (b, j, 0, key(ki, 1))))
        args.append(vt_aug)
        scratch.append(pltpu.VMEM((tk, tq), BF16))
    else:
        scratch.append(pltpu.VMEM((2, 8, tq), F32))
    return pl.pallas_call(
        _glb_bounded_kernel if bounded else _glb_online_kernel,
        grid=(bsz, 2, n_q // tq, nk + 1 if bounded else nk),
        in_specs=in_specs,
        out_specs=pl.BlockSpec((1, tq, 128), lambda b, j, qi, ki: (b, qi, j)),
        out_shape=jax.ShapeDtypeStruct((bsz, n_q, 256), BF16),
        scratch_shapes=scratch,
        compiler_params=_cparams(("parallel", "parallel", "parallel", "arbitrary")),
        name="global_attention" if bounded else "global_attention_online",
    )(*args)


def _s5_tile(u, bmat_ref, cmat_ref, a_ref, pc_ref, carry_ref, reverse):
    rows = u.shape[0]
    n_groups = rows // SCAN_ROWS
    w = STATE_CHUNK
    ub = u.astype(BF16)
    order = range(n_groups - 1, -1, -1) if reverse else range(n_groups)
    last = 0 if reverse else SCAN_ROWS - 1
    y = None
    for c in range(N_STATE // w):
        bu = _dot(ub, bmat_ref[0, c])
        pr, pi = pc_ref[0, c, :, :w], pc_ref[0, c, :, w:]
        cr, ci = carry_ref[c, :, :w], carry_ref[c, :, w:]
        states = [None] * n_groups
        for g in order:
            vr = bu[SCAN_ROWS * g:SCAN_ROWS * (g + 1), :w]
            vi = bu[SCAN_ROWS * g:SCAN_ROWS * (g + 1), w:]
            for s, k in enumerate((1, 2, 4)):
                ar, ai = a_ref[0, s, c, :, :w], a_ref[0, s, c, :, w:]
                shift = SCAN_ROWS - k if reverse else k
                sr = pltpu.roll(vr, shift, 0)
                si = pltpu.roll(vi, shift, 0)
                vr, vi = vr + ar * sr - ai * si, vi + ar * si + ai * sr
            vr, vi = vr + pr * cr - pi * ci, vi + pr * ci + pi * cr
            cr = jnp.broadcast_to(vr[last:last + 1], (SCAN_ROWS, w))
            ci = jnp.broadcast_to(vi[last:last + 1], (SCAN_ROWS, w))
            states[g] = jnp.concatenate([vr, vi], axis=1)
        carry_ref[c, :, :w] = cr
        carry_ref[c, :, w:] = ci
        part = _dot(jnp.concatenate(states, axis=0).astype(BF16), cmat_ref[0, c])
        y = part if y is None else y + part
    return y


def _s5_kernel(u_ref, bmat_ref, cmat_ref, a_ref, pc_ref, y_ref, carry_ref):
    direction = pl.program_id(0)
    step = pl.program_id(2)

    @pl.when(step == 0)
    def _():
        carry_ref[...] = jnp.zeros_like(carry_ref)

    for reverse in (False, True):
        @pl.when(direction == int(reverse))
        def _(reverse=reverse):
            y_ref[0, 0] = _s5_tile(u_ref[0], bmat_ref, cmat_ref, a_ref, pc_ref, carry_ref, reverse)


def _s5_tables(a_re, a_im, log_dt, b_re, b_im, c_re, c_im):
    dt = jnp.exp(log_dt)[..., None]
    re_dt, im_dt = a_re * dt, a_im * dt
    lb_re, lb_im = jnp.exp(re_dt) * jnp.cos(im_dt), jnp.exp(re_dt) * jnp.sin(im_dt)
    den = a_re * a_re + a_im * a_im
    f_re = ((lb_re - 1.0) * a_re + lb_im * a_im) / den
    f_im = (lb_im * a_re - (lb_re - 1.0) * a_im) / den
    bb_re = f_re[..., None] * b_re - f_im[..., None] * b_im
    bb_im = f_re[..., None] * b_im + f_im[..., None] * b_re
    eye = jnp.eye(SSM_GROUPS, dtype=F32)
    in_w, st_w, n_chunks = SSM_GROUPS * SSM_GROUP, N_STATE, N_STATE // STATE_CHUNK
    b_blk = lambda z: jnp.einsum('dgph,gk->dghkp', z, eye).reshape(2, in_w, n_chunks, STATE_CHUNK)
    bmat = jnp.concatenate([b_blk(bb_re), b_blk(bb_im)], axis=-1).transpose(0, 2, 1, 3).astype(BF16)
    c_blk = lambda z: jnp.einsum('dghp,gk->dgpkh', z, eye).reshape(2, n_chunks, STATE_CHUNK, in_w)
    cmat = jnp.concatenate([c_blk(c_re), -c_blk(c_im)], axis=2).astype(BF16)
    steps = jnp.arange(1, SCAN_ROWS + 1, dtype=F32)[None, :, None, None]
    mag = jnp.exp(re_dt[:, None] * steps)
    ang = im_dt[:, None] * steps
    chunked = lambda z: z.reshape(2, SCAN_ROWS, n_chunks, STATE_CHUNK)
    powers = jnp.concatenate([chunked(mag * jnp.cos(ang)), chunked(mag * jnp.sin(ang))], axis=-1)
    row = jnp.arange(SCAN_ROWS)[:, None, None]
    pc = jnp.stack([powers[0], powers[1][::-1]]).transpose(0, 2, 1, 3)
    a_tabs = []
    for k in (1, 2, 4):
        fwd = jnp.where(row >= k, powers[0, k - 1][None], 0.0)
        bwd = jnp.where(row <= SCAN_ROWS - 1 - k, powers[1, k - 1][None], 0.0)
        a_tabs.append(jnp.stack([fwd, bwd]).transpose(0, 2, 1, 3))
    return bmat, cmat, jnp.stack(a_tabs, axis=1), pc


def _s5_scan(u, tables, n_lat, n_ctx):
    bsz, t_all, width = u.shape
    tm = TOKEN_TILE
    assert n_ctx == tm
    nt = n_lat // tm
    bmat, cmat, a_tab, pc = tables
    n_chunks, lanes = N_STATE // STATE_CHUNK, 2 * STATE_CHUNK
    tile_of = lambda d, s: jnp.where(s == 0, nt, jnp.where(d == 0, s - 1, nt - s))
    return pl.pallas_call(
        _s5_kernel,
        grid=(2, bsz, nt + 1),
        in_specs=[
            pl.BlockSpec((1, tm, width), lambda d, b, s: (b, tile_of(d, s), 0)),
            pl.BlockSpec((1, n_chunks, width, lanes), lambda d, b, s: (d, 0, 0, 0)),
            pl.BlockSpec((1, n_chunks, lanes, width), lambda d, b, s: (d, 0, 0, 0)),
            pl.BlockSpec((1, 3, n_chunks, SCAN_ROWS, lanes), lambda d, b, s: (d, 0, 0, 0, 0)),
            pl.BlockSpec((1, n_chunks, SCAN_ROWS, lanes), lambda d, b, s: (d, 0, 0, 0)),
        ],
        out_specs=pl.BlockSpec((1, 1, tm, width), lambda d, b, s: (d, b, tile_of(d, s), 0)),
        out_shape=jax.ShapeDtypeStruct((2, bsz, t_all, width), F32),
        scratch_shapes=[pltpu.VMEM((n_chunks, SCAN_ROWS, lanes), F32)],
        compiler_params=_cparams(("parallel", "parallel", "arbitrary")),
        name="s5_scan",
    )(u, bmat, cmat, a_tab, pc)


def _gelu_tanh(x):
    c = math.sqrt(2.0 / math.pi)
    return x * (0.5 * (1.0 + jnp.tanh(c * (x + 0.044715 * (x * x * x)))))


def _merge_kernel(x_ref, sh_ref, sc_ref, gt_ref, ng_ref, xa_ref, xap_ref, xan_ref, yb_ref, yf_ref, yr_ref,
                  u_ref, yd_ref, ydt_ref, wg_ref, pw_ref, ps_ref, dsk_ref, glu_ref, wb_ref, wo_ref, o_ref, ext_ref,
                  *, n_lat_tiles, n_lat, n_ctx):
    t = pl.program_id(1)
    x = x_ref[0]
    tm = x.shape[0]
    bw = BRANCH_W
    h = (_rms(x, ng_ref[...]) * (1.0 + sc_ref[...]) + sh_ref[...]).astype(BF16)
    gates = [jax.nn.sigmoid(_dot(h, wg_ref[:, D_MODEL * k:D_MODEL * (k + 1)])) for k in range(N_BRANCH)]
    is_ctx = t >= n_lat_tiles
    seg_tile = jnp.where(is_ctx, t - n_lat_tiles, t)
    seg_tiles = jnp.where(is_ctx, n_ctx // tm, n_lat_tiles)
    seg_len = jnp.where(is_ctx, n_ctx, n_lat)
    xa = xa_ref[0]
    ext_ref[0:POOL_HALO, :] = jnp.where(seg_tile > 0, xap_ref[0], 0.0)
    ext_ref[POOL_HALO:POOL_HALO + tm, :] = xa
    ext_ref[POOL_HALO + tm:, :] = jnp.where(seg_tile + 1 < seg_tiles, xan_ref[0], 0.0)
    at = lambda off: ext_ref[POOL_HALO + off:POOL_HALO + off + tm, :]
    sums = []
    acc = at(-1) + xa
    sums.append(acc)
    for half in (2, 4, 8):
        for off in range(half // 2, half):
            acc = acc + at(-off - 1) + at(off)
        sums.append(acc)
    lane = lax.broadcasted_iota(jnp.int32, (tm, bw), 1)
    pos = seg_tile * tm + lax.broadcasted_iota(jnp.int32, (tm, bw), 0)
    group = lane // (bw // len(POOL_WINDOWS))
    half_w = jnp.left_shift(1, group)
    count = jnp.minimum(pos + half_w, seg_len) - jnp.maximum(pos - half_w, 0)
    window_sum = jnp.where(group == 0, sums[0], jnp.where(group == 1, sums[1],
                                                           jnp.where(group == 2, sums[2], sums[3])))
    diff = window_sum / count.astype(F32) - xa
    y_a = _dot(diff.astype(BF16), pw_ref[...]) * ps_ref[...]
    y_s = yf_ref[0, 0] + yr_ref[0, 0] + dsk_ref[...] * u_ref[0]
    z = _dot(_gelu_tanh(y_s).astype(BF16), glu_ref[...])
    y_c = z[:, :bw] * jax.nn.sigmoid(z[:, bw:])
    branches = (y_a.astype(BF16), yb_ref[0], y_c.astype(BF16), _tail_rows(yd_ref, ydt_ref, n_lat_tiles))
    total = None
    for gate, y_k, k in zip(gates, branches, range(N_BRANCH)):
        term = gate * _dot(y_k, wb_ref[k])
        total = term if total is None else total + term
    o_ref[0] = x + gt_ref[...] * _dot(total.astype(BF16), wo_ref[...])


def _merge(x, mod, layer, row_of, norm_g, xa, y_b, y_s5, u, y_d, y_d_tail, w_gate, pool_bd, pool_scale, d_skip,
           glu_w, branch_w, out_w, n_rows, n_lat, n_ctx):
    bsz = x.shape[0]
    tm = TOKEN_TILE
    assert n_ctx % tm == 0 or n_rows == n_lat
    halo_blocks = xa.shape[1] // POOL_HALO
    per_tile = tm // POOL_HALO
    tok = lambda width: pl.BlockSpec((1, tm, width), lambda b, t: (b, t, 0))
    single = dict(pipeline_mode=pl.Buffered(1))
    return pl.pallas_call(
        functools.partial(_merge_kernel, n_lat_tiles=n_lat // tm, n_lat=n_lat, n_ctx=n_ctx),
        grid=(bsz, n_rows // tm),
        in_specs=[
            tok(D_MODEL),
            _mod_spec(layer, 3, row_of), _mod_spec(layer, 4, row_of), _mod_spec(layer, 5, row_of),
            _const_spec((1, D_MODEL)),
            tok(256),
            pl.BlockSpec((1, POOL_HALO, 256), lambda b, t: (b, jnp.maximum(t * per_tile - 1, 0), 0)),
            pl.BlockSpec((1, POOL_HALO, 256),
                         lambda b, t: (b, jnp.minimum((t + 1) * per_tile, halo_blocks - 1), 0)),
            tok(256),
            pl.BlockSpec((1, 1, tm, 256), lambda b, t: (0, b, t, 0)),
            pl.BlockSpec((1, 1, tm, 256), lambda b, t: (1, b, t, 0)),
            tok(256), *_tail_specs(tm, 256, n_lat // tm),
            pl.BlockSpec((D_MODEL, N_BRANCH * D_MODEL), lambda b, t: (0, 0), **single),
            _const_spec((256, 256)), _const_spec((1, 256)), _const_spec((1, 256)),
            _const_spec((256, 512)),
            pl.BlockSpec((N_BRANCH, 256, D_MODEL), lambda b, t: (0, 0, 0), **single),
            pl.BlockSpec((D_MODEL, D_MODEL), lambda b, t: (0, 0), **single),
        ],
        out_specs=tok(D_MODEL),
        out_shape=jax.ShapeDtypeStruct((bsz, n_rows, D_MODEL), F32),
        scratch_shapes=[pltpu.VMEM((tm + 2 * POOL_HALO, 256), F32)],
        compiler_params=_cparams(("parallel", "parallel")),
        name="merge",
    )(x, mod, mod, mod, norm_g.reshape(1, D_MODEL), xa, xa, xa, y_b, y_s5, y_s5, u, y_d, y_d_tail,
      w_gate, pool_bd, pool_scale.reshape(1, 256), d_skip.reshape(1, 256), glu_w, branch_w, out_w)


def _rope_tables(n_lat, n_ctx):
    rows = n_lat // GRID_W
    n_freq = HEAD_DIM // 4
    row = jnp.repeat(jnp.arange(rows), GRID_W)
    col = jnp.tile(jnp.arange(GRID_W), rows)
    inv = ROPE_THETA ** (-jnp.arange(n_freq, dtype=F32) / n_freq)
    ang = jnp.stack([row, col], axis=-1).astype(F32)[..., None] * inv
    cos = jnp.broadcast_to(jnp.cos(ang)[:, :, None, :], (n_lat, 2, 2, n_freq)).reshape(n_lat, HEAD_DIM)
    sign = jnp.array([-1.0, 1.0], F32)[None, None, :, None]
    sin = (jnp.sin(ang)[:, :, None, :] * sign).reshape(n_lat, HEAD_DIM)
    cos = jnp.concatenate([cos, jnp.ones((n_ctx, HEAD_DIM), F32)], axis=0)
    sin = jnp.concatenate([sin, jnp.zeros((n_ctx, HEAD_DIM), F32)], axis=0)
    return jnp.tile(cos, (1, 2)), jnp.tile(sin, (1, 2))


def _largest_divisor(n, candidates):
    for c in candidates:
        if n % c == 0:
            return c
    raise ValueError(f"no tile in {candidates} divides {n}")


def _pool_block_diag(pool_w):
    groups, width, _ = pool_w.shape
    eye = jnp.eye(groups, dtype=pool_w.dtype)
    return jnp.einsum('gcd,gk->gckd', pool_w, eye).reshape(groups * width, groups * width)


def kernel(x, c, ctx, c_ctx, w_mod, b_mod, norm_g, ffn_in, ffn_out, w_in, win_sink, qk_norm, pool_w, pool_scale, ssm_a_re, ssm_a_im, ssm_log_dt, ssm_b_re, ssm_b_im, ssm_c_re, ssm_c_im, ssm_d, glu_w, branch_w, out_w, final_g):
    bsz, n_lat, d = x.shape
    n_ctx = ctx.shape[1]
    depth = w_mod.shape[0]
    t_all = n_lat + n_ctx
    assert d == D_MODEL and bsz < MOD_ROWS and n_lat % TOKEN_TILE == 0 and n_ctx == TOKEN_TILE
    assert n_lat % GRID_W == 0

    cond_rows = jnp.zeros((MOD_ROWS, d), F32).at[:bsz].set(c).at[bsz].set(c_ctx)
    mod = _modulation(cond_rows, w_mod, b_mod)
    n_lat_tiles = n_lat // TOKEN_TILE
    row_of = lambda b, t: jnp.where(t >= n_lat_tiles, bsz, b)
    lat_row = lambda b, t: b

    cos, sin = _rope_tables(n_lat, n_ctx)
    tq = _largest_divisor(n_lat, (2048, 1024, 512, 256))
    tk = _largest_divisor(t_all, (1280, 1024, 768, 512, 256))

    xs = x
    for l in range(depth):
        last = l == depth - 1
        bf = lambda w: w.astype(BF16)
        xs = _ffn(xs, mod, l, 0, row_of, norm_g[l, 0], bf(ffn_in[l, 0]), bf(ffn_out[l, 0]), t_all,
                  tail=ctx if l == 0 else None)

        score_bound = 1.02 * HEAD_DIM ** 0.5 * jnp.max(jnp.abs(qk_norm[l, 0])) * jnp.max(jnp.abs(qk_norm[l, 1]))
        qw, qt, kw, vw, ka, vt, u, xa = _proj(xs, mod, l, row_of, norm_g[l, 1], bf(w_in[l, :, :PROJ_W]),
                                              cos, sin, qk_norm[l, 0], qk_norm[l, 1], score_bound)
        y_b = _window_attention(qw, kw, vw, win_sink[l], n_lat, n_ctx, not last)

        def global_attention(q_row0, n_q, k_row0, n_k, tq_, tk_):
            run = lambda bounded: lambda: _global_attention(qt, ka, vt, q_row0, n_q, k_row0, n_k, tq_, tk_,
                                                            bounded)
            return lax.cond(score_bound <= MAX_SCORE_BOUND, run(True), run(False))

        y_d = global_attention(0, n_lat, 0, t_all, tq, tk)
        tables = _s5_tables(ssm_a_re[l], ssm_a_im[l], ssm_log_dt[l], ssm_b_re[l], ssm_b_im[l],
                            ssm_c_re[l], ssm_c_im[l])
        y_s5 = _s5_scan(u, tables, n_lat, n_ctx)
        n_rows, y_d_ctx = n_lat, y_d
        if not last:
            y_d_ctx = global_attention(n_lat, n_ctx, n_lat, n_ctx, n_ctx, n_ctx)
            n_rows = t_all
        xs = _merge(xs, mod, l, row_of, norm_g[l, 1], xa, y_b, y_s5, u, y_d, y_d_ctx, bf(w_in[l, :, PROJ_W:]),
                    bf(_pool_block_diag(pool_w[l])), pool_scale[l], ssm_d[l], bf(glu_w[l]), bf(branch_w[l]),
                    bf(out_w[l]), n_rows, n_lat, n_ctx)
        xs = _ffn(xs, mod, l, 2, row_of if not last else lat_row, norm_g[l, 2], bf(ffn_in[l, 1]),
                  bf(ffn_out[l, 1]), n_rows, final_g=final_g if last else None)
    return xs
```

```python
import functools
import math

import jax
import jax.numpy as jnp
import numpy as np
from jax import lax
from jax.experimental import pallas as pl
from jax.experimental.pallas import tpu as pltpu

F32 = jnp.float32
BF16 = jnp.bfloat16

D_MODEL = 1024
D_FF = 2816
N_SUB = 3
BRANCH_W = 256
HEAD_DIM = 64
GRID_W = 64
BLOCK = 128
EPS = 1e-6
ROPE_THETA = 10000.0
N_BRANCH = 4
POOL_WINDOWS = (2, 4, 8, 16)
SSM_GROUPS = 16
SSM_GROUP = 16
SSM_STATE = 64
N_STATE = SSM_GROUPS * SSM_STATE
PROJ_W = 6 * BRANCH_W
MOD_ROWS = 8
POOL_HALO = 8
SCAN_ROWS = 8
STATE_CHUNK = 128
WIN_BLOCKS = 2
NEG = -1e30
QK_SCALE = HEAD_DIM ** -0.5
LOG2E = math.log2(math.e)
V_ROWS = HEAD_DIM + 16
MAX_SCORE_BOUND = 40.0
TOKEN_TILE = 256
VMEM_LIMIT = 56 * 1024 * 1024


def _cparams(sem):
    return pltpu.CompilerParams(dimension_semantics=sem, vmem_limit_bytes=VMEM_LIMIT)


def _dot(a, b):
    return jnp.dot(a, b, preferred_element_type=F32)


def _dot_nt(a, b):
    return lax.dot_general(a, b, (((1,), (1,)), ((), ())), preferred_element_type=F32)


def _rms(x, g):
    ms = jnp.mean(x * x, axis=-1, keepdims=True)
    return x * lax.rsqrt(ms + EPS) * g


def _silu(x):
    return x * jax.nn.sigmoid(x)


def _const_spec(shape):
    nd = len(shape)
    return pl.BlockSpec(shape, lambda *_: (0,) * nd)


def _mod_kernel(s_ref, w_ref, b_ref, o_ref):
    s = _silu(s_ref[...])
    w = w_ref[0]
    s_hi = s.astype(BF16)
    s_lo = (s - s_hi.astype(F32)).astype(BF16)
    w_hi = w.astype(BF16)
    w_lo = (w - w_hi.astype(F32)).astype(BF16)
    o_ref[0] = _dot(s_hi, w_hi) + _dot(s_hi, w_lo) + _dot(s_lo, w_hi) + b_ref[0]


def _modulation(rows, w_mod, b_mod):
    depth, d, width = w_mod.shape
    tn = 1024
    out = pl.pallas_call(
        _mod_kernel,
        grid=(depth, width // tn),
        in_specs=[
            pl.BlockSpec((MOD_ROWS, d), lambda l, j: (0, 0)),
            pl.BlockSpec((1, d, tn), lambda l, j: (l, 0, j)),
            pl.BlockSpec((1, 1, tn), lambda l, j: (l, 0, j)),
        ],
        out_specs=pl.BlockSpec((1, MOD_ROWS, tn), lambda l, j: (l, 0, j)),
        out_shape=jax.ShapeDtypeStruct((depth, MOD_ROWS, width), F32),
        compiler_params=_cparams(("parallel", "parallel")),
        name="modulation",
    )(rows, w_mod, b_mod.reshape(depth, 1, width))
    return out.reshape(depth, MOD_ROWS, N_SUB * 3, 1, d).transpose(0, 2, 1, 3, 4)


def _mod_spec(layer, vec, row_of):
    return pl.BlockSpec((None, None, None, 1, D_MODEL), lambda b, t: (layer, vec, row_of(b, t), 0, 0))


def _tail_rows(main_ref, tail_ref, n_main_tiles):
    return jnp.where(pl.program_id(1) >= n_main_tiles, tail_ref[0], main_ref[0])


def _ffn_kernel(x_ref, sh_ref, sc_ref, gt_ref, ng_ref, win_ref, wout_ref, *rest, final, n_main_tiles):
    o_ref = rest[-1]
    x = x_ref[0] if n_main_tiles is None else _tail_rows(x_ref, rest[0], n_main_tiles)
    h = (_rms(x, ng_ref[...]) * (1.0 + sc_ref[...]) + sh_ref[...]).astype(BF16)
    gu = _dot(h, win_ref[...])
    a = (_silu(gu[:, :D_FF]) * gu[:, D_FF:]).astype(BF16)
    xn = x + (0.5 * gt_ref[...]) * _dot(a, wout_ref[...])
    if final:
        xn = _rms(xn, rest[0][...])
    o_ref[0] = xn


def _tail_specs(tm, width, n_main_tiles):
    return (pl.BlockSpec((1, tm, width), lambda b, t: (b, jnp.minimum(t, n_main_tiles - 1), 0)),
            pl.BlockSpec((1, tm, width), lambda b, t: (b, jnp.maximum(t - n_main_tiles, 0), 0)))


def _ffn(x, mod, layer, sub, row_of, norm_g, w_in, w_out, n_rows, final_g=None, tail=None):
    bsz = x.shape[0]
    tm = TOKEN_TILE
    final = final_g is not None
    assert not (final and tail is not None)
    n_main_tiles = None if tail is None else x.shape[1] // tm
    x_spec = pl.BlockSpec((1, tm, D_MODEL), lambda b, t: (b, t, 0))
    if tail is not None:
        x_spec, tail_spec = _tail_specs(tm, D_MODEL, n_main_tiles)
    in_specs = [
        x_spec,
        _mod_spec(layer, sub * 3 + 0, row_of),
        _mod_spec(layer, sub * 3 + 1, row_of),
        _mod_spec(layer, sub * 3 + 2, row_of),
        _const_spec((1, D_MODEL)),
        pl.BlockSpec((D_MODEL, 2 * D_FF), lambda b, t: (0, 0), pipeline_mode=pl.Buffered(1)),
        pl.BlockSpec((D_FF, D_MODEL), lambda b, t: (0, 0), pipeline_mode=pl.Buffered(1)),
    ]
    args = [x, mod, mod, mod, norm_g.reshape(1, D_MODEL), w_in, w_out]
    if final:
        in_specs.append(_const_spec((1, D_MODEL)))
        args.append(final_g.reshape(1, D_MODEL))
    if tail is not None:
        in_specs.append(tail_spec)
        args.append(tail)
    return pl.pallas_call(
        functools.partial(_ffn_kernel, final=final, n_main_tiles=n_main_tiles),
        grid=(bsz, n_rows // tm),
        in_specs=in_specs,
        out_specs=pl.BlockSpec((1, tm, D_MODEL), lambda b, t: (b, t, 0)),
        out_shape=jax.ShapeDtypeStruct((bsz, n_rows, D_MODEL), F32),
        compiler_params=_cparams(("parallel", "parallel")),
        name="ffn_final" if final else "ffn",
    )(*args)


def _rope(x, cos, sin, first_half):
    w = x.shape[-1]
    swapped = jnp.where(first_half, pltpu.roll(x, w - 16, 1), pltpu.roll(x, 16, 1))
    return x * cos + swapped * sin


def _head_rms(x, g, low_head):
    x2 = x * x
    lo = jnp.sum(jnp.where(low_head, x2, 0.0), axis=-1, keepdims=True)
    hi = jnp.sum(jnp.where(low_head, 0.0, x2), axis=-1, keepdims=True)
    ms = jnp.where(low_head, lo, hi) * (1.0 / HEAD_DIM)
    return x * lax.rsqrt(ms + EPS) * g


def _store_padded(ref, x, low_head):
    sw = pltpu.roll(x, HEAD_DIM, 1)
    zero = jnp.zeros_like(x)
    ref[0, 0, 0] = jnp.where(low_head, x, zero).astype(ref.dtype)
    ref[0, 0, 1] = jnp.where(low_head, zero, sw).astype(ref.dtype)
    ref[0, 1, 0] = jnp.where(low_head, sw, zero).astype(ref.dtype)
    ref[0, 1, 1] = jnp.where(low_head, zero, x).astype(ref.dtype)


def _proj_kernel(shift_ref, x_ref, sh_ref, sc_ref, ng_ref, w_ref, cos_ref, sin_ref, qg_ref, kg_ref,
                 qw_ref, qt_ref, kw_ref, vw_ref, ka_ref, vt_ref, u_ref, xa_ref):
    x = x_ref[0]
    tm = x.shape[0]
    h = (_rms(x, ng_ref[...]) * (1.0 + sc_ref[...]) + sh_ref[...]).astype(BF16)
    p = _dot(h, w_ref[...])
    lane = lax.broadcasted_iota(jnp.int32, (tm, 2 * HEAD_DIM), 1)
    low_head = lane < HEAD_DIM
    first_half = jnp.bitwise_and(lane, 31) < 16
    cos = cos_ref[...]
    sin = sin_ref[...]
    rope = lambda z: _rope(z, cos, sin, first_half)
    k_win = rope(p[:, 0:128])
    v_win = p[:, 128:256]
    u_ref[0] = p[:, 256:512]
    k_glb = rope(_head_rms(p[:, 512:640], kg_ref[...], low_head))
    v_glb = p[:, 640:768]
    zero = jnp.zeros_like(k_glb)
    ka_ref[0, 0] = jnp.where(low_head, k_glb, jnp.where(lane == HEAD_DIM, 1.0, zero)).astype(BF16)
    ka_ref[0, 1] = jnp.where(low_head, jnp.where(lane == 0, 1.0, zero), k_glb).astype(BF16)
    first_row = lax.broadcasted_iota(jnp.int32, (HEAD_DIM, tm), 0) == 0
    shift_rows = jnp.where(first_row, shift_ref[0], 0.0)
    for c in range(2):
        qw = rope(p[:, 768 + 128 * c:896 + 128 * c]) * QK_SCALE
        qw_ref[0, :, 128 * c:128 * (c + 1)] = qw.astype(BF16)
        qd = rope(_head_rms(p[:, 1024 + 128 * c:1152 + 128 * c], qg_ref[...], low_head)) * (QK_SCALE * LOG2E)
        q_t = qd.T
        for g in range(2):
            own = q_t[HEAD_DIM * g:HEAD_DIM * (g + 1)]
            pair = [own, shift_rows] if c == 0 else [shift_rows, own]
            qt_ref[0, 2 * c + g] = jnp.concatenate(pair, axis=0).astype(BF16)
    v_t = v_glb.T
    ones_rows = jnp.where(lax.broadcasted_iota(jnp.int32, (V_ROWS - HEAD_DIM, tm), 0) == 0, 1.0, 0.0)
    for j in range(2):
        vt_ref[0, j] = jnp.concatenate([v_t[HEAD_DIM * j:HEAD_DIM * (j + 1)], ones_rows], axis=0).astype(BF16)
    xa_ref[0] = p[:, 1280:1536]
    _store_padded(kw_ref, k_win, low_head)
    _store_padded(vw_ref, v_win, low_head)


def _proj(x, mod, layer, row_of, norm_g, w, cos, sin, q_g, k_g, score_bound):
    bsz, t_all, _ = x.shape
    tm = TOKEN_TILE
    tok = lambda width: pl.BlockSpec((1, tm, width), lambda b, t: (b, t, 0))
    pad = pl.BlockSpec((1, 2, 2, tm, 128), lambda b, t: (b, 0, 0, t, 0))
    tab = pl.BlockSpec((tm, 128), lambda b, t: (t, 0))
    pad_shape = jax.ShapeDtypeStruct((bsz, 2, 2, t_all, 128), BF16)
    return pl.pallas_call(
        _proj_kernel,
        grid=(bsz, t_all // tm),
        in_specs=[
            pl.BlockSpec(memory_space=pltpu.SMEM),
            tok(D_MODEL),
            _mod_spec(layer, 3, row_of),
            _mod_spec(layer, 4, row_of),
            _const_spec((1, D_MODEL)),
            pl.BlockSpec((D_MODEL, PROJ_W), lambda b, t: (0, 0), pipeline_mode=pl.Buffered(1)),
            tab, tab,
            _const_spec((1, 128)), _const_spec((1, 128)),
        ],
        out_specs=[
            tok(256),
            pl.BlockSpec((1, 4, 128, tm), lambda b, t: (b, 0, 0, t)),
            pad, pad,
            pl.BlockSpec((1, 2, tm, 128), lambda b, t: (b, 0, t, 0)),
            pl.BlockSpec((1, 2, V_ROWS, tm), lambda b, t: (b, 0, 0, t)),
            tok(256), tok(256),
        ],
        out_shape=[
            jax.ShapeDtypeStruct((bsz, t_all, 256), BF16),
            jax.ShapeDtypeStruct((bsz, 4, 128, t_all), BF16),
            pad_shape, pad_shape,
            jax.ShapeDtypeStruct((bsz, 2, t_all, 128), BF16),
            jax.ShapeDtypeStruct((bsz, 2, V_ROWS, t_all), BF16),
            jax.ShapeDtypeStruct((bsz, t_all, 256), F32),
            jax.ShapeDtypeStruct((bsz, t_all, 256), F32),
        ],
        compiler_params=_cparams(("parallel", "parallel")),
        name="proj",
    )((-score_bound * LOG2E).reshape(1), x, mod, mod, norm_g.reshape(1, D_MODEL), w, cos, sin,
      jnp.tile(q_g, 2).reshape(1, 128), jnp.tile(k_g, 2).reshape(1, 128))


def _win_kernel(sink_ref, q_ref, kp_ref, kc_ref, kn_ref, kx_ref, vp_ref, vc_ref, vn_ref, vx_ref,
                o_ref, *, n_lat_blocks):
    i = pl.program_id(1)
    tile = WIN_BLOCKS * BLOCK
    band = tile + 2 * BLOCK
    r = lax.broadcasted_iota(jnp.int32, (tile, band), 0)
    c = lax.broadcasted_iota(jnp.int32, (tile, band), 1)
    blk0 = i * WIN_BLOCKS
    lo = jnp.where(blk0 >= 1, 0, BLOCK)
    hi = jnp.where(blk0 < n_lat_blocks, jnp.minimum((n_lat_blocks - blk0 + 1) * BLOCK, band), 0)
    in_window = (c - r).astype(jnp.uint32) <= 2 * BLOCK
    in_range = (c - lo).astype(jnp.uint32) < (hi - lo).astype(jnp.uint32)
    bias = jnp.where(in_window, jnp.where(in_range, 0.0, NEG), NEG)
    for j in range(2):
        q = q_ref[0, :, 128 * j:128 * (j + 1)]
        acc = jnp.zeros((tile, 128), F32)
        for g in range(2):
            k_band = jnp.concatenate([kp_ref[0, j, g], kc_ref[0, j, g], kn_ref[0, j, g]], axis=0)
            s = jnp.concatenate([_dot_nt(q, k_band) + bias, _dot_nt(q, kx_ref[0, j, g])], axis=1)
            v = jnp.concatenate([vp_ref[0, j, g], vc_ref[0, j, g], vn_ref[0, j, g], vx_ref[0, j, g]], axis=0)
            sink = sink_ref[2 * j + g]
            m = jnp.maximum(s.max(-1, keepdims=True), sink)
            p = jnp.exp(s - m)
            den = p.sum(-1, keepdims=True) + jnp.exp(sink - m)
            acc = acc + _dot(p.astype(BF16), v) / den
        o_ref[0, :, 128 * j:128 * (j + 1)] = acc.astype(o_ref.dtype)


def _window_attention(q, k_pad, v_pad, sink, n_lat, n_ctx, with_ctx):
    bsz, t_all, _ = q.shape
    n_lat_blocks = n_lat // BLOCK
    n_blocks = t_all // BLOCK
    n_q_blocks = n_blocks if with_ctx else n_lat_blocks
    assert n_q_blocks % WIN_BLOCKS == 0 and n_lat_blocks % WIN_BLOCKS == 0
    tile = WIN_BLOCKS * BLOCK
    edge = lambda off: pl.BlockSpec(
        (1, 2, 2, BLOCK, 128), lambda b, i: (b, 0, 0, jnp.clip(i * WIN_BLOCKS + off, 0, n_blocks - 1), 0))
    cur = pl.BlockSpec((1, 2, 2, tile, 128), lambda b, i: (b, 0, 0, i, 0))
    ctx = pl.BlockSpec((1, 2, 2, n_ctx, 128), lambda b, i: (b, 0, 0, n_lat // n_ctx, 0))
    return pl.pallas_call(
        functools.partial(_win_kernel, n_lat_blocks=n_lat_blocks),
        grid=(bsz, n_q_blocks // WIN_BLOCKS),
        in_specs=[
            pl.BlockSpec(memory_space=pltpu.SMEM),
            pl.BlockSpec((1, tile, 256), lambda b, i: (b, i, 0)),
            edge(-1), cur, edge(WIN_BLOCKS), ctx,
            edge(-1), cur, edge(WIN_BLOCKS), ctx,
        ],
        out_specs=pl.BlockSpec((1, tile, 256), lambda b, i: (b, i, 0)),
        out_shape=jax.ShapeDtypeStruct((bsz, n_q_blocks * BLOCK, 256), BF16),
        compiler_params=_cparams(("parallel", "parallel")),
        name="window_attention",
    )(sink, q, k_pad, k_pad, k_pad, k_pad, v_pad, v_pad, v_pad, v_pad)


def _glb_finish(acc_ref, o_ref):
    o_t = jnp.concatenate([acc_ref[g, 0:HEAD_DIM] / acc_ref[g, HEAD_DIM:HEAD_DIM + 1] for g in range(2)],
                          axis=0)
    o_ref[0] = o_t.T.astype(o_ref.dtype)


def _glb_bounded_kernel(qt_ref, k_ref, vt_ref, o_ref, acc_ref):
    ki = pl.program_id(3)

    @pl.when(ki == 0)
    def _():
        acc_ref[...] = jnp.zeros_like(acc_ref)

    for g in range(2):
        p = jnp.exp2(_dot(k_ref[0, 0], qt_ref[0, g])).astype(BF16)
        acc_ref[g] += _dot(vt_ref[0, 0], p)

    @pl.when(ki == pl.num_programs(3) - 1)
    def _():
        _glb_finish(acc_ref, o_ref)


def _glb_online_kernel(qt_ref, k_ref, vt_ref, o_ref, acc_ref, m_ref):
    ki = pl.program_id(3)

    @pl.when(ki == 0)
    def _():
        acc_ref[...] = jnp.zeros_like(acc_ref)
        m_ref[...] = jnp.full_like(m_ref, NEG)

    for g in range(2):
        s = _dot(k_ref[0, 0], qt_ref[0, g])
        m_prev = m_ref[g, 0:1]
        m_new = jnp.maximum(m_prev, s.max(axis=0, keepdims=True))
        p = jnp.exp2(s - m_new).astype(BF16)
        acc_ref[g] = jnp.exp2(m_prev - m_new) * acc_ref[g] + _dot(vt_ref[0, 0], p)
        m_ref[g] = jnp.broadcast_to(m_new, m_ref.shape[1:])

    @pl.when(ki == pl.num_programs(3) - 1)
    def _():
        _glb_finish(acc_ref, o_ref)


def _global_attention(qt, k_aug, vt_aug, q_row0, n_q, k_row0, n_k, tq, tk, bounded):
    bsz = qt.shape[0]
    q0, k0 = q_row0 // tq, k_row0 // tk
    scratch = [pltpu.VMEM((2, V_ROWS, tq), F32)]
    if not bounded:
        scratch.append(pltpu.VMEM((2, 8, tq), F32))
    return pl.pallas_call(
        _glb_bounded_kernel if bounded else _glb_online_kernel,
        grid=(bsz, 2, n_q // tq, n_k // tk),
        in_specs=[
            pl.BlockSpec((1, 2, 128, tq), lambda b, j, qi, ki: (b, j, 0, q0 + qi)),
            pl.BlockSpec((1, 1, tk, 128), lambda b, j, qi, ki: (b, j, k0 + ki, 0)),
            pl.BlockSpec((1, 1, V_ROWS, tk), lambda b, j, qi, ki: (b, j, 0, k0 + ki)),
        ],
        out_specs=pl.BlockSpec((1, tq, 128), lambda b, j, qi, ki: (b, qi, j)),
        out_shape=jax.ShapeDtypeStruct((bsz, n_q, 256), BF16),
        scratch_shapes=scratch,
        compiler_params=_cparams(("parallel", "parallel", "parallel", "arbitrary")),
        name="global_attention" if bounded else "global_attention_online",
    )(qt, k_aug, vt_aug)


def _s5_tile(u, bmat_ref, cmat_ref, a_ref, pc_ref, carry_ref, reverse):
    rows = u.shape[0]
    n_groups = rows // SCAN_ROWS
    w = STATE_CHUNK
    ub = u.astype(BF16)
    order = range(n_groups - 1, -1, -1) if reverse else range(n_groups)
    last = 0 if reverse else SCAN_ROWS - 1
    y = None
    for c in range(N_STATE // w):
        bu = _dot(ub, bmat_ref[0, c])
        pr, pi = pc_ref[0, c, :, :w], pc_ref[0, c, :, w:]
        cr, ci = carry_ref[c, :, :w], carry_ref[c, :, w:]
        states = [None] * n_groups
        for g in order:
            vr = bu[SCAN_ROWS * g:SCAN_ROWS * (g + 1), :w]
            vi = bu[SCAN_ROWS * g:SCAN_ROWS * (g + 1), w:]
            for s, k in enumerate((1, 2, 4)):
                ar, ai = a_ref[0, s, c, :, :w], a_ref[0, s, c, :, w:]
                shift = SCAN_ROWS - k if reverse else k
                sr = pltpu.roll(vr, shift, 0)
                si = pltpu.roll(vi, shift, 0)
                vr, vi = vr + ar * sr - ai * si, vi + ar * si + ai * sr
            vr, vi = vr + pr * cr - pi * ci, vi + pr * ci + pi * cr
            cr = jnp.broadcast_to(vr[last:last + 1], (SCAN_ROWS, w))
            ci = jnp.broadcast_to(vi[last:last + 1], (SCAN_ROWS, w))
            states[g] = jnp.concatenate([vr, vi], axis=1)
        carry_ref[c, :, :w] = cr
        carry_ref[c, :, w:] = ci
        part = _dot(jnp.concatenate(states, axis=0).astype(BF16), cmat_ref[0, c])
        y = part if y is None else y + part
    return y


def _s5_kernel(u_ref, bmat_ref, cmat_ref, a_ref, pc_ref, y_ref, carry_ref):
    direction = pl.program_id(0)
    step = pl.program_id(2)

    @pl.when(step == 0)
    def _():
        carry_ref[...] = jnp.zeros_like(carry_ref)

    for reverse in (False, True):
        @pl.when(direction == int(reverse))
        def _(reverse=reverse):
            y_ref[0, 0] = _s5_tile(u_ref[0], bmat_ref, cmat_ref, a_ref, pc_ref, carry_ref, reverse)


def _s5_tables(a_re, a_im, log_dt, b_re, b_im, c_re, c_im):
    dt = jnp.exp(log_dt)[..., None]
    re_dt, im_dt = a_re * dt, a_im * dt
    lb_re, lb_im = jnp.exp(re_dt) * jnp.cos(im_dt), jnp.exp(re_dt) * jnp.sin(im_dt)
    den = a_re * a_re + a_im * a_im
    f_re = ((lb_re - 1.0) * a_re + lb_im * a_im) / den
    f_im = (lb_im * a_re - (lb_re - 1.0) * a_im) / den
    bb_re = f_re[..., None] * b_re - f_im[..., None] * b_im
    bb_im = f_re[..., None] * b_im + f_im[..., None] * b_re
    eye = jnp.eye(SSM_GROUPS, dtype=F32)
    in_w, st_w, n_chunks = SSM_GROUPS * SSM_GROUP, N_STATE, N_STATE // STATE_CHUNK
    b_blk = lambda z: jnp.einsum('dgph,gk->dghkp', z, eye).reshape(2, in_w, n_chunks, STATE_CHUNK)
    bmat = jnp.concatenate([b_blk(bb_re), b_blk(bb_im)], axis=-1).transpose(0, 2, 1, 3).astype(BF16)
    c_blk = lambda z: jnp.einsum('dghp,gk->dgpkh', z, eye).reshape(2, n_chunks, STATE_CHUNK, in_w)
    cmat = jnp.concatenate([c_blk(c_re), -c_blk(c_im)], axis=2).astype(BF16)
    steps = jnp.arange(1, SCAN_ROWS + 1, dtype=F32)[None, :, None, None]
    mag = jnp.exp(re_dt[:, None] * steps)
    ang = im_dt[:, None] * steps
    chunked = lambda z: z.reshape(2, SCAN_ROWS, n_chunks, STATE_CHUNK)
    powers = jnp.concatenate([chunked(mag * jnp.cos(ang)), chunked(mag * jnp.sin(ang))], axis=-1)
    row = jnp.arange(SCAN_ROWS)[:, None, None]
    pc = jnp.stack([powers[0], powers[1][::-1]]).transpose(0, 2, 1, 3)
    a_tabs = []
    for k in (1, 2, 4):
        fwd = jnp.where(row >= k, powers[0, k - 1][None], 0.0)
        bwd = jnp.where(row <= SCAN_ROWS - 1 - k, powers[1, k - 1][None], 0.0)
        a_tabs.append(jnp.stack([fwd, bwd]).transpose(0, 2, 1, 3))
    return bmat, cmat, jnp.stack(a_tabs, axis=1), pc


def _s5_scan(u, tables, n_lat, n_ctx):
    bsz, t_all, width = u.shape
    tm = TOKEN_TILE
    assert n_ctx == tm
    nt = n_lat // tm
    bmat, cmat, a_tab, pc = tables
    n_chunks, lanes = N_STATE // STATE_CHUNK, 2 * STATE_CHUNK
    tile_of = lambda d, s: jnp.where(s == 0, nt, jnp.where(d == 0, s - 1, nt - s))
    return pl.pallas_call(
        _s5_kernel,
        grid=(2, bsz, nt + 1),
        in_specs=[
            pl.BlockSpec((1, tm, width), lambda d, b, s: (b, tile_of(d, s), 0)),
            pl.BlockSpec((1, n_chunks, width, lanes), lambda d, b, s: (d, 0, 0, 0)),
            pl.BlockSpec((1, n_chunks, lanes, width), lambda d, b, s: (d, 0, 0, 0)),
            pl.BlockSpec((1, 3, n_chunks, SCAN_ROWS, lanes), lambda d, b, s: (d, 0, 0, 0, 0)),
            pl.BlockSpec((1, n_chunks, SCAN_ROWS, lanes), lambda d, b, s: (d, 0, 0, 0)),
        ],
        out_specs=pl.BlockSpec((1, 1, tm, width), lambda d, b, s: (d, b, tile_of(d, s), 0)),
        out_shape=jax.ShapeDtypeStruct((2, bsz, t_all, width), F32),
        scratch_shapes=[pltpu.VMEM((n_chunks, SCAN_ROWS, lanes), F32)],
        compiler_params=_cparams(("parallel", "parallel", "arbitrary")),
        name="s5_scan",
    )(u, bmat, cmat, a_tab, pc)


def _gelu_tanh(x):
    c = math.sqrt(2.0 / math.pi)
    return x * (0.5 * (1.0 + jnp.tanh(c * (x + 0.044715 * (x * x * x)))))


def _merge_kernel(x_ref, sh_ref, sc_ref, gt_ref, ng_ref, xa_ref, xap_ref, xan_ref, yb_ref, yf_ref, yr_ref,
                  u_ref, yd_ref, ydt_ref, wg_ref, pw_ref, ps_ref, dsk_ref, glu_ref, wb_ref, wo_ref, o_ref, ext_ref,
                  *, n_lat_tiles, n_lat, n_ctx):
    t = pl.program_id(1)
    x = x_ref[0]
    tm = x.shape[0]
    bw = BRANCH_W
    h = (_rms(x, ng_ref[...]) * (1.0 + sc_ref[...]) + sh_ref[...]).astype(BF16)
    gates = [jax.nn.sigmoid(_dot(h, wg_ref[:, PROJ_W + D_MODEL * k:PROJ_W + D_MODEL * (k + 1)]))
             for k in range(N_BRANCH)]
    is_ctx = t >= n_lat_tiles
    seg_tile = jnp.where(is_ctx, t - n_lat_tiles, t)
    seg_tiles = jnp.where(is_ctx, n_ctx // tm, n_lat_tiles)
    seg_len = jnp.where(is_ctx, n_ctx, n_lat)
    xa = xa_ref[0]
    halo, span = POOL_HALO, tm + 2 * POOL_HALO
    for k in range(4):
        ext_ref[k, 0:halo, :] = jnp.zeros((halo, bw), F32)
        ext_ref[k, halo + span:, :] = jnp.zeros((halo, bw), F32)
    ext_ref[0, halo:2 * halo, :] = jnp.where(seg_tile > 0, xap_ref[0], 0.0)
    ext_ref[0, 2 * halo:2 * halo + tm, :] = xa
    ext_ref[0, 2 * halo + tm:halo + span, :] = jnp.where(seg_tile + 1 < seg_tiles, xan_ref[0], 0.0)
    wide = lambda k, off: ext_ref[k, halo + off:halo + off + span, :]
    ext_ref[1, halo:halo + span, :] = wide(0, -1) + wide(0, 0)
    ext_ref[2, halo:halo + span, :] = wide(1, -1) + wide(1, 1)
    ext_ref[3, halo:halo + span, :] = wide(2, -2) + wide(2, 2)
    own = lambda k, off: ext_ref[k, 2 * halo + off:2 * halo + off + tm, :]
    sums = [own(1, 0), own(2, 0), own(3, 0), own(3, -4) + own(3, 4)]
    lane = lax.broadcasted_iota(jnp.int32, (tm, bw), 1)
    pos = seg_tile * tm + lax.broadcasted_iota(jnp.int32, (tm, bw), 0)
    group = lane // (bw // len(POOL_WINDOWS))
    half_w = jnp.left_shift(1, group)
    count = jnp.minimum(pos + half_w, seg_len) - jnp.maximum(pos - half_w, 0)
    window_sum = jnp.where(group == 0, sums[0], jnp.where(group == 1, sums[1],
                                                           jnp.where(group == 2, sums[2], sums[3])))
    diff = window_sum / count.astype(F32) - xa
    y_a = _dot(diff.astype(BF16), pw_ref[...]) * ps_ref[...]
    y_s = yf_ref[0, 0] + yr_ref[0, 0] + dsk_ref[...] * u_ref[0]
    z = _dot(_gelu_tanh(y_s).astype(BF16), glu_ref[...])
    y_c = z[:, :bw] * jax.nn.sigmoid(z[:, bw:])
    branches = (y_a.astype(BF16), yb_ref[0], y_c.astype(BF16), _tail_rows(yd_ref, ydt_ref, n_lat_tiles))
    total = None
    for gate, y_k, k in zip(gates, branches, range(N_BRANCH)):
        term = gate * _dot(y_k, wb_ref[k])
        total = term if total is None else total + term
    o_ref[0] = x + gt_ref[...] * _dot(total.astype(BF16), wo_ref[...])


def _merge(x, mod, layer, row_of, norm_g, xa, y_b, y_s5, u, y_d, y_d_tail, w_gate, pool_bd, pool_scale, d_skip,
           glu_w, branch_w, out_w, n_rows, n_lat, n_ctx):
    bsz = x.shape[0]
    tm = TOKEN_TILE
    assert n_ctx % tm == 0 or n_rows == n_lat
    halo_blocks = xa.shape[1] // POOL_HALO
    per_tile = tm // POOL_HALO
    tok = lambda width: pl.BlockSpec((1, tm, width), lambda b, t: (b, t, 0))
    single = dict(pipeline_mode=pl.Buffered(1))
    return pl.pallas_call(
        functools.partial(_merge_kernel, n_lat_tiles=n_lat // tm, n_lat=n_lat, n_ctx=n_ctx),
        grid=(bsz, n_rows // tm),
        in_specs=[
            tok(D_MODEL),
            _mod_spec(layer, 3, row_of), _mod_spec(layer, 4, row_of), _mod_spec(layer, 5, row_of),
            _const_spec((1, D_MODEL)),
            tok(256),
            pl.BlockSpec((1, POOL_HALO, 256), lambda b, t: (b, jnp.maximum(t * per_tile - 1, 0), 0)),
            pl.BlockSpec((1, POOL_HALO, 256),
                         lambda b, t: (b, jnp.minimum((t + 1) * per_tile, halo_blocks - 1), 0)),
            tok(256),
            pl.BlockSpec((1, 1, tm, 256), lambda b, t: (0, b, t, 0)),
            pl.BlockSpec((1, 1, tm, 256), lambda b, t: (1, b, t, 0)),
            tok(256), *_tail_specs(tm, 256, n_lat // tm),
            pl.BlockSpec((D_MODEL, PROJ_W + N_BRANCH * D_MODEL), lambda b, t: (0, 0), **single),
            _const_spec((256, 256)), _const_spec((1, 256)), _const_spec((1, 256)),
            _const_spec((256, 512)),
            pl.BlockSpec((N_BRANCH, 256, D_MODEL), lambda b, t: (0, 0, 0), **single),
            pl.BlockSpec((D_MODEL, D_MODEL), lambda b, t: (0, 0), **single),
        ],
        out_specs=tok(D_MODEL),
        out_shape=jax.ShapeDtypeStruct((bsz, n_rows, D_MODEL), F32),
        scratch_shapes=[pltpu.VMEM((4, tm + 4 * POOL_HALO, 256), F32)],
        compiler_params=_cparams(("parallel", "parallel")),
        name="merge",
    )(x, mod, mod, mod, norm_g.reshape(1, D_MODEL), xa, xa, xa, y_b, y_s5, y_s5, u, y_d, y_d_tail,
      w_gate, pool_bd, pool_scale.reshape(1, 256), d_skip.reshape(1, 256), glu_w, branch_w, out_w)


def _rope_tables(n_lat, n_ctx):
    rows = n_lat // GRID_W
    n_freq = HEAD_DIM // 4
    row = jnp.repeat(jnp.arange(rows), GRID_W)
    col = jnp.tile(jnp.arange(GRID_W), rows)
    inv = ROPE_THETA ** (-jnp.arange(n_freq, dtype=F32) / n_freq)
    ang = jnp.stack([row, col], axis=-1).astype(F32)[..., None] * inv
    cos = jnp.broadcast_to(jnp.cos(ang)[:, :, None, :], (n_lat, 2, 2, n_freq)).reshape(n_lat, HEAD_DIM)
    sign = jnp.array([-1.0, 1.0], F32)[None, None, :, None]
    sin = (jnp.sin(ang)[:, :, None, :] * sign).reshape(n_lat, HEAD_DIM)
    cos = jnp.concatenate([cos, jnp.ones((n_ctx, HEAD_DIM), F32)], axis=0)
    sin = jnp.concatenate([sin, jnp.zeros((n_ctx, HEAD_DIM), F32)], axis=0)
    return jnp.tile(cos, (1, 2)), jnp.tile(sin, (1, 2))


def _largest_divisor(n, candidates):
    for c in candidates:
        if n % c == 0:
            return c
    raise ValueError(f"no tile in {candidates} divides {n}")


def _pool_block_diag(pool_w):
    groups, width, _ = pool_w.shape
    eye = jnp.eye(groups, dtype=pool_w.dtype)
    return jnp.einsum('gcd,gk->gckd', pool_w, eye).reshape(groups * width, groups * width)


def kernel(x, c, ctx, c_ctx, w_mod, b_mod, norm_g, ffn_in, ffn_out, w_in, win_sink, qk_norm, pool_w, pool_scale, ssm_a_re, ssm_a_im, ssm_log_dt, ssm_b_re, ssm_b_im, ssm_c_re, ssm_c_im, ssm_d, glu_w, branch_w, out_w, final_g):
    bsz, n_lat, d = x.shape
    n_ctx = ctx.shape[1]
    depth = w_mod.shape[0]
    t_all = n_lat + n_ctx
    assert d == D_MODEL and bsz < MOD_ROWS and n_lat % TOKEN_TILE == 0 and n_ctx == TOKEN_TILE
    assert n_lat % GRID_W == 0

    cond_rows = jnp.zeros((MOD_ROWS, d), F32).at[:bsz].set(c).at[bsz].set(c_ctx)
    mod = _modulation(cond_rows, w_mod, b_mod)
    n_lat_tiles = n_lat // TOKEN_TILE
    row_of = lambda b, t: jnp.where(t >= n_lat_tiles, bsz, b)
    lat_row = lambda b, t: b

    cos, sin = _rope_tables(n_lat, n_ctx)
    tq = _largest_divisor(n_lat, (2048, 1024, 512, 256))
    tk = _largest_divisor(t_all, (1280, 1024, 768, 512, 256))

    xs = x
    for l in range(depth):
        last = l == depth - 1
        bf = lambda w: w.astype(BF16)
        xs = _ffn(xs, mod, l, 0, row_of, norm_g[l, 0], bf(ffn_in[l, 0]), bf(ffn_out[l, 0]), t_all,
                  tail=ctx if l == 0 else None)

        score_bound = 1.02 * HEAD_DIM ** 0.5 * jnp.max(jnp.abs(qk_norm[l, 0])) * jnp.max(jnp.abs(qk_norm[l, 1]))
        w_in_l = bf(w_in[l])
        qw, qt, kw, vw, ka, vt, u, xa = _proj(xs, mod, l, row_of, norm_g[l, 1], w_in_l,
                                              cos, sin, qk_norm[l, 0], qk_norm[l, 1], score_bound)
        y_b = _window_attention(qw, kw, vw, win_sink[l], n_lat, n_ctx, not last)

        def global_attention(q_row0, n_q, k_row0, n_k, tq_, tk_):
            run = lambda bounded: lambda: _global_attention(qt, ka, vt, q_row0, n_q, k_row0, n_k, tq_, tk_,
                                                            bounded)
            return lax.cond(score_bound <= MAX_SCORE_BOUND, run(True), run(False))

        y_d = global_attention(0, n_lat, 0, t_all, tq, tk)
        tables = _s5_tables(ssm_a_re[l], ssm_a_im[l], ssm_log_dt[l], ssm_b_re[l], ssm_b_im[l],
                            ssm_c_re[l], ssm_c_im[l])
        y_s5 = _s5_scan(u, tables, n_lat, n_ctx)
        n_rows, y_d_ctx = n_lat, y_d
        if not last:
            y_d_ctx = global_attention(n_lat, n_ctx, n_lat, n_ctx, n_ctx, n_ctx)
            n_rows = t_all
        xs = _merge(xs, mod, l, row_of, norm_g[l, 1], xa, y_b, y_s5, u, y_d, y_d_ctx, w_in_l,
                    bf(_pool_block_diag(pool_w[l])), pool_scale[l], ssm_d[l], bf(glu_w[l]), bf(branch_w[l]),
                    bf(out_w[l]), n_rows, n_lat, n_ctx)
        xs = _ffn(xs, mod, l, 2, row_of if not last else lat_row, norm_g[l, 2], bf(ffn_in[l, 1]),
                  bf(ffn_out[l, 1]), n_rows, final_g=final_g if last else None)
    return xs
```

```python
import functools
import math

import jax
import jax.numpy as jnp
import numpy as np
from jax import lax
from jax.experimental import pallas as pl
from jax.experimental.pallas import tpu as pltpu

F32 = jnp.float32
BF16 = jnp.bfloat16

D_MODEL = 1024
D_FF = 2816
N_SUB = 3
BRANCH_W = 256
HEAD_DIM = 64
GRID_W = 64
BLOCK = 128
EPS = 1e-6
ROPE_THETA = 10000.0
N_BRANCH = 4
POOL_WINDOWS = (2, 4, 8, 16)
SSM_GROUPS = 16
SSM_GROUP = 16
SSM_STATE = 64
N_STATE = SSM_GROUPS * SSM_STATE
PROJ_W = 6 * BRANCH_W
MOD_ROWS = 8
POOL_HALO = 8
SCAN_ROWS = 8
STATE_CHUNK = 128
WIN_BLOCKS = 2
NEG = -1e30
QK_SCALE = HEAD_DIM ** -0.5
LOG2E = math.log2(math.e)
V_ROWS = HEAD_DIM + 16
MAX_SCORE_BOUND = 40.0
TOKEN_TILE = 256
VMEM_LIMIT = 56 * 1024 * 1024


def _cparams(sem):
    return pltpu.CompilerParams(dimension_semantics=sem, vmem_limit_bytes=VMEM_LIMIT)


def _dot(a, b):
    return jnp.dot(a, b, preferred_element_type=F32)


def _dot_nt(a, b):
    return lax.dot_general(a, b, (((1,), (1,)), ((), ())), preferred_element_type=F32)


def _rms(x, g):
    ms = jnp.mean(x * x, axis=-1, keepdims=True)
    return x * lax.rsqrt(ms + EPS) * g


def _silu(x):
    return x * jax.nn.sigmoid(x)


def _const_spec(shape):
    nd = len(shape)
    return pl.BlockSpec(shape, lambda *_: (0,) * nd)


def _mod_kernel(s_ref, w_ref, b_ref, o_ref):
    s = _silu(s_ref[...])
    w = w_ref[0]
    s_hi = s.astype(BF16)
    s_lo = (s - s_hi.astype(F32)).astype(BF16)
    w_hi = w.astype(BF16)
    w_lo = (w - w_hi.astype(F32)).astype(BF16)
    o_ref[0] = _dot(s_hi, w_hi) + _dot(s_hi, w_lo) + _dot(s_lo, w_hi) + b_ref[0]


def _modulation(rows, w_mod, b_mod):
    depth, d, width = w_mod.shape
    tn = 1024
    out = pl.pallas_call(
        _mod_kernel,
        grid=(depth, width // tn),
        in_specs=[
            pl.BlockSpec((MOD_ROWS, d), lambda l, j: (0, 0)),
            pl.BlockSpec((1, d, tn), lambda l, j: (l, 0, j)),
            pl.BlockSpec((1, 1, tn), lambda l, j: (l, 0, j)),
        ],
        out_specs=pl.BlockSpec((1, MOD_ROWS, tn), lambda l, j: (l, 0, j)),
        out_shape=jax.ShapeDtypeStruct((depth, MOD_ROWS, width), F32),
        compiler_params=_cparams(("parallel", "parallel")),
        name="modulation",
    )(rows, w_mod, b_mod.reshape(depth, 1, width))
    return out.reshape(depth, MOD_ROWS, N_SUB * 3, 1, d).transpose(0, 2, 1, 3, 4)


def _mod_spec(layer, vec, row_of):
    return pl.BlockSpec((None, None, None, 1, D_MODEL), lambda b, t: (layer, vec, row_of(b, t), 0, 0))


def _tail_rows(main_ref, tail_ref, n_main_tiles):
    return jnp.where(pl.program_id(1) >= n_main_tiles, tail_ref[0], main_ref[0])


def _ffn_kernel(x_ref, sh_ref, sc_ref, gt_ref, ng_ref, win_ref, wout_ref, *rest, final, n_main_tiles):
    o_ref = rest[-1]
    x = x_ref[0] if n_main_tiles is None else _tail_rows(x_ref, rest[0], n_main_tiles)
    h = (_rms(x, ng_ref[...]) * (1.0 + sc_ref[...]) + sh_ref[...]).astype(BF16)
    gu = _dot(h, win_ref[...])
    a = (_silu(gu[:, :D_FF]) * gu[:, D_FF:]).astype(BF16)
    xn = x + (0.5 * gt_ref[...]) * _dot(a, wout_ref[...])
    if final:
        xn = _rms(xn, rest[0][...])
    o_ref[0] = xn


def _tail_specs(tm, width, n_main_tiles):
    return (pl.BlockSpec((1, tm, width), lambda b, t: (b, jnp.minimum(t, n_main_tiles - 1), 0)),
            pl.BlockSpec((1, tm, width), lambda b, t: (b, jnp.maximum(t - n_main_tiles, 0), 0)))


def _ffn(x, mod, layer, sub, row_of, norm_g, w_in, w_out, n_rows, final_g=None, tail=None):
    bsz = x.shape[0]
    tm = TOKEN_TILE
    final = final_g is not None
    assert not (final and tail is not None)
    n_main_tiles = None if tail is None else x.shape[1] // tm
    x_spec = pl.BlockSpec((1, tm, D_MODEL), lambda b, t: (b, t, 0))
    if tail is not None:
        x_spec, tail_spec = _tail_specs(tm, D_MODEL, n_main_tiles)
    in_specs = [
        x_spec,
        _mod_spec(layer, sub * 3 + 0, row_of),
        _mod_spec(layer, sub * 3 + 1, row_of),
        _mod_spec(layer, sub * 3 + 2, row_of),
        _const_spec((1, D_MODEL)),
        pl.BlockSpec((D_MODEL, 2 * D_FF), lambda b, t: (0, 0), pipeline_mode=pl.Buffered(1)),
        pl.BlockSpec((D_FF, D_MODEL), lambda b, t: (0, 0), pipeline_mode=pl.Buffered(1)),
    ]
    args = [x, mod, mod, mod, norm_g.reshape(1, D_MODEL), w_in, w_out]
    if final:
        in_specs.append(_const_spec((1, D_MODEL)))
        args.append(final_g.reshape(1, D_MODEL))
    if tail is not None:
        in_specs.append(tail_spec)
        args.append(tail)
    return pl.pallas_call(
        functools.partial(_ffn_kernel, final=final, n_main_tiles=n_main_tiles),
        grid=(bsz, n_rows // tm),
        in_specs=in_specs,
        out_specs=pl.BlockSpec((1, tm, D_MODEL), lambda b, t: (b, t, 0)),
        out_shape=jax.ShapeDtypeStruct((bsz, n_rows, D_MODEL), F32),
        compiler_params=_cparams(("parallel", "parallel")),
        name="ffn_final" if final else "ffn",
    )(*args)


def _rope(x, cos, sin, first_half):
    w = x.shape[-1]
    swapped = jnp.where(first_half, pltpu.roll(x, w - 16, 1), pltpu.roll(x, 16, 1))
    return x * cos + swapped * sin


def _head_rms(x, g, low_head):
    x2 = x * x
    lo = jnp.sum(jnp.where(low_head, x2, 0.0), axis=-1, keepdims=True)
    hi = jnp.sum(jnp.where(low_head, 0.0, x2), axis=-1, keepdims=True)
    ms = jnp.where(low_head, lo, hi) * (1.0 / HEAD_DIM)
    return x * lax.rsqrt(ms + EPS) * g


def _store_padded(ref, x, low_head):
    sw = pltpu.roll(x, HEAD_DIM, 1)
    zero = jnp.zeros_like(x)
    ref[0, 0, 0] = jnp.where(low_head, x, zero).astype(ref.dtype)
    ref[0, 0, 1] = jnp.where(low_head, zero, sw).astype(ref.dtype)
    ref[0, 1, 0] = jnp.where(low_head, sw, zero).astype(ref.dtype)
    ref[0, 1, 1] = jnp.where(low_head, zero, x).astype(ref.dtype)


def _proj_kernel(shift_ref, x_ref, sh_ref, sc_ref, ng_ref, w_ref, cos_ref, sin_ref, qg_ref, kg_ref,
                 qw_ref, qt_ref, kw_ref, vw_ref, ka_ref, vt_ref, u_ref, xa_ref):
    x = x_ref[0]
    tm = x.shape[0]
    h = (_rms(x, ng_ref[...]) * (1.0 + sc_ref[...]) + sh_ref[...]).astype(BF16)
    p = _dot(h, w_ref[...])
    lane = lax.broadcasted_iota(jnp.int32, (tm, 2 * HEAD_DIM), 1)
    low_head = lane < HEAD_DIM
    first_half = jnp.bitwise_and(lane, 31) < 16
    cos = cos_ref[...]
    sin = sin_ref[...]
    rope = lambda z: _rope(z, cos, sin, first_half)
    k_win = rope(p[:, 0:128])
    v_win = p[:, 128:256]
    u_ref[0] = p[:, 256:512]
    k_glb = rope(_head_rms(p[:, 512:640], kg_ref[...], low_head))
    v_glb = p[:, 640:768]
    zero = jnp.zeros_like(k_glb)
    ka_ref[0, 0] = jnp.where(low_head, k_glb, jnp.where(lane == HEAD_DIM, 1.0, zero)).astype(BF16)
    ka_ref[0, 1] = jnp.where(low_head, jnp.where(lane == 0, 1.0, zero), k_glb).astype(BF16)
    first_row = lax.broadcasted_iota(jnp.int32, (HEAD_DIM, tm), 0) == 0
    shift_rows = jnp.where(first_row, shift_ref[0], 0.0)
    for c in range(2):
        qw = rope(p[:, 768 + 128 * c:896 + 128 * c]) * QK_SCALE
        qw_ref[0, :, 128 * c:128 * (c + 1)] = qw.astype(BF16)
        qd = rope(_head_rms(p[:, 1024 + 128 * c:1152 + 128 * c], qg_ref[...], low_head)) * (QK_SCALE * LOG2E)
        q_t = qd.T
        for g in range(2):
            own = q_t[HEAD_DIM * g:HEAD_DIM * (g + 1)]
            pair = [own, shift_rows] if c == 0 else [shift_rows, own]
            qt_ref[0, 2 * c + g] = jnp.concatenate(pair, axis=0).astype(BF16)
    v_t = v_glb.T
    ones_rows = jnp.where(lax.broadcasted_iota(jnp.int32, (V_ROWS - HEAD_DIM, tm), 0) == 0, 1.0, 0.0)
    for j in range(2):
        vt_ref[0, j] = jnp.concatenate([v_t[HEAD_DIM * j:HEAD_DIM * (j + 1)], ones_rows], axis=0).astype(BF16)
    xa_ref[0] = p[:, 1280:1536]
    _store_padded(kw_ref, k_win, low_head)
    _store_padded(vw_ref, v_win, low_head)


def _proj(x, mod, layer, row_of, norm_g, w, cos, sin, q_g, k_g, score_bound):
    bsz, t_all, _ = x.shape
    tm = TOKEN_TILE
    tok = lambda width: pl.BlockSpec((1, tm, width), lambda b, t: (b, t, 0))
    pad = pl.BlockSpec((1, 2, 2, tm, 128), lambda b, t: (b, 0, 0, t, 0))
    tab = pl.BlockSpec((tm, 128), lambda b, t: (t, 0))
    pad_shape = jax.ShapeDtypeStruct((bsz, 2, 2, t_all, 128), BF16)
    return pl.pallas_call(
        _proj_kernel,
        grid=(bsz, t_all // tm),
        in_specs=[
            pl.BlockSpec(memory_space=pltpu.SMEM),
            tok(D_MODEL),
            _mod_spec(layer, 3, row_of),
            _mod_spec(layer, 4, row_of),
            _const_spec((1, D_MODEL)),
            pl.BlockSpec((D_MODEL, PROJ_W), lambda b, t: (0, 0), pipeline_mode=pl.Buffered(1)),
            tab, tab,
            _const_spec((1, 128)), _const_spec((1, 128)),
        ],
        out_specs=[
            tok(256),
            pl.BlockSpec((1, 4, 128, tm), lambda b, t: (b, 0, 0, t)),
            pad, pad,
            pl.BlockSpec((1, 2, tm, 128), lambda b, t: (b, 0, t, 0)),
            pl.BlockSpec((1, 2, V_ROWS, tm), lambda b, t: (b, 0, 0, t)),
            tok(256), tok(256),
        ],
        out_shape=[
            jax.ShapeDtypeStruct((bsz, t_all, 256), BF16),
            jax.ShapeDtypeStruct((bsz, 4, 128, t_all), BF16),
            pad_shape, pad_shape,
            jax.ShapeDtypeStruct((bsz, 2, t_all, 128), BF16),
            jax.ShapeDtypeStruct((bsz, 2, V_ROWS, t_all), BF16),
            jax.ShapeDtypeStruct((bsz, t_all, 256), F32),
            jax.ShapeDtypeStruct((bsz, t_all, 256), F32),
        ],
        compiler_params=_cparams(("parallel", "parallel")),
        name="proj",
    )((-score_bound * LOG2E).reshape(1), x, mod, mod, norm_g.reshape(1, D_MODEL), w, cos, sin,
      jnp.tile(q_g, 2).reshape(1, 128), jnp.tile(k_g, 2).reshape(1, 128))


def _win_kernel(sink_ref, q_ref, kp_ref, kc_ref, kn_ref, kx_ref, vp_ref, vc_ref, vn_ref, vx_ref,
                o_ref, *, n_lat_blocks):
    i = pl.program_id(1)
    tile = WIN_BLOCKS * BLOCK
    band = tile + 2 * BLOCK
    r = lax.broadcasted_iota(jnp.int32, (tile, band), 0)
    c = lax.broadcasted_iota(jnp.int32, (tile, band), 1)
    blk0 = i * WIN_BLOCKS
    lo = jnp.where(blk0 >= 1, 0, BLOCK)
    hi = jnp.where(blk0 < n_lat_blocks, jnp.minimum((n_lat_blocks - blk0 + 1) * BLOCK, band), 0)
    in_window = (c - r).astype(jnp.uint32) <= 2 * BLOCK
    in_range = (c - lo).astype(jnp.uint32) < (hi - lo).astype(jnp.uint32)
    bias = jnp.where(in_window, jnp.where(in_range, 0.0, NEG), NEG)
    for j in range(2):
        q = q_ref[0, :, 128 * j:128 * (j + 1)]
        acc = jnp.zeros((tile, 128), F32)
        for g in range(2):
            k_band = jnp.concatenate([kp_ref[0, j, g], kc_ref[0, j, g], kn_ref[0, j, g]], axis=0)
            s = jnp.concatenate([_dot_nt(q, k_band) + bias, _dot_nt(q, kx_ref[0, j, g])], axis=1)
            v = jnp.concatenate([vp_ref[0, j, g], vc_ref[0, j, g], vn_ref[0, j, g], vx_ref[0, j, g]], axis=0)
            sink = sink_ref[2 * j + g]
            m = jnp.maximum(s.max(-1, keepdims=True), sink)
            p = jnp.exp(s - m)
            den = p.sum(-1, keepdims=True) + jnp.exp(sink - m)
            acc = acc + _dot(p.astype(BF16), v) / den
        o_ref[0, :, 128 * j:128 * (j + 1)] = acc.astype(o_ref.dtype)


def _window_attention(q, k_pad, v_pad, sink, n_lat, n_ctx, with_ctx):
    bsz, t_all, _ = q.shape
    n_lat_blocks = n_lat // BLOCK
    n_blocks = t_all // BLOCK
    n_q_blocks = n_blocks if with_ctx else n_lat_blocks
    assert n_q_blocks % WIN_BLOCKS == 0 and n_lat_blocks % WIN_BLOCKS == 0
    tile = WIN_BLOCKS * BLOCK
    edge = lambda off: pl.BlockSpec(
        (1, 2, 2, BLOCK, 128), lambda b, i: (b, 0, 0, jnp.clip(i * WIN_BLOCKS + off, 0, n_blocks - 1), 0))
    cur = pl.BlockSpec((1, 2, 2, tile, 128), lambda b, i: (b, 0, 0, i, 0))
    ctx = pl.BlockSpec((1, 2, 2, n_ctx, 128), lambda b, i: (b, 0, 0, n_lat // n_ctx, 0))
    return pl.pallas_call(
        functools.partial(_win_kernel, n_lat_blocks=n_lat_blocks),
        grid=(bsz, n_q_blocks // WIN_BLOCKS),
        in_specs=[
            pl.BlockSpec(memory_space=pltpu.SMEM),
            pl.BlockSpec((1, tile, 256), lambda b, i: (b, i, 0)),
            edge(-1), cur, edge(WIN_BLOCKS), ctx,
            edge(-1), cur, edge(WIN_BLOCKS), ctx,
        ],
        out_specs=pl.BlockSpec((1, tile, 256), lambda b, i: (b, i, 0)),
        out_shape=jax.ShapeDtypeStruct((bsz, n_q_blocks * BLOCK, 256), BF16),
        compiler_params=_cparams(("parallel", "parallel")),
        name="window_attention",
    )(sink, q, k_pad, k_pad, k_pad, k_pad, v_pad, v_pad, v_pad, v_pad)


def _glb_finish(acc_ref, o_ref):
    o_t = jnp.concatenate([acc_ref[g, 0:HEAD_DIM] / acc_ref[g, HEAD_DIM:HEAD_DIM + 1] for g in range(2)],
                          axis=0)
    o_ref[0] = o_t.T.astype(o_ref.dtype)


def _glb_bounded_kernel(qt_ref, k_ref, vt_ref, o_ref, acc_ref):
    ki = pl.program_id(3)

    @pl.when(ki == 0)
    def _():
        acc_ref[...] = jnp.zeros_like(acc_ref)

    for g in range(2):
        p = jnp.exp2(_dot(k_ref[0, 0], qt_ref[0, g])).astype(BF16)
        acc_ref[g] += _dot(vt_ref[0, 0], p)

    @pl.when(ki == pl.num_programs(3) - 1)
    def _():
        _glb_finish(acc_ref, o_ref)


def _glb_online_kernel(qt_ref, k_ref, vt_ref, o_ref, acc_ref, m_ref):
    ki = pl.program_id(3)

    @pl.when(ki == 0)
    def _():
        acc_ref[...] = jnp.zeros_like(acc_ref)
        m_ref[...] = jnp.full_like(m_ref, NEG)

    for g in range(2):
        s = _dot(k_ref[0, 0], qt_ref[0, g])
        m_prev = m_ref[g, 0:1]
        m_new = jnp.maximum(m_prev, s.max(axis=0, keepdims=True))
        p = jnp.exp2(s - m_new).astype(BF16)
        acc_ref[g] = jnp.exp2(m_prev - m_new) * acc_ref[g] + _dot(vt_ref[0, 0], p)
        m_ref[g] = jnp.broadcast_to(m_new, m_ref.shape[1:])

    @pl.when(ki == pl.num_programs(3) - 1)
    def _():
        _glb_finish(acc_ref, o_ref)


def _global_attention(qt, k_aug, vt_aug, q_row0, n_q, k_row0, n_k, tq, tk, bounded):
    bsz = qt.shape[0]
    q0, k0 = q_row0 // tq, k_row0 // tk
    scratch = [pltpu.VMEM((2, V_ROWS, tq), F32)]
    if not bounded:
        scratch.append(pltpu.VMEM((2, 8, tq), F32))
    return pl.pallas_call(
        _glb_bounded_kernel if bounded else _glb_online_kernel,
        grid=(bsz, 2, n_q // tq, n_k // tk),
        in_specs=[
            pl.BlockSpec((1, 2, 128, tq), lambda b, j, qi, ki: (b, j, 0, q0 + qi)),
            pl.BlockSpec((1, 1, tk, 128), lambda b, j, qi, ki: (b, j, k0 + ki, 0)),
            pl.BlockSpec((1, 1, V_ROWS, tk), lambda b, j, qi, ki: (b, j, 0, k0 + ki)),
        ],
        out_specs=pl.BlockSpec((1, tq, 128), lambda b, j, qi, ki: (b, qi, j)),
        out_shape=jax.ShapeDtypeStruct((bsz, n_q, 256), BF16),
        scratch_shapes=scratch,
        compiler_params=_cparams(("parallel", "parallel", "parallel", "arbitrary")),
        name="global_attention" if bounded else "global_attention_online",
    )(qt, k_aug, vt_aug)


def _s5_tile(u, bmat_ref, cmat_ref, a_ref, pc_ref, carry_ref, reverse):
    rows = u.shape[0]
    n_groups = rows // SCAN_ROWS
    w = STATE_CHUNK
    ub = u.astype(BF16)
    order = range(n_groups - 1, -1, -1) if reverse else range(n_groups)
    last = 0 if reverse else SCAN_ROWS - 1
    y = None
    for c in range(N_STATE // w):
        bu = _dot(ub, bmat_ref[0, c])
        pr, pi = pc_ref[0, c, :, :w], pc_ref[0, c, :, w:]
        cr, ci = carry_ref[c, :, :w], carry_ref[c, :, w:]
        states = [None] * n_groups
        for g in order:
            vr = bu[SCAN_ROWS * g:SCAN_ROWS * (g + 1), :w]
            vi = bu[SCAN_ROWS * g:SCAN_ROWS * (g + 1), w:]
            for s, k in enumerate((1, 2, 4)):
                ar, ai = a_ref[0, s, c, :, :w], a_ref[0, s, c, :, w:]
                shift = SCAN_ROWS - k if reverse else k
                sr = pltpu.roll(vr, shift, 0)
                si = pltpu.roll(vi, shift, 0)
                vr, vi = vr + ar * sr - ai * si, vi + ar * si + ai * sr
            vr, vi = vr + pr * cr - pi * ci, vi + pr * ci + pi * cr
            cr = jnp.broadcast_to(vr[last:last + 1], (SCAN_ROWS, w))
            ci = jnp.broadcast_to(vi[last:last + 1], (SCAN_ROWS, w))
            states[g] = jnp.concatenate([vr, vi], axis=1)
        carry_ref[c, :, :w] = cr
        carry_ref[c, :, w:] = ci
        part = _dot(jnp.concatenate(states, axis=0).astype(BF16), cmat_ref[0, c])
        y = part if y is None else y + part
    return y


def _s5_kernel(u_ref, bmat_ref, cmat_ref, a_ref, pc_ref, y_ref, carry_ref):
    direction = pl.program_id(0)
    step = pl.program_id(2)

    @pl.when(step == 0)
    def _():
        carry_ref[...] = jnp.zeros_like(carry_ref)

    for reverse in (False, True):
        @pl.when(direction == int(reverse))
        def _(reverse=reverse):
            y_ref[0, 0] = _s5_tile(u_ref[0], bmat_ref, cmat_ref, a_ref, pc_ref, carry_ref, reverse)


def _s5_tables(a_re, a_im, log_dt, b_re, b_im, c_re, c_im):
    dt = jnp.exp(log_dt)[..., None]
    re_dt, im_dt = a_re * dt, a_im * dt
    lb_re, lb_im = jnp.exp(re_dt) * jnp.cos(im_dt), jnp.exp(re_dt) * jnp.sin(im_dt)
    den = a_re * a_re + a_im * a_im
    f_re = ((lb_re - 1.0) * a_re + lb_im * a_im) / den
    f_im = (lb_im * a_re - (lb_re - 1.0) * a_im) / den
    bb_re = f_re[..., None] * b_re - f_im[..., None] * b_im
    bb_im = f_re[..., None] * b_im + f_im[..., None] * b_re
    eye = jnp.eye(SSM_GROUPS, dtype=F32)
    in_w, st_w, n_chunks = SSM_GROUPS * SSM_GROUP, N_STATE, N_STATE // STATE_CHUNK
    b_blk = lambda z: jnp.einsum('dgph,gk->dghkp', z, eye).reshape(2, in_w, n_chunks, STATE_CHUNK)
    bmat = jnp.concatenate([b_blk(bb_re), b_blk(bb_im)], axis=-1).transpose(0, 2, 1, 3).astype(BF16)
    c_blk = lambda z: jnp.einsum('dghp,gk->dgpkh', z, eye).reshape(2, n_chunks, STATE_CHUNK, in_w)
    cmat = jnp.concatenate([c_blk(c_re), -c_blk(c_im)], axis=2).astype(BF16)
    steps = jnp.arange(1, SCAN_ROWS + 1, dtype=F32)[None, :, None, None]
    mag = jnp.exp(re_dt[:, None] * steps)
    ang = im_dt[:, None] * steps
    chunked = lambda z: z.reshape(2, SCAN_ROWS, n_chunks, STATE_CHUNK)
    powers = jnp.concatenate([chunked(mag * jnp.cos(ang)), chunked(mag * jnp.sin(ang))], axis=-1)
    row = jnp.arange(SCAN_ROWS)[:, None, None]
    pc = jnp.stack([powers[0], powers[1][::-1]]).transpose(0, 2, 1, 3)
    a_tabs = []
    for k in (1, 2, 4):
        fwd = jnp.where(row >= k, powers[0, k - 1][None], 0.0)
        bwd = jnp.where(row <= SCAN_ROWS - 1 - k, powers[1, k - 1][None], 0.0)
        a_tabs.append(jnp.stack([fwd, bwd]).transpose(0, 2, 1, 3))
    return bmat, cmat, jnp.stack(a_tabs, axis=1), pc


def _s5_scan(u, tables, n_lat, n_ctx):
    bsz, t_all, width = u.shape
    tm = TOKEN_TILE
    assert n_ctx == tm
    nt = n_lat // tm
    bmat, cmat, a_tab, pc = tables
    n_chunks, lanes = N_STATE // STATE_CHUNK, 2 * STATE_CHUNK
    tile_of = lambda d, s: jnp.where(s == 0, nt, jnp.where(d == 0, s - 1, nt - s))
    return pl.pallas_call(
        _s5_kernel,
        grid=(2, bsz, nt + 1),
        in_specs=[
            pl.BlockSpec((1, tm, width), lambda d, b, s: (b, tile_of(d, s), 0)),
            pl.BlockSpec((1, n_chunks, width, lanes), lambda d, b, s: (d, 0, 0, 0)),
            pl.BlockSpec((1, n_chunks, lanes, width), lambda d, b, s: (d, 0, 0, 0)),
            pl.BlockSpec((1, 3, n_chunks, SCAN_ROWS, lanes), lambda d, b, s: (d, 0, 0, 0, 0)),
            pl.BlockSpec((1, n_chunks, SCAN_ROWS, lanes), lambda d, b, s: (d, 0, 0, 0)),
        ],
        out_specs=pl.BlockSpec((1, 1, tm, width), lambda d, b, s: (d, b, tile_of(d, s), 0)),
        out_shape=jax.ShapeDtypeStruct((2, bsz, t_all, width), F32),
        scratch_shapes=[pltpu.VMEM((n_chunks, SCAN_ROWS, lanes), F32)],
        compiler_params=_cparams(("parallel", "parallel", "arbitrary")),
        name="s5_scan",
    )(u, bmat, cmat, a_tab, pc)


def _gelu_tanh(x):
    c = math.sqrt(2.0 / math.pi)
    return x * (0.5 * (1.0 + jnp.tanh(c * (x + 0.044715 * (x * x * x)))))


def _merge_kernel(x_ref, sh_ref, sc_ref, gt_ref, ng_ref, xa_ref, xap_ref, xan_ref, yb_ref, yf_ref, yr_ref,
                  u_ref, yd_ref, ydt_ref, wg_ref, pw_ref, ps_ref, dsk_ref, glu_ref, wb_ref, wo_ref, o_ref, ext_ref,
                  *, n_lat_tiles, n_lat, n_ctx):
    t = pl.program_id(1)
    x = x_ref[0]
    tm = x.shape[0]
    bw = BRANCH_W
    h = (_rms(x, ng_ref[...]) * (1.0 + sc_ref[...]) + sh_ref[...]).astype(BF16)
    gates = [jax.nn.sigmoid(_dot(h, wg_ref[:, PROJ_W + D_MODEL * k:PROJ_W + D_MODEL * (k + 1)]))
             for k in range(N_BRANCH)]
    is_ctx = t >= n_lat_tiles
    seg_tile = jnp.where(is_ctx, t - n_lat_tiles, t)
    seg_tiles = jnp.where(is_ctx, n_ctx // tm, n_lat_tiles)
    seg_len = jnp.where(is_ctx, n_ctx, n_lat)
    xa = xa_ref[0]
    halo, span = POOL_HALO, tm + 2 * POOL_HALO
    for k in range(4):
        ext_ref[k, 0:halo, :] = jnp.zeros((halo, bw), F32)
        ext_ref[k, halo + span:, :] = jnp.zeros((halo, bw), F32)
    ext_ref[0, halo:2 * halo, :] = jnp.where(seg_tile > 0, xap_ref[0], 0.0)
    ext_ref[0, 2 * halo:2 * halo + tm, :] = xa
    ext_ref[0, 2 * halo + tm:halo + span, :] = jnp.where(seg_tile + 1 < seg_tiles, xan_ref[0], 0.0)
    wide = lambda k, off: ext_ref[k, halo + off:halo + off + span, :]
    ext_ref[1, halo:halo + span, :] = wide(0, -1) + wide(0, 0)
    ext_ref[2, halo:halo + span, :] = wide(1, -1) + wide(1, 1)
    ext_ref[3, halo:halo + span, :] = wide(2, -2) + wide(2, 2)
    own = lambda k, off: ext_ref[k, 2 * halo + off:2 * halo + off + tm, :]
    sums = [own(1, 0), own(2, 0), own(3, 0), own(3, -4) + own(3, 4)]
    lane = lax.broadcasted_iota(jnp.int32, (tm, bw), 1)
    pos = seg_tile * tm + lax.broadcasted_iota(jnp.int32, (tm, bw), 0)
    group = lane // (bw // len(POOL_WINDOWS))
    half_w = jnp.left_shift(1, group)
    count = jnp.minimum(pos + half_w, seg_len) - jnp.maximum(pos - half_w, 0)
    window_sum = jnp.where(group == 0, sums[0], jnp.where(group == 1, sums[1],
                                                           jnp.where(group == 2, sums[2], sums[3])))
    diff = window_sum / count.astype(F32) - xa
    y_a = _dot(diff.astype(BF16), pw_ref[...]) * ps_ref[...]
    y_s = yf_ref[0, 0] + yr_ref[0, 0] + dsk_ref[...] * u_ref[0]
    z = _dot(_gelu_tanh(y_s).astype(BF16), glu_ref[...])
    y_c = z[:, :bw] * jax.nn.sigmoid(z[:, bw:])
    branches = (y_a.astype(BF16), yb_ref[0], y_c.astype(BF16), _tail_rows(yd_ref, ydt_ref, n_lat_tiles))
    total = None
    for gate, y_k, k in zip(gates, branches, range(N_BRANCH)):
        term = gate * _dot(y_k, wb_ref[k])
        total = term if total is None else total + term
    o_ref[0] = x + gt_ref[...] * _dot(total.astype(BF16), wo_ref[...])


def _merge(x, mod, layer, row_of, norm_g, xa, y_b, y_s5, u, y_d, y_d_tail, w_gate, pool_bd, pool_scale, d_skip,
           glu_w, branch_w, out_w, n_rows, n_lat, n_ctx):
    bsz = x.shape[0]
    tm = TOKEN_TILE
    assert n_ctx % tm == 0 or n_rows == n_lat
    halo_blocks = xa.shape[1] // POOL_HALO
    per_tile = tm // POOL_HALO
    tok = lambda width: pl.BlockSpec((1, tm, width), lambda b, t: (b, t, 0))
    single = dict(pipeline_mode=pl.Buffered(1))
    return pl.pallas_call(
        functools.partial(_merge_kernel, n_lat_tiles=n_lat // tm, n_lat=n_lat, n_ctx=n_ctx),
        grid=(bsz, n_rows // tm),
        in_specs=[
            tok(D_MODEL),
            _mod_spec(layer, 3, row_of), _mod_spec(layer, 4, row_of), _mod_spec(layer, 5, row_of),
            _const_spec((1, D_MODEL)),
            tok(256),
            pl.BlockSpec((1, POOL_HALO, 256), lambda b, t: (b, jnp.maximum(t * per_tile - 1, 0), 0)),
            pl.BlockSpec((1, POOL_HALO, 256),
                         lambda b, t: (b, jnp.minimum((t + 1) * per_tile, halo_blocks - 1), 0)),
            tok(256),
            pl.BlockSpec((1, 1, tm, 256), lambda b, t: (0, b, t, 0)),
            pl.BlockSpec((1, 1, tm, 256), lambda b, t: (1, b, t, 0)),
            tok(256), *_tail_specs(tm, 256, n_lat // tm),
            pl.BlockSpec((D_MODEL, PROJ_W + N_BRANCH * D_MODEL), lambda b, t: (0, 0), **single),
            _const_spec((256, 256)), _const_spec((1, 256)), _const_spec((1, 256)),
            _const_spec((256, 512)),
            pl.BlockSpec((N_BRANCH, 256, D_MODEL), lambda b, t: (0, 0, 0), **single),
            pl.BlockSpec((D_MODEL, D_MODEL), lambda b, t: (0, 0), **single),
        ],
        out_specs=tok(D_MODEL),
        out_shape=jax.ShapeDtypeStruct((bsz, n_rows, D_MODEL), F32),
        scratch_shapes=[pltpu.VMEM((4, tm + 4 * POOL_HALO, 256), F32)],
        compiler_params=_cparams(("parallel", "parallel")),
        name="merge",
    )(x, mod, mod, mod, norm_g.reshape(1, D_MODEL), xa, xa, xa, y_b, y_s5, y_s5, u, y_d, y_d_tail,
      w_gate, pool_bd, pool_scale.reshape(1, 256), d_skip.reshape(1, 256), glu_w, branch_w, out_w)


def _rope_tables(n_lat, n_ctx):
    rows = n_lat // GRID_W
    n_freq = HEAD_DIM // 4
    row = jnp.repeat(jnp.arange(rows), GRID_W)
    col = jnp.tile(jnp.arange(GRID_W), rows)
    inv = ROPE_THETA ** (-jnp.arange(n_freq, dtype=F32) / n_freq)
    ang = jnp.stack([row, col], axis=-1).astype(F32)[..., None] * inv
    cos = jnp.broadcast_to(jnp.cos(ang)[:, :, None, :], (n_lat, 2, 2, n_freq)).reshape(n_lat, HEAD_DIM)
    sign = jnp.array([-1.0, 1.0], F32)[None, None, :, None]
    sin = (jnp.sin(ang)[:, :, None, :] * sign).reshape(n_lat, HEAD_DIM)
    cos = jnp.concatenate([cos, jnp.ones((n_ctx, HEAD_DIM), F32)], axis=0)
    sin = jnp.concatenate([sin, jnp.zeros((n_ctx, HEAD_DIM), F32)], axis=0)
    return jnp.tile(cos, (1, 2)), jnp.tile(sin, (1, 2))


def _largest_divisor(n, candidates):
    for c in candidates:
        if n % c == 0:
            return c
    raise ValueError(f"no tile in {candidates} divides {n}")


def _pool_block_diag(pool_w):
    groups, width, _ = pool_w.shape
    eye = jnp.eye(groups, dtype=pool_w.dtype)
    return jnp.einsum('gcd,gk->gckd', pool_w, eye).reshape(groups * width, groups * width)


def kernel(x, c, ctx, c_ctx, w_mod, b_mod, norm_g, ffn_in, ffn_out, w_in, win_sink, qk_norm, pool_w, pool_scale, ssm_a_re, ssm_a_im, ssm_log_dt, ssm_b_re, ssm_b_im, ssm_c_re, ssm_c_im, ssm_d, glu_w, branch_w, out_w, final_g):
    bsz, n_lat, d = x.shape
    n_ctx = ctx.shape[1]
    depth = w_mod.shape[0]
    t_all = n_lat + n_ctx
    assert d == D_MODEL and bsz < MOD_ROWS and n_lat % TOKEN_TILE == 0 and n_ctx == TOKEN_TILE
    assert n_lat % GRID_W == 0

    cond_rows = jnp.zeros((MOD_ROWS, d), F32).at[:bsz].set(c).at[bsz].set(c_ctx)
    mod = _modulation(cond_rows, w_mod, b_mod)
    n_lat_tiles = n_lat // TOKEN_TILE
    row_of = lambda b, t: jnp.where(t >= n_lat_tiles, bsz, b)
    lat_row = lambda b, t: b

    cos, sin = _rope_tables(n_lat, n_ctx)
    tq = _largest_divisor(n_lat, (1024, 512, 256))
    tk = _largest_divisor(t_all, (3328, 1280, 1024, 768, 512, 256))

    xs = x
    for l in range(depth):
        last = l == depth - 1
        bf = lambda w: w.astype(BF16)
        xs = _ffn(xs, mod, l, 0, row_of, norm_g[l, 0], bf(ffn_in[l, 0]), bf(ffn_out[l, 0]), t_all,
                  tail=ctx if l == 0 else None)

        score_bound = 1.02 * HEAD_DIM ** 0.5 * jnp.max(jnp.abs(qk_norm[l, 0])) * jnp.max(jnp.abs(qk_norm[l, 1]))
        w_in_l = bf(w_in[l])
        qw, qt, kw, vw, ka, vt, u, xa = _proj(xs, mod, l, row_of, norm_g[l, 1], w_in_l,
                                              cos, sin, qk_norm[l, 0], qk_norm[l, 1], score_bound)
        y_b = _window_attention(qw, kw, vw, win_sink[l], n_lat, n_ctx, not last)

        def global_attention(q_row0, n_q, k_row0, n_k, tq_, tk_):
            run = lambda bounded: lambda: _global_attention(qt, ka, vt, q_row0, n_q, k_row0, n_k, tq_, tk_,
                                                            bounded)
            return lax.cond(score_bound <= MAX_SCORE_BOUND, run(True), run(False))

        y_d = global_attention(0, n_lat, 0, t_all, tq, tk)
        tables = _s5_tables(ssm_a_re[l], ssm_a_im[l], ssm_log_dt[l], ssm_b_re[l], ssm_b_im[l],
                            ssm_c_re[l], ssm_c_im[l])
        y_s5 = _s5_scan(u, tables, n_lat, n_ctx)
        n_rows, y_d_ctx = n_lat, y_d
        if not last:
            y_d_ctx = global_attention(n_lat, n_ctx, n_lat, n_ctx, n_ctx, n_ctx)
            n_rows = t_all
        xs = _merge(xs, mod, l, row_of, norm_g[l, 1], xa, y_b, y_s5, u, y_d, y_d_ctx, w_in_l,
                    bf(_pool_block_diag(pool_w[l])), pool_scale[l], ssm_d[l], bf(glu_w[l]), bf(branch_w[l]),
                    bf(out_w[l]), n_rows, n_lat, n_ctx)
        xs = _ffn(xs, mod, l, 2, row_of if not last else lat_row, norm_g[l, 2], bf(ffn_in[l, 1]),
                  bf(ffn_out[l, 1]), n_rows, final_g=final_g if last else None)
    return xs
```

```python
import functools
import math

import jax
import jax.numpy as jnp
import numpy as np
from jax import lax
from jax.experimental import pallas as pl
from jax.experimental.pallas import tpu as pltpu

F32 = jnp.float32
BF16 = jnp.bfloat16

D_MODEL = 1024
D_FF = 2816
N_SUB = 3
BRANCH_W = 256
HEAD_DIM = 64
GRID_W = 64
BLOCK = 128
EPS = 1e-6
ROPE_THETA = 10000.0
N_BRANCH = 4
POOL_WINDOWS = (2, 4, 8, 16)
SSM_GROUPS = 16
SSM_GROUP = 16
SSM_STATE = 64
N_STATE = SSM_GROUPS * SSM_STATE
PROJ_W = 6 * BRANCH_W
MOD_ROWS = 8
POOL_HALO = 8
SCAN_ROWS = 8
STATE_CHUNK = 128
WIN_BLOCKS = 2
NEG = -1e30
QK_SCALE = HEAD_DIM ** -0.5
LOG2E = math.log2(math.e)
V_ROWS = HEAD_DIM + 16
MAX_SCORE_BOUND = 40.0
TOKEN_TILE = 256
VMEM_LIMIT = 56 * 1024 * 1024


def _cparams(sem):
    return pltpu.CompilerParams(dimension_semantics=sem, vmem_limit_bytes=VMEM_LIMIT)


def _dot(a, b):
    return jnp.dot(a, b, preferred_element_type=F32)


def _dot_nt(a, b):
    return lax.dot_general(a, b, (((1,), (1,)), ((), ())), preferred_element_type=F32)


def _rms(x, g):
    ms = jnp.mean(x * x, axis=-1, keepdims=True)
    return x * lax.rsqrt(ms + EPS) * g


def _silu(x):
    return x * jax.nn.sigmoid(x)


def _const_spec(shape):
    nd = len(shape)
    return pl.BlockSpec(shape, lambda *_: (0,) * nd)


def _mod_kernel(s_ref, w_ref, b_ref, o_ref):
    s = _silu(s_ref[...])
    w = w_ref[0]
    s_hi = s.astype(BF16)
    s_lo = (s - s_hi.astype(F32)).astype(BF16)
    w_hi = w.astype(BF16)
    w_lo = (w - w_hi.astype(F32)).astype(BF16)
    o_ref[0] = _dot(s_hi, w_hi) + _dot(s_hi, w_lo) + _dot(s_lo, w_hi) + b_ref[0]


def _modulation(rows, w_mod, b_mod):
    depth, d, width = w_mod.shape
    tn = 1024
    out = pl.pallas_call(
        _mod_kernel,
        grid=(depth, width // tn),
        in_specs=[
            pl.BlockSpec((MOD_ROWS, d), lambda l, j: (0, 0)),
            pl.BlockSpec((1, d, tn), lambda l, j: (l, 0, j)),
            pl.BlockSpec((1, 1, tn), lambda l, j: (l, 0, j)),
        ],
        out_specs=pl.BlockSpec((1, MOD_ROWS, tn), lambda l, j: (l, 0, j)),
        out_shape=jax.ShapeDtypeStruct((depth, MOD_ROWS, width), F32),
        compiler_params=_cparams(("parallel", "parallel")),
        name="modulation",
    )(rows, w_mod, b_mod.reshape(depth, 1, width))
    return out.reshape(depth, MOD_ROWS, N_SUB * 3, 1, d).transpose(0, 2, 1, 3, 4)


def _mod_spec(layer, vec, row_of):
    return pl.BlockSpec((None, None, None, 1, D_MODEL), lambda b, t: (layer, vec, row_of(b, t), 0, 0))


def _tail_rows(main_ref, tail_ref, n_main_tiles):
    return jnp.where(pl.program_id(1) >= n_main_tiles, tail_ref[0], main_ref[0])


def _ffn_kernel(x_ref, sh_ref, sc_ref, gt_ref, ng_ref, win_ref, wout_ref, *rest, final, n_main_tiles):
    o_ref = rest[-1]
    x = x_ref[0] if n_main_tiles is None else _tail_rows(x_ref, rest[0], n_main_tiles)
    h = (_rms(x, ng_ref[...]) * (1.0 + sc_ref[...]) + sh_ref[...]).astype(BF16)
    gu = _dot(h, win_ref[...])
    a = (_silu(gu[:, :D_FF]) * gu[:, D_FF:]).astype(BF16)
    xn = x + (0.5 * gt_ref[...]) * _dot(a, wout_ref[...])
    if final:
        xn = _rms(xn, rest[0][...])
    o_ref[0] = xn


def _tail_specs(tm, width, n_main_tiles):
    return (pl.BlockSpec((1, tm, width), lambda b, t: (b, jnp.minimum(t, n_main_tiles - 1), 0)),
            pl.BlockSpec((1, tm, width), lambda b, t: (b, jnp.maximum(t - n_main_tiles, 0), 0)))


def _ffn(x, mod, layer, sub, row_of, norm_g, w_in, w_out, n_rows, final_g=None, tail=None):
    bsz = x.shape[0]
    tm = TOKEN_TILE
    final = final_g is not None
    assert not (final and tail is not None)
    n_main_tiles = None if tail is None else x.shape[1] // tm
    x_spec = pl.BlockSpec((1, tm, D_MODEL), lambda b, t: (b, t, 0))
    if tail is not None:
        x_spec, tail_spec = _tail_specs(tm, D_MODEL, n_main_tiles)
    in_specs = [
        x_spec,
        _mod_spec(layer, sub * 3 + 0, row_of),
        _mod_spec(layer, sub * 3 + 1, row_of),
        _mod_spec(layer, sub * 3 + 2, row_of),
        _const_spec((1, D_MODEL)),
        pl.BlockSpec((D_MODEL, 2 * D_FF), lambda b, t: (0, 0), pipeline_mode=pl.Buffered(1)),
        pl.BlockSpec((D_FF, D_MODEL), lambda b, t: (0, 0), pipeline_mode=pl.Buffered(1)),
    ]
    args = [x, mod, mod, mod, norm_g.reshape(1, D_MODEL), w_in, w_out]
    if final:
        in_specs.append(_const_spec((1, D_MODEL)))
        args.append(final_g.reshape(1, D_MODEL))
    if tail is not None:
        in_specs.append(tail_spec)
        args.append(tail)
    return pl.pallas_call(
        functools.partial(_ffn_kernel, final=final, n_main_tiles=n_main_tiles),
        grid=(bsz, n_rows // tm),
        in_specs=in_specs,
        out_specs=pl.BlockSpec((1, tm, D_MODEL), lambda b, t: (b, t, 0)),
        out_shape=jax.ShapeDtypeStruct((bsz, n_rows, D_MODEL), F32),
        compiler_params=_cparams(("parallel", "parallel")),
        name="ffn_final" if final else "ffn",
    )(*args)


def _rope(x, cos, sin, first_half):
    w = x.shape[-1]
    swapped = jnp.where(first_half, pltpu.roll(x, w - 16, 1), pltpu.roll(x, 16, 1))
    return x * cos + swapped * sin


def _head_rms(x, g, low_head):
    x2 = x * x
    lo = jnp.sum(jnp.where(low_head, x2, 0.0), axis=-1, keepdims=True)
    hi = jnp.sum(jnp.where(low_head, 0.0, x2), axis=-1, keepdims=True)
    ms = jnp.where(low_head, lo, hi) * (1.0 / HEAD_DIM)
    return x * lax.rsqrt(ms + EPS) * g


def _store_padded(ref, x, low_head):
    sw = pltpu.roll(x, HEAD_DIM, 1)
    zero = jnp.zeros_like(x)
    ref[0, 0, 0] = jnp.where(low_head, x, zero).astype(ref.dtype)
    ref[0, 0, 1] = jnp.where(low_head, zero, sw).astype(ref.dtype)
    ref[0, 1, 0] = jnp.where(low_head, sw, zero).astype(ref.dtype)
    ref[0, 1, 1] = jnp.where(low_head, zero, x).astype(ref.dtype)


def _proj_kernel(shift_ref, x_ref, sh_ref, sc_ref, ng_ref, w_ref, cos_ref, sin_ref, qg_ref, kg_ref,
                 qw_ref, qt_ref, kw_ref, vw_ref, ka_ref, vt_ref, u_ref, xa_ref):
    x = x_ref[0]
    tm = x.shape[0]
    h = (_rms(x, ng_ref[...]) * (1.0 + sc_ref[...]) + sh_ref[...]).astype(BF16)
    p = _dot(h, w_ref[...])
    lane = lax.broadcasted_iota(jnp.int32, (tm, 2 * HEAD_DIM), 1)
    low_head = lane < HEAD_DIM
    first_half = jnp.bitwise_and(lane, 31) < 16
    cos = cos_ref[...]
    sin = sin_ref[...]
    rope = lambda z: _rope(z, cos, sin, first_half)
    k_win = rope(p[:, 0:128])
    v_win = p[:, 128:256]
    u_ref[0] = p[:, 256:512]
    k_glb = rope(_head_rms(p[:, 512:640], kg_ref[...], low_head))
    v_glb = p[:, 640:768]
    zero = jnp.zeros_like(k_glb)
    ka_ref[0, 0] = jnp.where(low_head, k_glb, jnp.where(lane == HEAD_DIM, 1.0, zero)).astype(BF16)
    ka_ref[0, 1] = jnp.where(low_head, jnp.where(lane == 0, 1.0, zero), k_glb).astype(BF16)
    first_row = lax.broadcasted_iota(jnp.int32, (HEAD_DIM, tm), 0) == 0
    shift_rows = jnp.where(first_row, shift_ref[0], 0.0)
    for c in range(2):
        qw = rope(p[:, 768 + 128 * c:896 + 128 * c]) * QK_SCALE
        qw_ref[0, :, 128 * c:128 * (c + 1)] = qw.astype(BF16)
        qd = rope(_head_rms(p[:, 1024 + 128 * c:1152 + 128 * c], qg_ref[...], low_head)) * (QK_SCALE * LOG2E)
        q_t = qd.T
        for g in range(2):
            own = q_t[HEAD_DIM * g:HEAD_DIM * (g + 1)]
            pair = [own, shift_rows] if c == 0 else [shift_rows, own]
            qt_ref[0, 2 * c + g] = jnp.concatenate(pair, axis=0).astype(BF16)
    v_t = v_glb.T
    ones_rows = jnp.where(lax.broadcasted_iota(jnp.int32, (V_ROWS - HEAD_DIM, tm), 0) == 0, 1.0, 0.0)
    for j in range(2):
        vt_ref[0, j] = jnp.concatenate([v_t[HEAD_DIM * j:HEAD_DIM * (j + 1)], ones_rows], axis=0).astype(BF16)
    xa_ref[0] = p[:, 1280:1536]
    _store_padded(kw_ref, k_win, low_head)
    _store_padded(vw_ref, v_win, low_head)


def _proj(x, mod, layer, row_of, norm_g, w, cos, sin, q_g, k_g, score_bound):
    bsz, t_all, _ = x.shape
    tm = TOKEN_TILE
    tok = lambda width: pl.BlockSpec((1, tm, width), lambda b, t: (b, t, 0))
    pad = pl.BlockSpec((1, 2, 2, tm, 128), lambda b, t: (b, 0, 0, t, 0))
    tab = pl.BlockSpec((tm, 128), lambda b, t: (t, 0))
    pad_shape = jax.ShapeDtypeStruct((bsz, 2, 2, t_all, 128), BF16)
    return pl.pallas_call(
        _proj_kernel,
        grid=(bsz, t_all // tm),
        in_specs=[
            pl.BlockSpec(memory_space=pltpu.SMEM),
            tok(D_MODEL),
            _mod_spec(layer, 3, row_of),
            _mod_spec(layer, 4, row_of),
            _const_spec((1, D_MODEL)),
            pl.BlockSpec((D_MODEL, PROJ_W), lambda b, t: (0, 0), pipeline_mode=pl.Buffered(1)),
            tab, tab,
            _const_spec((1, 128)), _const_spec((1, 128)),
        ],
        out_specs=[
            tok(256),
            pl.BlockSpec((1, 4, 128, tm), lambda b, t: (b, 0, 0, t)),
            pad, pad,
            pl.BlockSpec((1, 2, tm, 128), lambda b, t: (b, 0, t, 0)),
            pl.BlockSpec((1, 2, V_ROWS, tm), lambda b, t: (b, 0, 0, t)),
            tok(256), tok(256),
        ],
        out_shape=[
            jax.ShapeDtypeStruct((bsz, t_all, 256), BF16),
            jax.ShapeDtypeStruct((bsz, 4, 128, t_all), BF16),
            pad_shape, pad_shape,
            jax.ShapeDtypeStruct((bsz, 2, t_all, 128), BF16),
            jax.ShapeDtypeStruct((bsz, 2, V_ROWS, t_all), BF16),
            jax.ShapeDtypeStruct((bsz, t_all, 256), F32),
            jax.ShapeDtypeStruct((bsz, t_all, 256), F32),
        ],
        compiler_params=_cparams(("parallel", "parallel")),
        name="proj",
    )((-score_bound * LOG2E).reshape(1), x, mod, mod, norm_g.reshape(1, D_MODEL), w, cos, sin,
      jnp.tile(q_g, 2).reshape(1, 128), jnp.tile(k_g, 2).reshape(1, 128))


def _win_kernel(sink_ref, q_ref, kp_ref, kc_ref, kn_ref, kx_ref, vp_ref, vc_ref, vn_ref, vx_ref,
                o_ref, *, n_lat_blocks):
    i = pl.program_id(1)
    tile = WIN_BLOCKS * BLOCK
    band = tile + 2 * BLOCK
    r = lax.broadcasted_iota(jnp.int32, (tile, band), 0)
    c = lax.broadcasted_iota(jnp.int32, (tile, band), 1)
    blk0 = i * WIN_BLOCKS
    lo = jnp.where(blk0 >= 1, 0, BLOCK)
    hi = jnp.where(blk0 < n_lat_blocks, jnp.minimum((n_lat_blocks - blk0 + 1) * BLOCK, band), 0)
    in_window = (c - r).astype(jnp.uint32) <= 2 * BLOCK
    in_range = (c - lo).astype(jnp.uint32) < (hi - lo).astype(jnp.uint32)
    bias = jnp.where(in_window, jnp.where(in_range, 0.0, NEG), NEG)
    for j in range(2):
        q = q_ref[0, :, 128 * j:128 * (j + 1)]
        acc = jnp.zeros((tile, 128), F32)
        for g in range(2):
            k_band = jnp.concatenate([kp_ref[0, j, g], kc_ref[0, j, g], kn_ref[0, j, g]], axis=0)
            s = jnp.concatenate([_dot_nt(q, k_band) + bias, _dot_nt(q, kx_ref[0, j, g])], axis=1)
            v = jnp.concatenate([vp_ref[0, j, g], vc_ref[0, j, g], vn_ref[0, j, g], vx_ref[0, j, g]], axis=0)
            sink = sink_ref[2 * j + g]
            m = jnp.maximum(s.max(-1, keepdims=True), sink)
            p = jnp.exp(s - m)
            den = p.sum(-1, keepdims=True) + jnp.exp(sink - m)
            acc = acc + _dot(p.astype(BF16), v) / den
        o_ref[0, :, 128 * j:128 * (j + 1)] = acc.astype(o_ref.dtype)


def _window_attention(q, k_pad, v_pad, sink, n_lat, n_ctx, with_ctx):
    bsz, t_all, _ = q.shape
    n_lat_blocks = n_lat // BLOCK
    n_blocks = t_all // BLOCK
    n_q_blocks = n_blocks if with_ctx else n_lat_blocks
    assert n_q_blocks % WIN_BLOCKS == 0 and n_lat_blocks % WIN_BLOCKS == 0
    tile = WIN_BLOCKS * BLOCK
    edge = lambda off: pl.BlockSpec(
        (1, 2, 2, BLOCK, 128), lambda b, i: (b, 0, 0, jnp.clip(i * WIN_BLOCKS + off, 0, n_blocks - 1), 0))
    cur = pl.BlockSpec((1, 2, 2, tile, 128), lambda b, i: (b, 0, 0, i, 0))
    ctx = pl.BlockSpec((1, 2, 2, n_ctx, 128), lambda b, i: (b, 0, 0, n_lat // n_ctx, 0))
    return pl.pallas_call(
        functools.partial(_win_kernel, n_lat_blocks=n_lat_blocks),
        grid=(bsz, n_q_blocks // WIN_BLOCKS),
        in_specs=[
            pl.BlockSpec(memory_space=pltpu.SMEM),
            pl.BlockSpec((1, tile, 256), lambda b, i: (b, i, 0)),
            edge(-1), cur, edge(WIN_BLOCKS), ctx,
            edge(-1), cur, edge(WIN_BLOCKS), ctx,
        ],
        out_specs=pl.BlockSpec((1, tile, 256), lambda b, i: (b, i, 0)),
        out_shape=jax.ShapeDtypeStruct((bsz, n_q_blocks * BLOCK, 256), BF16),
        compiler_params=_cparams(("parallel", "parallel")),
        name="window_attention",
    )(sink, q, k_pad, k_pad, k_pad, k_pad, v_pad, v_pad, v_pad, v_pad)


def _glb_finish(acc_ref, o_ref):
    o_t = jnp.concatenate([acc_ref[g, 0:HEAD_DIM] / acc_ref[g, HEAD_DIM:HEAD_DIM + 1] for g in range(2)],
                          axis=0)
    o_ref[0] = o_t.T.astype(o_ref.dtype)


def _glb_bounded_kernel(qt_ref, k_ref, vt_ref, o_ref, acc_ref, *, tk):
    for ki in range(k_ref.shape[2] // tk):
        keys = slice(ki * tk, (ki + 1) * tk)
        for g in range(2):
            p = jnp.exp2(_dot(k_ref[0, 0, keys], qt_ref[0, g])).astype(BF16)
            pv = _dot(vt_ref[0, 0, :, keys], p)
            acc_ref[g] = pv if ki == 0 else acc_ref[g] + pv
    _glb_finish(acc_ref, o_ref)


def _glb_online_kernel(qt_ref, k_ref, vt_ref, o_ref, acc_ref, m_ref):
    ki = pl.program_id(3)

    @pl.when(ki == 0)
    def _():
        acc_ref[...] = jnp.zeros_like(acc_ref)
        m_ref[...] = jnp.full_like(m_ref, NEG)

    for g in range(2):
        s = _dot(k_ref[0, 0], qt_ref[0, g])
        m_prev = m_ref[g, 0:1]
        m_new = jnp.maximum(m_prev, s.max(axis=0, keepdims=True))
        p = jnp.exp2(s - m_new).astype(BF16)
        acc_ref[g] = jnp.exp2(m_prev - m_new) * acc_ref[g] + _dot(vt_ref[0, 0], p)
        m_ref[g] = jnp.broadcast_to(m_new, m_ref.shape[1:])

    @pl.when(ki == pl.num_programs(3) - 1)
    def _():
        _glb_finish(acc_ref, o_ref)


def _global_attention(qt, k_aug, vt_aug, q_row0, n_q, k_row0, n_k, tq, tk, bounded):
    bsz = qt.shape[0]
    q0, k0 = q_row0 // tq, k_row0 // tk
    scratch = [pltpu.VMEM((2, V_ROWS, tq), F32)]
    if bounded:
        once = dict(pipeline_mode=pl.Buffered(1))
        return pl.pallas_call(
            functools.partial(_glb_bounded_kernel, tk=tk),
            grid=(bsz, 2, n_q // tq),
            in_specs=[
                pl.BlockSpec((1, 2, 128, tq), lambda b, j, qi: (b, j, 0, q0 + qi)),
                pl.BlockSpec((1, 1, n_k, 128), lambda b, j, qi: (b, j, k_row0 // n_k, 0), **once),
                pl.BlockSpec((1, 1, V_ROWS, n_k), lambda b, j, qi: (b, j, 0, k_row0 // n_k), **once),
            ],
            out_specs=pl.BlockSpec((1, tq, 128), lambda b, j, qi: (b, qi, j)),
            out_shape=jax.ShapeDtypeStruct((bsz, n_q, 256), BF16),
            scratch_shapes=scratch,
            compiler_params=_cparams(("parallel", "parallel", "parallel")),
            name="global_attention",
        )(qt, k_aug, vt_aug)
    scratch.append(pltpu.VMEM((2, 8, tq), F32))
    return pl.pallas_call(
        _glb_online_kernel,
        grid=(bsz, 2, n_q // tq, n_k // tk),
        in_specs=[
            pl.BlockSpec((1, 2, 128, tq), lambda b, j, qi, ki: (b, j, 0, q0 + qi)),
            pl.BlockSpec((1, 1, tk, 128), lambda b, j, qi, ki: (b, j, k0 + ki, 0)),
            pl.BlockSpec((1, 1, V_ROWS, tk), lambda b, j, qi, ki: (b, j, 0, k0 + ki)),
        ],
        out_specs=pl.BlockSpec((1, tq, 128), lambda b, j, qi, ki: (b, qi, j)),
        out_shape=jax.ShapeDtypeStruct((bsz, n_q, 256), BF16),
        scratch_shapes=scratch,
        compiler_params=_cparams(("parallel", "parallel", "parallel", "arbitrary")),
        name="global_attention_online",
    )(qt, k_aug, vt_aug)


def _s5_tile(u, bmat_ref, cmat_ref, a_ref, pc_ref, carry_ref, reverse):
    rows = u.shape[0]
    n_groups = rows // SCAN_ROWS
    w = STATE_CHUNK
    ub = u.astype(BF16)
    order = range(n_groups - 1, -1, -1) if reverse else range(n_groups)
    last = 0 if reverse else SCAN_ROWS - 1
    y = None
    for c in range(N_STATE // w):
        bu = _dot(ub, bmat_ref[0, c])
        pr, pi = pc_ref[0, c, :, :w], pc_ref[0, c, :, w:]
        cr, ci = carry_ref[c, :, :w], carry_ref[c, :, w:]
        states = [None] * n_groups
        for g in order:
            vr = bu[SCAN_ROWS * g:SCAN_ROWS * (g + 1), :w]
            vi = bu[SCAN_ROWS * g:SCAN_ROWS * (g + 1), w:]
            for s, k in enumerate((1, 2, 4)):
                ar, ai = a_ref[0, s, c, :, :w], a_ref[0, s, c, :, w:]
                shift = SCAN_ROWS - k if reverse else k
                sr = pltpu.roll(vr, shift, 0)
                si = pltpu.roll(vi, shift, 0)
                vr, vi = vr + ar * sr - ai * si, vi + ar * si + ai * sr
            vr, vi = vr + pr * cr - pi * ci, vi + pr * ci + pi * cr
            cr = jnp.broadcast_to(vr[last:last + 1], (SCAN_ROWS, w))
            ci = jnp.broadcast_to(vi[last:last + 1], (SCAN_ROWS, w))
            states[g] = jnp.concatenate([vr, vi], axis=1)
        carry_ref[c, :, :w] = cr
        carry_ref[c, :, w:] = ci
        part = _dot(jnp.concatenate(states, axis=0).astype(BF16), cmat_ref[0, c])
        y = part if y is None else y + part
    return y


def _s5_kernel(u_ref, bmat_ref, cmat_ref, a_ref, pc_ref, y_ref, carry_ref):
    direction = pl.program_id(0)
    step = pl.program_id(2)

    @pl.when(step == 0)
    def _():
        carry_ref[...] = jnp.zeros_like(carry_ref)

    for reverse in (False, True):
        @pl.when(direction == int(reverse))
        def _(reverse=reverse):
            y_ref[0, 0] = _s5_tile(u_ref[0], bmat_ref, cmat_ref, a_ref, pc_ref, carry_ref, reverse)


def _s5_tables(a_re, a_im, log_dt, b_re, b_im, c_re, c_im):
    dt = jnp.exp(log_dt)[..., None]
    re_dt, im_dt = a_re * dt, a_im * dt
    lb_re, lb_im = jnp.exp(re_dt) * jnp.cos(im_dt), jnp.exp(re_dt) * jnp.sin(im_dt)
    den = a_re * a_re + a_im * a_im
    f_re = ((lb_re - 1.0) * a_re + lb_im * a_im) / den
    f_im = (lb_im * a_re - (lb_re - 1.0) * a_im) / den
    bb_re = f_re[..., None] * b_re - f_im[..., None] * b_im
    bb_im = f_re[..., None] * b_im + f_im[..., None] * b_re
    eye = jnp.eye(SSM_GROUPS, dtype=F32)
    in_w, st_w, n_chunks = SSM_GROUPS * SSM_GROUP, N_STATE, N_STATE // STATE_CHUNK
    b_blk = lambda z: jnp.einsum('dgph,gk->dghkp', z, eye).reshape(2, in_w, n_chunks, STATE_CHUNK)
    bmat = jnp.concatenate([b_blk(bb_re), b_blk(bb_im)], axis=-1).transpose(0, 2, 1, 3).astype(BF16)
    c_blk = lambda z: jnp.einsum('dghp,gk->dgpkh', z, eye).reshape(2, n_chunks, STATE_CHUNK, in_w)
    cmat = jnp.concatenate([c_blk(c_re), -c_blk(c_im)], axis=2).astype(BF16)
    steps = jnp.arange(1, SCAN_ROWS + 1, dtype=F32)[None, :, None, None]
    mag = jnp.exp(re_dt[:, None] * steps)
    ang = im_dt[:, None] * steps
    chunked = lambda z: z.reshape(2, SCAN_ROWS, n_chunks, STATE_CHUNK)
    powers = jnp.concatenate([chunked(mag * jnp.cos(ang)), chunked(mag * jnp.sin(ang))], axis=-1)
    row = jnp.arange(SCAN_ROWS)[:, None, None]
    pc = jnp.stack([powers[0], powers[1][::-1]]).transpose(0, 2, 1, 3)
    a_tabs = []
    for k in (1, 2, 4):
        fwd = jnp.where(row >= k, powers[0, k - 1][None], 0.0)
        bwd = jnp.where(row <= SCAN_ROWS - 1 - k, powers[1, k - 1][None], 0.0)
        a_tabs.append(jnp.stack([fwd, bwd]).transpose(0, 2, 1, 3))
    return bmat, cmat, jnp.stack(a_tabs, axis=1), pc


def _s5_scan(u, tables, n_lat, n_ctx):
    bsz, t_all, width = u.shape
    tm = TOKEN_TILE
    assert n_ctx == tm
    nt = n_lat // tm
    bmat, cmat, a_tab, pc = tables
    n_chunks, lanes = N_STATE // STATE_CHUNK, 2 * STATE_CHUNK
    tile_of = lambda d, s: jnp.where(s == 0, nt, jnp.where(d == 0, s - 1, nt - s))
    return pl.pallas_call(
        _s5_kernel,
        grid=(2, bsz, nt + 1),
        in_specs=[
            pl.BlockSpec((1, tm, width), lambda d, b, s: (b, tile_of(d, s), 0)),
            pl.BlockSpec((1, n_chunks, width, lanes), lambda d, b, s: (d, 0, 0, 0)),
            pl.BlockSpec((1, n_chunks, lanes, width), lambda d, b, s: (d, 0, 0, 0)),
            pl.BlockSpec((1, 3, n_chunks, SCAN_ROWS, lanes), lambda d, b, s: (d, 0, 0, 0, 0)),
            pl.BlockSpec((1, n_chunks, SCAN_ROWS, lanes), lambda d, b, s: (d, 0, 0, 0)),
        ],
        out_specs=pl.BlockSpec((1, 1, tm, width), lambda d, b, s: (d, b, tile_of(d, s), 0)),
        out_shape=jax.ShapeDtypeStruct((2, bsz, t_all, width), F32),
        scratch_shapes=[pltpu.VMEM((n_chunks, SCAN_ROWS, lanes), F32)],
        compiler_params=_cparams(("parallel", "parallel", "arbitrary")),
        name="s5_scan",
    )(u, bmat, cmat, a_tab, pc)


def _gelu_tanh(x):
    c = math.sqrt(2.0 / math.pi)
    return x * (0.5 * (1.0 + jnp.tanh(c * (x + 0.044715 * (x * x * x)))))


def _merge_kernel(x_ref, sh_ref, sc_ref, gt_ref, ng_ref, xa_ref, xap_ref, xan_ref, yb_ref, yf_ref, yr_ref,
                  u_ref, yd_ref, ydt_ref, wg_ref, pw_ref, ps_ref, dsk_ref, glu_ref, wb_ref, wo_ref, o_ref, ext_ref,
                  *, n_lat_tiles, n_lat, n_ctx):
    t = pl.program_id(1)
    x = x_ref[0]
    tm = x.shape[0]
    bw = BRANCH_W
    h = (_rms(x, ng_ref[...]) * (1.0 + sc_ref[...]) + sh_ref[...]).astype(BF16)
    gates = [jax.nn.sigmoid(_dot(h, wg_ref[:, PROJ_W + D_MODEL * k:PROJ_W + D_MODEL * (k + 1)]))
             for k in range(N_BRANCH)]
    is_ctx = t >= n_lat_tiles
    seg_tile = jnp.where(is_ctx, t - n_lat_tiles, t)
    seg_tiles = jnp.where(is_ctx, n_ctx // tm, n_lat_tiles)
    seg_len = jnp.where(is_ctx, n_ctx, n_lat)
    xa = xa_ref[0]
    halo, span = POOL_HALO, tm + 2 * POOL_HALO
    for k in range(4):
        ext_ref[k, 0:halo, :] = jnp.zeros((halo, bw), F32)
        ext_ref[k, halo + span:, :] = jnp.zeros((halo, bw), F32)
    ext_ref[0, halo:2 * halo, :] = jnp.where(seg_tile > 0, xap_ref[0], 0.0)
    ext_ref[0, 2 * halo:2 * halo + tm, :] = xa
    ext_ref[0, 2 * halo + tm:halo + span, :] = jnp.where(seg_tile + 1 < seg_tiles, xan_ref[0], 0.0)
    wide = lambda k, off: ext_ref[k, halo + off:halo + off + span, :]
    ext_ref[1, halo:halo + span, :] = wide(0, -1) + wide(0, 0)
    ext_ref[2, halo:halo + span, :] = wide(1, -1) + wide(1, 1)
    ext_ref[3, halo:halo + span, :] = wide(2, -2) + wide(2, 2)
    own = lambda k, off: ext_ref[k, 2 * halo + off:2 * halo + off + tm, :]
    sums = [own(1, 0), own(2, 0), own(3, 0), own(3, -4) + own(3, 4)]
    lane = lax.broadcasted_iota(jnp.int32, (tm, bw), 1)
    pos = seg_tile * tm + lax.broadcasted_iota(jnp.int32, (tm, bw), 0)
    group = lane // (bw // len(POOL_WINDOWS))
    half_w = jnp.left_shift(1, group)
    count = jnp.minimum(pos + half_w, seg_len) - jnp.maximum(pos - half_w, 0)
    window_sum = jnp.where(group == 0, sums[0], jnp.where(group == 1, sums[1],
                                                           jnp.where(group == 2, sums[2], sums[3])))
    diff = window_sum / count.astype(F32) - xa
    y_a = _dot(diff.astype(BF16), pw_ref[...]) * ps_ref[...]
    y_s = yf_ref[0, 0] + yr_ref[0, 0] + dsk_ref[...] * u_ref[0]
    z = _dot(_gelu_tanh(y_s).astype(BF16), glu_ref[...])
    y_c = z[:, :bw] * jax.nn.sigmoid(z[:, bw:])
    branches = (y_a.astype(BF16), yb_ref[0], y_c.astype(BF16), _tail_rows(yd_ref, ydt_ref, n_lat_tiles))
    total = None
    for gate, y_k, k in zip(gates, branches, range(N_BRANCH)):
        term = gate * _dot(y_k, wb_ref[k])
        total = term if total is None else total + term
    o_ref[0] = x + gt_ref[...] * _dot(total.astype(BF16), wo_ref[...])


def _merge(x, mod, layer, row_of, norm_g, xa, y_b, y_s5, u, y_d, y_d_tail, w_gate, pool_bd, pool_scale, d_skip,
           glu_w, branch_w, out_w, n_rows, n_lat, n_ctx):
    bsz = x.shape[0]
    tm = TOKEN_TILE
    assert n_ctx % tm == 0 or n_rows == n_lat
    halo_blocks = xa.shape[1] // POOL_HALO
    per_tile = tm // POOL_HALO
    tok = lambda width: pl.BlockSpec((1, tm, width), lambda b, t: (b, t, 0))
    single = dict(pipeline_mode=pl.Buffered(1))
    return pl.pallas_call(
        functools.partial(_merge_kernel, n_lat_tiles=n_lat // tm, n_lat=n_lat, n_ctx=n_ctx),
        grid=(bsz, n_rows // tm),
        in_specs=[
            tok(D_MODEL),
            _mod_spec(layer, 3, row_of), _mod_spec(layer, 4, row_of), _mod_spec(layer, 5, row_of),
            _const_spec((1, D_MODEL)),
            tok(256),
            pl.BlockSpec((1, POOL_HALO, 256), lambda b, t: (b, jnp.maximum(t * per_tile - 1, 0), 0)),
            pl.BlockSpec((1, POOL_HALO, 256),
                         lambda b, t: (b, jnp.minimum((t + 1) * per_tile, halo_blocks - 1), 0)),
            tok(256),
            pl.BlockSpec((1, 1, tm, 256), lambda b, t: (0, b, t, 0)),
            pl.BlockSpec((1, 1, tm, 256), lambda b, t: (1, b, t, 0)),
            tok(256), *_tail_specs(tm, 256, n_lat // tm),
            pl.BlockSpec((D_MODEL, PROJ_W + N_BRANCH * D_MODEL), lambda b, t: (0, 0), **single),
            _const_spec((256, 256)), _const_spec((1, 256)), _const_spec((1, 256)),
            _const_spec((256, 512)),
            pl.BlockSpec((N_BRANCH, 256, D_MODEL), lambda b, t: (0, 0, 0), **single),
            pl.BlockSpec((D_MODEL, D_MODEL), lambda b, t: (0, 0), **single),
        ],
        out_specs=tok(D_MODEL),
        out_shape=jax.ShapeDtypeStruct((bsz, n_rows, D_MODEL), F32),
        scratch_shapes=[pltpu.VMEM((4, tm + 4 * POOL_HALO, 256), F32)],
        compiler_params=_cparams(("parallel", "parallel")),
        name="merge",
    )(x, mod, mod, mod, norm_g.reshape(1, D_MODEL), xa, xa, xa, y_b, y_s5, y_s5, u, y_d, y_d_tail,
      w_gate, pool_bd, pool_scale.reshape(1, 256), d_skip.reshape(1, 256), glu_w, branch_w, out_w)


def _rope_tables(n_lat, n_ctx):
    rows = n_lat // GRID_W
    n_freq = HEAD_DIM // 4
    row = jnp.repeat(jnp.arange(rows), GRID_W)
    col = jnp.tile(jnp.arange(GRID_W), rows)
    inv = ROPE_THETA ** (-jnp.arange(n_freq, dtype=F32) / n_freq)
    ang = jnp.stack([row, col], axis=-1).astype(F32)[..., None] * inv
    cos = jnp.broadcast_to(jnp.cos(ang)[:, :, None, :], (n_lat, 2, 2, n_freq)).reshape(n_lat, HEAD_DIM)
    sign = jnp.array([-1.0, 1.0], F32)[None, None, :, None]
    sin = (jnp.sin(ang)[:, :, None, :] * sign).reshape(n_lat, HEAD_DIM)
    cos = jnp.concatenate([cos, jnp.ones((n_ctx, HEAD_DIM), F32)], axis=0)
    sin = jnp.concatenate([sin, jnp.zeros((n_ctx, HEAD_DIM), F32)], axis=0)
    return jnp.tile(cos, (1, 2)), jnp.tile(sin, (1, 2))


def _largest_divisor(n, candidates):
    for c in candidates:
        if n % c == 0:
            return c
    raise ValueError(f"no tile in {candidates} divides {n}")


def _pool_block_diag(pool_w):
    groups, width, _ = pool_w.shape
    eye = jnp.eye(groups, dtype=pool_w.dtype)
    return jnp.einsum('gcd,gk->gckd', pool_w, eye).reshape(groups * width, groups * width)


def kernel(x, c, ctx, c_ctx, w_mod, b_mod, norm_g, ffn_in, ffn_out, w_in, win_sink, qk_norm, pool_w, pool_scale, ssm_a_re, ssm_a_im, ssm_log_dt, ssm_b_re, ssm_b_im, ssm_c_re, ssm_c_im, ssm_d, glu_w, branch_w, out_w, final_g):
    bsz, n_lat, d = x.shape
    n_ctx = ctx.shape[1]
    depth = w_mod.shape[0]
    t_all = n_lat + n_ctx
    assert d == D_MODEL and bsz < MOD_ROWS and n_lat % TOKEN_TILE == 0 and n_ctx == TOKEN_TILE
    assert n_lat % GRID_W == 0

    cond_rows = jnp.zeros((MOD_ROWS, d), F32).at[:bsz].set(c).at[bsz].set(c_ctx)
    mod = _modulation(cond_rows, w_mod, b_mod)
    n_lat_tiles = n_lat // TOKEN_TILE
    row_of = lambda b, t: jnp.where(t >= n_lat_tiles, bsz, b)
    lat_row = lambda b, t: b

    cos, sin = _rope_tables(n_lat, n_ctx)
    tq = _largest_divisor(n_lat, (1024, 512, 256))
    tk = _largest_divisor(t_all, (3328, 1280, 1024, 768, 512, 256))

    xs = x
    for l in range(depth):
        last = l == depth - 1
        bf = lambda w: w.astype(BF16)
        xs = _ffn(xs, mod, l, 0, row_of, norm_g[l, 0], bf(ffn_in[l, 0]), bf(ffn_out[l, 0]), t_all,
                  tail=ctx if l == 0 else None)

        score_bound = 1.02 * HEAD_DIM ** 0.5 * jnp.max(jnp.abs(qk_norm[l, 0])) * jnp.max(jnp.abs(qk_norm[l, 1]))
        w_in_l = bf(w_in[l])
        qw, qt, kw, vw, ka, vt, u, xa = _proj(xs, mod, l, row_of, norm_g[l, 1], w_in_l,
                                              cos, sin, qk_norm[l, 0], qk_norm[l, 1], score_bound)
        y_b = _window_attention(qw, kw, vw, win_sink[l], n_lat, n_ctx, not last)

        def global_attention(q_row0, n_q, k_row0, n_k, tq_, tk_):
            run = lambda bounded: lambda: _global_attention(qt, ka, vt, q_row0, n_q, k_row0, n_k, tq_, tk_,
                                                            bounded)
            return lax.cond(score_bound <= MAX_SCORE_BOUND, run(True), run(False))

        y_d = global_attention(0, n_lat, 0, t_all, tq, tk)
        tables = _s5_tables(ssm_a_re[l], ssm_a_im[l], ssm_log_dt[l], ssm_b_re[l], ssm_b_im[l],
                            ssm_c_re[l], ssm_c_im[l])
        y_s5 = _s5_scan(u, tables, n_lat, n_ctx)
        n_rows, y_d_ctx = n_lat, y_d
        if not last:
            y_d_ctx = global_attention(n_lat, n_ctx, n_lat, n_ctx, n_ctx, n_ctx)
            n_rows = t_all
        xs = _merge(xs, mod, l, row_of, norm_g[l, 1], xa, y_b, y_s5, u, y_d, y_d_ctx, w_in_l,
                    bf(_pool_block_diag(pool_w[l])), pool_scale[l], ssm_d[l], bf(glu_w[l]), bf(branch_w[l]),
                    bf(out_w[l]), n_rows, n_lat, n_ctx)
        xs = _ffn(xs, mod, l, 2, row_of if not last else lat_row, norm_g[l, 2], bf(ffn_in[l, 1]),
                  bf(ffn_out[l, 1]), n_rows, final_g=final_g if last else None)
    return xs
```

```python
import functools
import math

import jax
import jax.numpy as jnp
import numpy as np
from jax import lax
from jax.experimental import pallas as pl
from jax.experimental.pallas import tpu as pltpu

F32 = jnp.float32
BF16 = jnp.bfloat16

D_MODEL = 1024
D_FF = 2816
N_SUB = 3
BRANCH_W = 256
HEAD_DIM = 64
GRID_W = 64
BLOCK = 128
EPS = 1e-6
ROPE_THETA = 10000.0
N_BRANCH = 4
POOL_WINDOWS = (2, 4, 8, 16)
SSM_GROUPS = 16
SSM_GROUP = 16
SSM_STATE = 64
N_STATE = SSM_GROUPS * SSM_STATE
PROJ_W = 6 * BRANCH_W
MOD_ROWS = 8
POOL_HALO = 8
SCAN_ROWS = 8
STATE_CHUNK = 128
WIN_BLOCKS = 2
NEG = -1e30
QK_SCALE = HEAD_DIM ** -0.5
LOG2E = math.log2(math.e)
V_ROWS = HEAD_DIM + 16
MAX_SCORE_BOUND = 40.0
TOKEN_TILE = 256
VMEM_LIMIT = 56 * 1024 * 1024


def _cparams(sem):
    return pltpu.CompilerParams(dimension_semantics=sem, vmem_limit_bytes=VMEM_LIMIT)


def _dot(a, b):
    return jnp.dot(a, b, preferred_element_type=F32)


def _dot_nt(a, b):
    return lax.dot_general(a, b, (((1,), (1,)), ((), ())), preferred_element_type=F32)


def _rms(x, g):
    ms = jnp.mean(x * x, axis=-1, keepdims=True)
    return x * lax.rsqrt(ms + EPS) * g


def _silu(x):
    return x * jax.nn.sigmoid(x)


def _const_spec(shape):
    nd = len(shape)
    return pl.BlockSpec(shape, lambda *_: (0,) * nd)


def _mod_kernel(s_ref, w_ref, b_ref, o_ref):
    s = _silu(s_ref[...])
    w = w_ref[0]
    s_hi = s.astype(BF16)
    s_lo = (s - s_hi.astype(F32)).astype(BF16)
    w_hi = w.astype(BF16)
    w_lo = (w - w_hi.astype(F32)).astype(BF16)
    o_ref[0] = _dot(s_hi, w_hi) + _dot(s_hi, w_lo) + _dot(s_lo, w_hi) + b_ref[0]


def _modulation(rows, w_mod, b_mod):
    depth, d, width = w_mod.shape
    tn = 1024
    out = pl.pallas_call(
        _mod_kernel,
        grid=(depth, width // tn),
        in_specs=[
            pl.BlockSpec((MOD_ROWS, d), lambda l, j: (0, 0)),
            pl.BlockSpec((1, d, tn), lambda l, j: (l, 0, j)),
            pl.BlockSpec((1, 1, tn), lambda l, j: (l, 0, j)),
        ],
        out_specs=pl.BlockSpec((1, MOD_ROWS, tn), lambda l, j: (l, 0, j)),
        out_shape=jax.ShapeDtypeStruct((depth, MOD_ROWS, width), F32),
        compiler_params=_cparams(("parallel", "parallel")),
        name="modulation",
    )(rows, w_mod, b_mod.reshape(depth, 1, width))
    return out.reshape(depth, MOD_ROWS, N_SUB * 3, 1, d).transpose(0, 2, 1, 3, 4)


def _mod_spec(layer, vec, row_of):
    return pl.BlockSpec((None, None, None, 1, D_MODEL), lambda b, t: (layer, vec, row_of(b, t), 0, 0))


def _tail_rows(main_ref, tail_ref, n_main_tiles):
    return jnp.where(pl.program_id(1) >= n_main_tiles, tail_ref[0], main_ref[0])


def _ffn_kernel(x_ref, lat_ref, ctx_ref, ng_ref, win_ref, wout_ref, *rest, sub, final, n_main_tiles, n_lat_tiles):
    o_ref = rest[-1]
    tiles = o_ref.shape[1] // TOKEN_TILE
    step = pl.program_id(1)
    for s in range(tiles):
        rows = slice(TOKEN_TILE * s, TOKEN_TILE * (s + 1))
        x = x_ref[0, rows] if n_main_tiles is None else _tail_rows(x_ref, rest[0], n_main_tiles)
        is_ctx = step * tiles + s >= n_lat_tiles
        shift, scale, gate = (jnp.where(is_ctx, ctx_ref[3 * sub + v], lat_ref[3 * sub + v]) for v in range(3))
        h = (_rms(x, ng_ref[...]) * (1.0 + scale) + shift).astype(BF16)
        gu = _dot(h, win_ref[...])
        a = (_silu(gu[:, :D_FF]) * gu[:, D_FF:]).astype(BF16)
        xn = x + (0.5 * gate) * _dot(a, wout_ref[...])
        if final:
            xn = _rms(xn, rest[0][...])
        o_ref[0, rows] = xn


def _tail_specs(tm, width, n_main_tiles):
    return (pl.BlockSpec((1, tm, width), lambda b, t: (b, jnp.minimum(t, n_main_tiles - 1), 0)),
            pl.BlockSpec((1, tm, width), lambda b, t: (b, jnp.maximum(t - n_main_tiles, 0), 0)))


def _ffn(x, mod, layer, sub, norm_g, w_in, w_out, n_rows, n_lat, final_g=None, tail=None):
    bsz = x.shape[0]
    final = final_g is not None
    assert not (final and tail is not None)
    n_tiles = n_rows // TOKEN_TILE
    per_step = 1 if tail is not None else _largest_divisor(n_tiles, (5, 4, 2, 1))
    tm = per_step * TOKEN_TILE
    n_main_tiles = None if tail is None else x.shape[1] // tm
    x_spec = pl.BlockSpec((1, tm, D_MODEL), lambda b, t: (b, t, 0))
    if tail is not None:
        x_spec, tail_spec = _tail_specs(tm, D_MODEL, n_main_tiles)
    vectors = lambda row_of: pl.BlockSpec((None, N_SUB * 3, None, 1, D_MODEL),
                                          lambda b, t: (layer, 0, row_of(b), 0, 0))
    in_specs = [
        x_spec,
        vectors(lambda b: b), vectors(lambda b: bsz),
        _const_spec((1, D_MODEL)),
        pl.BlockSpec((D_MODEL, 2 * D_FF), lambda b, t: (0, 0), pipeline_mode=pl.Buffered(1)),
        pl.BlockSpec((D_FF, D_MODEL), lambda b, t: (0, 0), pipeline_mode=pl.Buffered(1)),
    ]
    args = [x, mod, mod, norm_g.reshape(1, D_MODEL), w_in, w_out]
    if final:
        in_specs.append(_const_spec((1, D_MODEL)))
        args.append(final_g.reshape(1, D_MODEL))
    if tail is not None:
        in_specs.append(tail_spec)
        args.append(tail)
    return pl.pallas_call(
        functools.partial(_ffn_kernel, sub=sub, final=final, n_main_tiles=n_main_tiles,
                          n_lat_tiles=n_lat // TOKEN_TILE),
        grid=(bsz, n_tiles // per_step),
        in_specs=in_specs,
        out_specs=pl.BlockSpec((1, tm, D_MODEL), lambda b, t: (b, t, 0)),
        out_shape=jax.ShapeDtypeStruct((bsz, n_rows, D_MODEL), F32),
        compiler_params=_cparams(("parallel", "parallel")),
        name="ffn_final" if final else "ffn",
    )(*args)


def _rope(x, cos, sin, first_half):
    w = x.shape[-1]
    swapped = jnp.where(first_half, pltpu.roll(x, w - 16, 1), pltpu.roll(x, 16, 1))
    return x * cos + swapped * sin


def _head_rms(x, g, low_head):
    x2 = x * x
    lo = jnp.sum(jnp.where(low_head, x2, 0.0), axis=-1, keepdims=True)
    hi = jnp.sum(jnp.where(low_head, 0.0, x2), axis=-1, keepdims=True)
    ms = jnp.where(low_head, lo, hi) * (1.0 / HEAD_DIM)
    return x * lax.rsqrt(ms + EPS) * g


def _store_padded(ref, x, low_head):
    sw = pltpu.roll(x, HEAD_DIM, 1)
    zero = jnp.zeros_like(x)
    ref[0, 0, 0] = jnp.where(low_head, x, zero).astype(ref.dtype)
    ref[0, 0, 1] = jnp.where(low_head, zero, sw).astype(ref.dtype)
    ref[0, 1, 0] = jnp.where(low_head, sw, zero).astype(ref.dtype)
    ref[0, 1, 1] = jnp.where(low_head, zero, x).astype(ref.dtype)


def _proj_kernel(shift_ref, x_ref, sh_ref, sc_ref, ng_ref, w_ref, cos_ref, sin_ref, qg_ref, kg_ref,
                 qw_ref, qt_ref, kw_ref, vw_ref, ka_ref, vt_ref, u_ref, xa_ref):
    x = x_ref[0]
    tm = x.shape[0]
    h = (_rms(x, ng_ref[...]) * (1.0 + sc_ref[...]) + sh_ref[...]).astype(BF16)
    p = _dot(h, w_ref[...])
    lane = lax.broadcasted_iota(jnp.int32, (tm, 2 * HEAD_DIM), 1)
    low_head = lane < HEAD_DIM
    first_half = jnp.bitwise_and(lane, 31) < 16
    cos = cos_ref[...]
    sin = sin_ref[...]
    rope = lambda z: _rope(z, cos, sin, first_half)
    k_win = rope(p[:, 0:128])
    v_win = p[:, 128:256]
    u_ref[0] = p[:, 256:512]
    k_glb = rope(_head_rms(p[:, 512:640], kg_ref[...], low_head))
    v_glb = p[:, 640:768]
    zero = jnp.zeros_like(k_glb)
    ka_ref[0, 0] = jnp.where(low_head, k_glb, jnp.where(lane == HEAD_DIM, 1.0, zero)).astype(BF16)
    ka_ref[0, 1] = jnp.where(low_head, jnp.where(lane == 0, 1.0, zero), k_glb).astype(BF16)
    first_row = lax.broadcasted_iota(jnp.int32, (HEAD_DIM, tm), 0) == 0
    shift_rows = jnp.where(first_row, shift_ref[0], 0.0)
    for c in range(2):
        qw = rope(p[:, 768 + 128 * c:896 + 128 * c]) * QK_SCALE
        qw_ref[0, :, 128 * c:128 * (c + 1)] = qw.astype(BF16)
        qd = rope(_head_rms(p[:, 1024 + 128 * c:1152 + 128 * c], qg_ref[...], low_head)) * (QK_SCALE * LOG2E)
        q_t = qd.T
        for g in range(2):
            own = q_t[HEAD_DIM * g:HEAD_DIM * (g + 1)]
            pair = [own, shift_rows] if c == 0 else [shift_rows, own]
            qt_ref[0, 2 * c + g] = jnp.concatenate(pair, axis=0).astype(BF16)
    v_t = v_glb.T
    ones_rows = jnp.where(lax.broadcasted_iota(jnp.int32, (V_ROWS - HEAD_DIM, tm), 0) == 0, 1.0, 0.0)
    for j in range(2):
        vt_ref[0, j] = jnp.concatenate([v_t[HEAD_DIM * j:HEAD_DIM * (j + 1)], ones_rows], axis=0).astype(BF16)
    xa_ref[0] = p[:, 1280:1536]
    _store_padded(kw_ref, k_win, low_head)
    _store_padded(vw_ref, v_win, low_head)


def _proj(x, mod, layer, row_of, norm_g, w, cos, sin, q_g, k_g, score_bound):
    bsz, t_all, _ = x.shape
    tm = TOKEN_TILE
    tok = lambda width: pl.BlockSpec((1, tm, width), lambda b, t: (b, t, 0))
    pad = pl.BlockSpec((1, 2, 2, tm, 128), lambda b, t: (b, 0, 0, t, 0))
    tab = pl.BlockSpec((tm, 128), lambda b, t: (t, 0))
    pad_shape = jax.ShapeDtypeStruct((bsz, 2, 2, t_all, 128), BF16)
    return pl.pallas_call(
        _proj_kernel,
        grid=(bsz, t_all // tm),
        in_specs=[
            pl.BlockSpec(memory_space=pltpu.SMEM),
            tok(D_MODEL),
            _mod_spec(layer, 3, row_of),
            _mod_spec(layer, 4, row_of),
            _const_spec((1, D_MODEL)),
            pl.BlockSpec((D_MODEL, PROJ_W), lambda b, t: (0, 0), pipeline_mode=pl.Buffered(1)),
            tab, tab,
            _const_spec((1, 128)), _const_spec((1, 128)),
        ],
        out_specs=[
            tok(256),
            pl.BlockSpec((1, 4, 128, tm), lambda b, t: (b, 0, 0, t)),
            pad, pad,
            pl.BlockSpec((1, 2, tm, 128), lambda b, t: (b, 0, t, 0)),
            pl.BlockSpec((1, 2, V_ROWS, tm), lambda b, t: (b, 0, 0, t)),
            tok(256), tok(256),
        ],
        out_shape=[
            jax.ShapeDtypeStruct((bsz, t_all, 256), BF16),
            jax.ShapeDtypeStruct((bsz, 4, 128, t_all), BF16),
            pad_shape, pad_shape,
            jax.ShapeDtypeStruct((bsz, 2, t_all, 128), BF16),
            jax.ShapeDtypeStruct((bsz, 2, V_ROWS, t_all), BF16),
            jax.ShapeDtypeStruct((bsz, t_all, 256), F32),
            jax.ShapeDtypeStruct((bsz, t_all, 256), F32),
        ],
        compiler_params=_cparams(("parallel", "parallel")),
        name="proj",
    )((-score_bound * LOG2E).reshape(1), x, mod, mod, norm_g.reshape(1, D_MODEL), w, cos, sin,
      jnp.tile(q_g, 2).reshape(1, 128), jnp.tile(k_g, 2).reshape(1, 128))


def _win_kernel(sink_ref, q_ref, kp_ref, kc_ref, kn_ref, kx_ref, vp_ref, vc_ref, vn_ref, vx_ref,
                o_ref, *, n_lat_blocks):
    i = pl.program_id(1)
    tile = WIN_BLOCKS * BLOCK
    band = tile + 2 * BLOCK
    r = lax.broadcasted_iota(jnp.int32, (tile, band), 0)
    c = lax.broadcasted_iota(jnp.int32, (tile, band), 1)
    blk0 = i * WIN_BLOCKS
    lo = jnp.where(blk0 >= 1, 0, BLOCK)
    hi = jnp.where(blk0 < n_lat_blocks, jnp.minimum((n_lat_blocks - blk0 + 1) * BLOCK, band), 0)
    in_window = (c - r).astype(jnp.uint32) <= 2 * BLOCK
    in_range = (c - lo).astype(jnp.uint32) < (hi - lo).astype(jnp.uint32)
    bias = jnp.where(in_window, jnp.where(in_range, 0.0, NEG), NEG)
    for j in range(2):
        q = q_ref[0, :, 128 * j:128 * (j + 1)]
        acc = jnp.zeros((tile, 128), F32)
        for g in range(2):
            k_band = jnp.concatenate([kp_ref[0, j, g], kc_ref[0, j, g], kn_ref[0, j, g]], axis=0)
            s = jnp.concatenate([_dot_nt(q, k_band) + bias, _dot_nt(q, kx_ref[0, j, g])], axis=1)
            v = jnp.concatenate([vp_ref[0, j, g], vc_ref[0, j, g], vn_ref[0, j, g], vx_ref[0, j, g]], axis=0)
            sink = sink_ref[2 * j + g]
            m = jnp.maximum(s.max(-1, keepdims=True), sink)
            p = jnp.exp(s - m)
            den = p.sum(-1, keepdims=True) + jnp.exp(sink - m)
            acc = acc + _dot(p.astype(BF16), v) / den
        o_ref[0, :, 128 * j:128 * (j + 1)] = acc.astype(o_ref.dtype)


def _window_attention(q, k_pad, v_pad, sink, n_lat, n_ctx, with_ctx):
    bsz, t_all, _ = q.shape
    n_lat_blocks = n_lat // BLOCK
    n_blocks = t_all // BLOCK
    n_q_blocks = n_blocks if with_ctx else n_lat_blocks
    assert n_q_blocks % WIN_BLOCKS == 0 and n_lat_blocks % WIN_BLOCKS == 0
    tile = WIN_BLOCKS * BLOCK
    edge = lambda off: pl.BlockSpec(
        (1, 2, 2, BLOCK, 128), lambda b, i: (b, 0, 0, jnp.clip(i * WIN_BLOCKS + off, 0, n_blocks - 1), 0))
    cur = pl.BlockSpec((1, 2, 2, tile, 128), lambda b, i: (b, 0, 0, i, 0))
    ctx = pl.BlockSpec((1, 2, 2, n_ctx, 128), lambda b, i: (b, 0, 0, n_lat // n_ctx, 0))
    return pl.pallas_call(
        functools.partial(_win_kernel, n_lat_blocks=n_lat_blocks),
        grid=(bsz, n_q_blocks // WIN_BLOCKS),
        in_specs=[
            pl.BlockSpec(memory_space=pltpu.SMEM),
            pl.BlockSpec((1, tile, 256), lambda b, i: (b, i, 0)),
            edge(-1), cur, edge(WIN_BLOCKS), ctx,
            edge(-1), cur, edge(WIN_BLOCKS), ctx,
        ],
        out_specs=pl.BlockSpec((1, tile, 256), lambda b, i: (b, i, 0)),
        out_shape=jax.ShapeDtypeStruct((bsz, n_q_blocks * BLOCK, 256), BF16),
        compiler_params=_cparams(("parallel", "parallel")),
        name="window_attention",
    )(sink, q, k_pad, k_pad, k_pad, k_pad, v_pad, v_pad, v_pad, v_pad)


def _glb_finish(acc_ref, o_ref):
    o_t = jnp.concatenate([acc_ref[g, 0:HEAD_DIM] / acc_ref[g, HEAD_DIM:HEAD_DIM + 1] for g in range(2)],
                          axis=0)
    o_ref[0] = o_t.T.astype(o_ref.dtype)


def _glb_bounded_kernel(qt_ref, k_ref, vt_ref, o_ref, acc_ref, *, tk):
    for ki in range(k_ref.shape[2] // tk):
        keys = slice(ki * tk, (ki + 1) * tk)
        for g in range(2):
            p = jnp.exp2(_dot(k_ref[0, 0, keys], qt_ref[0, g])).astype(BF16)
            pv = _dot(vt_ref[0, 0, :, keys], p)
            acc_ref[g] = pv if ki == 0 else acc_ref[g] + pv
    _glb_finish(acc_ref, o_ref)


def _glb_online_kernel(qt_ref, k_ref, vt_ref, o_ref, acc_ref, m_ref):
    ki = pl.program_id(3)

    @pl.when(ki == 0)
    def _():
        acc_ref[...] = jnp.zeros_like(acc_ref)
        m_ref[...] = jnp.full_like(m_ref, NEG)

    for g in range(2):
        s = _dot(k_ref[0, 0], qt_ref[0, g])
        m_prev = m_ref[g, 0:1]
        m_new = jnp.maximum(m_prev, s.max(axis=0, keepdims=True))
        p = jnp.exp2(s - m_new).astype(BF16)
        acc_ref[g] = jnp.exp2(m_prev - m_new) * acc_ref[g] + _dot(vt_ref[0, 0], p)
        m_ref[g] = jnp.broadcast_to(m_new, m_ref.shape[1:])

    @pl.when(ki == pl.num_programs(3) - 1)
    def _():
        _glb_finish(acc_ref, o_ref)


def _global_attention(qt, k_aug, vt_aug, q_row0, n_q, k_row0, n_k, tq, tk, bounded):
    bsz = qt.shape[0]
    q0, k0 = q_row0 // tq, k_row0 // tk
    scratch = [pltpu.VMEM((2, V_ROWS, tq), F32)]
    if bounded:
        once = dict(pipeline_mode=pl.Buffered(1))
        return pl.pallas_call(
            functools.partial(_glb_bounded_kernel, tk=tk),
            grid=(bsz, 2, n_q // tq),
            in_specs=[
                pl.BlockSpec((1, 2, 128, tq), lambda b, j, qi: (b, j, 0, q0 + qi)),
                pl.BlockSpec((1, 1, n_k, 128), lambda b, j, qi: (b, j, k_row0 // n_k, 0), **once),
                pl.BlockSpec((1, 1, V_ROWS, n_k), lambda b, j, qi: (b, j, 0, k_row0 // n_k), **once),
            ],
            out_specs=pl.BlockSpec((1, tq, 128), lambda b, j, qi: (b, qi, j)),
            out_shape=jax.ShapeDtypeStruct((bsz, n_q, 256), BF16),
            scratch_shapes=scratch,
            compiler_params=_cparams(("parallel", "parallel", "parallel")),
            name="global_attention",
        )(qt, k_aug, vt_aug)
    scratch.append(pltpu.VMEM((2, 8, tq), F32))
    return pl.pallas_call(
        _glb_online_kernel,
        grid=(bsz, 2, n_q // tq, n_k // tk),
        in_specs=[
            pl.BlockSpec((1, 2, 128, tq), lambda b, j, qi, ki: (b, j, 0, q0 + qi)),
            pl.BlockSpec((1, 1, tk, 128), lambda b, j, qi, ki: (b, j, k0 + ki, 0)),
            pl.BlockSpec((1, 1, V_ROWS, tk), lambda b, j, qi, ki: (b, j, 0, k0 + ki)),
        ],
        out_specs=pl.BlockSpec((1, tq, 128), lambda b, j, qi, ki: (b, qi, j)),
        out_shape=jax.ShapeDtypeStruct((bsz, n_q, 256), BF16),
        scratch_shapes=scratch,
        compiler_params=_cparams(("parallel", "parallel", "parallel", "arbitrary")),
        name="global_attention_online",
    )(qt, k_aug, vt_aug)


def _s5_tile(u, bmat_ref, cmat_ref, a_ref, pc_ref, carry_ref, reverse):
    rows = u.shape[0]
    n_groups = rows // SCAN_ROWS
    w = STATE_CHUNK
    ub = u.astype(BF16)
    order = range(n_groups - 1, -1, -1) if reverse else range(n_groups)
    last = 0 if reverse else SCAN_ROWS - 1
    y = None
    for c in range(N_STATE // w):
        bu = _dot(ub, bmat_ref[0, c])
        pr, pi = pc_ref[0, c, :, :w], pc_ref[0, c, :, w:]
        cr, ci = carry_ref[c, :, :w], carry_ref[c, :, w:]
        states = [None] * n_groups
        for g in order:
            vr = bu[SCAN_ROWS * g:SCAN_ROWS * (g + 1), :w]
            vi = bu[SCAN_ROWS * g:SCAN_ROWS * (g + 1), w:]
            for s, k in enumerate((1, 2, 4)):
                ar, ai = a_ref[0, s, c, :, :w], a_ref[0, s, c, :, w:]
                shift = SCAN_ROWS - k if reverse else k
                sr = pltpu.roll(vr, shift, 0)
                si = pltpu.roll(vi, shift, 0)
                vr, vi = vr + ar * sr - ai * si, vi + ar * si + ai * sr
            vr, vi = vr + pr * cr - pi * ci, vi + pr * ci + pi * cr
            cr = jnp.broadcast_to(vr[last:last + 1], (SCAN_ROWS, w))
            ci = jnp.broadcast_to(vi[last:last + 1], (SCAN_ROWS, w))
            states[g] = jnp.concatenate([vr, vi], axis=1)
        carry_ref[c, :, :w] = cr
        carry_ref[c, :, w:] = ci
        part = _dot(jnp.concatenate(states, axis=0).astype(BF16), cmat_ref[0, c])
        y = part if y is None else y + part
    return y


def _s5_kernel(u_ref, bmat_ref, cmat_ref, a_ref, pc_ref, y_ref, carry_ref):
    direction = pl.program_id(0)
    step = pl.program_id(2)

    @pl.when(step == 0)
    def _():
        carry_ref[...] = jnp.zeros_like(carry_ref)

    for reverse in (False, True):
        @pl.when(direction == int(reverse))
        def _(reverse=reverse):
            y_ref[0, 0] = _s5_tile(u_ref[0], bmat_ref, cmat_ref, a_ref, pc_ref, carry_ref, reverse)


def _s5_tables(a_re, a_im, log_dt, b_re, b_im, c_re, c_im):
    dt = jnp.exp(log_dt)[..., None]
    re_dt, im_dt = a_re * dt, a_im * dt
    lb_re, lb_im = jnp.exp(re_dt) * jnp.cos(im_dt), jnp.exp(re_dt) * jnp.sin(im_dt)
    den = a_re * a_re + a_im * a_im
    f_re = ((lb_re - 1.0) * a_re + lb_im * a_im) / den
    f_im = (lb_im * a_re - (lb_re - 1.0) * a_im) / den
    bb_re = f_re[..., None] * b_re - f_im[..., None] * b_im
    bb_im = f_re[..., None] * b_im + f_im[..., None] * b_re
    eye = jnp.eye(SSM_GROUPS, dtype=F32)
    in_w, st_w, n_chunks = SSM_GROUPS * SSM_GROUP, N_STATE, N_STATE // STATE_CHUNK
    b_blk = lambda z: jnp.einsum('dgph,gk->dghkp', z, eye).reshape(2, in_w, n_chunks, STATE_CHUNK)
    bmat = jnp.concatenate([b_blk(bb_re), b_blk(bb_im)], axis=-1).transpose(0, 2, 1, 3).astype(BF16)
    c_blk = lambda z: jnp.einsum('dghp,gk->dgpkh', z, eye).reshape(2, n_chunks, STATE_CHUNK, in_w)
    cmat = jnp.concatenate([c_blk(c_re), -c_blk(c_im)], axis=2).astype(BF16)
    steps = jnp.arange(1, SCAN_ROWS + 1, dtype=F32)[None, :, None, None]
    mag = jnp.exp(re_dt[:, None] * steps)
    ang = im_dt[:, None] * steps
    chunked = lambda z: z.reshape(2, SCAN_ROWS, n_chunks, STATE_CHUNK)
    powers = jnp.concatenate([chunked(mag * jnp.cos(ang)), chunked(mag * jnp.sin(ang))], axis=-1)
    row = jnp.arange(SCAN_ROWS)[:, None, None]
    pc = jnp.stack([powers[0], powers[1][::-1]]).transpose(0, 2, 1, 3)
    a_tabs = []
    for k in (1, 2, 4):
        fwd = jnp.where(row >= k, powers[0, k - 1][None], 0.0)
        bwd = jnp.where(row <= SCAN_ROWS - 1 - k, powers[1, k - 1][None], 0.0)
        a_tabs.append(jnp.stack([fwd, bwd]).transpose(0, 2, 1, 3))
    return bmat, cmat, jnp.stack(a_tabs, axis=1), pc


def _s5_scan(u, tables, n_lat, n_ctx):
    bsz, t_all, width = u.shape
    tm = TOKEN_TILE
    assert n_ctx == tm
    nt = n_lat // tm
    bmat, cmat, a_tab, pc = tables
    n_chunks, lanes = N_STATE // STATE_CHUNK, 2 * STATE_CHUNK
    tile_of = lambda d, s: jnp.where(s == 0, nt, jnp.where(d == 0, s - 1, nt - s))
    return pl.pallas_call(
        _s5_kernel,
        grid=(2, bsz, nt + 1),
        in_specs=[
            pl.BlockSpec((1, tm, width), lambda d, b, s: (b, tile_of(d, s), 0)),
            pl.BlockSpec((1, n_chunks, width, lanes), lambda d, b, s: (d, 0, 0, 0)),
            pl.BlockSpec((1, n_chunks, lanes, width), lambda d, b, s: (d, 0, 0, 0)),
            pl.BlockSpec((1, 3, n_chunks, SCAN_ROWS, lanes), lambda d, b, s: (d, 0, 0, 0, 0)),
            pl.BlockSpec((1, n_chunks, SCAN_ROWS, lanes), lambda d, b, s: (d, 0, 0, 0)),
        ],
        out_specs=pl.BlockSpec((1, 1, tm, width), lambda d, b, s: (d, b, tile_of(d, s), 0)),
        out_shape=jax.ShapeDtypeStruct((2, bsz, t_all, width), F32),
        scratch_shapes=[pltpu.VMEM((n_chunks, SCAN_ROWS, lanes), F32)],
        compiler_params=_cparams(("parallel", "parallel", "arbitrary")),
        name="s5_scan",
    )(u, bmat, cmat, a_tab, pc)


def _gelu_tanh(x):
    c = math.sqrt(2.0 / math.pi)
    return x * (0.5 * (1.0 + jnp.tanh(c * (x + 0.044715 * (x * x * x)))))


def _merge_kernel(x_ref, sh_ref, sc_ref, gt_ref, ng_ref, xa_ref, xap_ref, xan_ref, yb_ref, yf_ref, yr_ref,
                  u_ref, yd_ref, ydt_ref, wg_ref, pw_ref, ps_ref, dsk_ref, glu_ref, wb_ref, wo_ref, o_ref, ext_ref,
                  *, n_lat_tiles, n_lat, n_ctx):
    t = pl.program_id(1)
    x = x_ref[0]
    tm = x.shape[0]
    bw = BRANCH_W
    h = (_rms(x, ng_ref[...]) * (1.0 + sc_ref[...]) + sh_ref[...]).astype(BF16)
    gates = [jax.nn.sigmoid(_dot(h, wg_ref[:, PROJ_W + D_MODEL * k:PROJ_W + D_MODEL * (k + 1)]))
             for k in range(N_BRANCH)]
    is_ctx = t >= n_lat_tiles
    seg_tile = jnp.where(is_ctx, t - n_lat_tiles, t)
    seg_tiles = jnp.where(is_ctx, n_ctx // tm, n_lat_tiles)
    seg_len = jnp.where(is_ctx, n_ctx, n_lat)
    xa = xa_ref[0]
    halo, span = POOL_HALO, tm + 2 * POOL_HALO
    for k in range(4):
        ext_ref[k, 0:halo, :] = jnp.zeros((halo, bw), F32)
        ext_ref[k, halo + span:, :] = jnp.zeros((halo, bw), F32)
    ext_ref[0, halo:2 * halo, :] = jnp.where(seg_tile > 0, xap_ref[0], 0.0)
    ext_ref[0, 2 * halo:2 * halo + tm, :] = xa
    ext_ref[0, 2 * halo + tm:halo + span, :] = jnp.where(seg_tile + 1 < seg_tiles, xan_ref[0], 0.0)
    wide = lambda k, off: ext_ref[k, halo + off:halo + off + span, :]
    ext_ref[1, halo:halo + span, :] = wide(0, -1) + wide(0, 0)
    ext_ref[2, halo:halo + span, :] = wide(1, -1) + wide(1, 1)
    ext_ref[3, halo:halo + span, :] = wide(2, -2) + wide(2, 2)
    own = lambda k, off: ext_ref[k, 2 * halo + off:2 * halo + off + tm, :]
    sums = [own(1, 0), own(2, 0), own(3, 0), own(3, -4) + own(3, 4)]
    lane = lax.broadcasted_iota(jnp.int32, (tm, bw), 1)
    pos = seg_tile * tm + lax.broadcasted_iota(jnp.int32, (tm, bw), 0)
    group = lane // (bw // len(POOL_WINDOWS))
    half_w = jnp.left_shift(1, group)
    count = jnp.minimum(pos + half_w, seg_len) - jnp.maximum(pos - half_w, 0)
    window_sum = jnp.where(group == 0, sums[0], jnp.where(group == 1, sums[1],
                                                           jnp.where(group == 2, sums[2], sums[3])))
    diff = window_sum / count.astype(F32) - xa
    y_a = _dot(diff.astype(BF16), pw_ref[...]) * ps_ref[...]
    y_s = yf_ref[0, 0] + yr_ref[0, 0] + dsk_ref[...] * u_ref[0]
    z = _dot(_gelu_tanh(y_s).astype(BF16), glu_ref[...])
    y_c = z[:, :bw] * jax.nn.sigmoid(z[:, bw:])
    branches = (y_a.astype(BF16), yb_ref[0], y_c.astype(BF16), _tail_rows(yd_ref, ydt_ref, n_lat_tiles))
    total = None
    for gate, y_k, k in zip(gates, branches, range(N_BRANCH)):
        term = gate * _dot(y_k, wb_ref[k])
        total = term if total is None else total + term
    o_ref[0] = x + gt_ref[...] * _dot(total.astype(BF16), wo_ref[...])


def _merge(x, mod, layer, row_of, norm_g, xa, y_b, y_s5, u, y_d, y_d_tail, w_gate, pool_bd, pool_scale, d_skip,
           glu_w, branch_w, out_w, n_rows, n_lat, n_ctx):
    bsz = x.shape[0]
    tm = TOKEN_TILE
    assert n_ctx % tm == 0 or n_rows == n_lat
    halo_blocks = xa.shape[1] // POOL_HALO
    per_tile = tm // POOL_HALO
    tok = lambda width: pl.BlockSpec((1, tm, width), lambda b, t: (b, t, 0))
    single = dict(pipeline_mode=pl.Buffered(1))
    return pl.pallas_call(
        functools.partial(_merge_kernel, n_lat_tiles=n_lat // tm, n_lat=n_lat, n_ctx=n_ctx),
        grid=(bsz, n_rows // tm),
        in_specs=[
            tok(D_MODEL),
            _mod_spec(layer, 3, row_of), _mod_spec(layer, 4, row_of), _mod_spec(layer, 5, row_of),
            _const_spec((1, D_MODEL)),
            tok(256),
            pl.BlockSpec((1, POOL_HALO, 256), lambda b, t: (b, jnp.maximum(t * per_tile - 1, 0), 0)),
            pl.BlockSpec((1, POOL_HALO, 256),
                         lambda b, t: (b, jnp.minimum((t + 1) * per_tile, halo_blocks - 1), 0)),
            tok(256),
            pl.BlockSpec((1, 1, tm, 256), lambda b, t: (0, b, t, 0)),
            pl.BlockSpec((1, 1, tm, 256), lambda b, t: (1, b, t, 0)),
            tok(256), *_tail_specs(tm, 256, n_lat // tm),
            pl.BlockSpec((D_MODEL, PROJ_W + N_BRANCH * D_MODEL), lambda b, t: (0, 0), **single),
            _const_spec((256, 256)), _const_spec((1, 256)), _const_spec((1, 256)),
            _const_spec((256, 512)),
            pl.BlockSpec((N_BRANCH, 256, D_MODEL), lambda b, t: (0, 0, 0), **single),
            pl.BlockSpec((D_MODEL, D_MODEL), lambda b, t: (0, 0), **single),
        ],
        out_specs=tok(D_MODEL),
        out_shape=jax.ShapeDtypeStruct((bsz, n_rows, D_MODEL), F32),
        scratch_shapes=[pltpu.VMEM((4, tm + 4 * POOL_HALO, 256), F32)],
        compiler_params=_cparams(("parallel", "parallel")),
        name="merge",
    )(x, mod, mod, mod, norm_g.reshape(1, D_MODEL), xa, xa, xa, y_b, y_s5, y_s5, u, y_d, y_d_tail,
      w_gate, pool_bd, pool_scale.reshape(1, 256), d_skip.reshape(1, 256), glu_w, branch_w, out_w)


def _rope_tables(n_lat, n_ctx):
    rows = n_lat // GRID_W
    n_freq = HEAD_DIM // 4
    row = jnp.repeat(jnp.arange(rows), GRID_W)
    col = jnp.tile(jnp.arange(GRID_W), rows)
    inv = ROPE_THETA ** (-jnp.arange(n_freq, dtype=F32) / n_freq)
    ang = jnp.stack([row, col], axis=-1).astype(F32)[..., None] * inv
    cos = jnp.broadcast_to(jnp.cos(ang)[:, :, None, :], (n_lat, 2, 2, n_freq)).reshape(n_lat, HEAD_DIM)
    sign = jnp.array([-1.0, 1.0], F32)[None, None, :, None]
    sin = (jnp.sin(ang)[:, :, None, :] * sign).reshape(n_lat, HEAD_DIM)
    cos = jnp.concatenate([cos, jnp.ones((n_ctx, HEAD_DIM), F32)], axis=0)
    sin = jnp.concatenate([sin, jnp.zeros((n_ctx, HEAD_DIM), F32)], axis=0)
    return jnp.tile(cos, (1, 2)), jnp.tile(sin, (1, 2))


def _largest_divisor(n, candidates):
    for c in candidates:
        if n % c == 0:
            return c
    raise ValueError(f"no tile in {candidates} divides {n}")


def _pool_block_diag(pool_w):
    groups, width, _ = pool_w.shape
    eye = jnp.eye(groups, dtype=pool_w.dtype)
    return jnp.einsum('gcd,gk->gckd', pool_w, eye).reshape(groups * width, groups * width)


def kernel(x, c, ctx, c_ctx, w_mod, b_mod, norm_g, ffn_in, ffn_out, w_in, win_sink, qk_norm, pool_w, pool_scale, ssm_a_re, ssm_a_im, ssm_log_dt, ssm_b_re, ssm_b_im, ssm_c_re, ssm_c_im, ssm_d, glu_w, branch_w, out_w, final_g):
    bsz, n_lat, d = x.shape
    n_ctx = ctx.shape[1]
    depth = w_mod.shape[0]
    t_all = n_lat + n_ctx
    assert d == D_MODEL and bsz < MOD_ROWS and n_lat % TOKEN_TILE == 0 and n_ctx == TOKEN_TILE
    assert n_lat % GRID_W == 0

    cond_rows = jnp.zeros((MOD_ROWS, d), F32).at[:bsz].set(c).at[bsz].set(c_ctx)
    mod = _modulation(cond_rows, w_mod, b_mod)
    n_lat_tiles = n_lat // TOKEN_TILE
    row_of = lambda b, t: jnp.where(t >= n_lat_tiles, bsz, b)

    cos, sin = _rope_tables(n_lat, n_ctx)
    tq = _largest_divisor(n_lat, (1024, 512, 256))
    tk = _largest_divisor(t_all, (3328, 1280, 1024, 768, 512, 256))

    xs = x
    for l in range(depth):
        last = l == depth - 1
        bf = lambda w: w.astype(BF16)
        xs = _ffn(xs, mod, l, 0, norm_g[l, 0], bf(ffn_in[l, 0]), bf(ffn_out[l, 0]), t_all, n_lat,
                  tail=ctx if l == 0 else None)

        score_bound = 1.02 * HEAD_DIM ** 0.5 * jnp.max(jnp.abs(qk_norm[l, 0])) * jnp.max(jnp.abs(qk_norm[l, 1]))
        w_in_l = bf(w_in[l])
        qw, qt, kw, vw, ka, vt, u, xa = _proj(xs, mod, l, row_of, norm_g[l, 1], w_in_l,
                                              cos, sin, qk_norm[l, 0], qk_norm[l, 1], score_bound)
        y_b = _window_attention(qw, kw, vw, win_sink[l], n_lat, n_ctx, not last)

        def global_attention(q_row0, n_q, k_row0, n_k, tq_, tk_):
            run = lambda bounded: lambda: _global_attention(qt, ka, vt, q_row0, n_q, k_row0, n_k, tq_, tk_,
                                                            bounded)
            return lax.cond(score_bound <= MAX_SCORE_BOUND, run(True), run(False))

        y_d = global_attention(0, n_lat, 0, t_all, tq, tk)
        tables = _s5_tables(ssm_a_re[l], ssm_a_im[l], ssm_log_dt[l], ssm_b_re[l], ssm_b_im[l],
                            ssm_c_re[l], ssm_c_im[l])
        y_s5 = _s5_scan(u, tables, n_lat, n_ctx)
        n_rows, y_d_ctx = n_lat, y_d
        if not last:
            y_d_ctx = global_attention(n_lat, n_ctx, n_lat, n_ctx, n_ctx, n_ctx)
            n_rows = t_all
        xs = _merge(xs, mod, l, row_of, norm_g[l, 1], xa, y_b, y_s5, u, y_d, y_d_ctx, w_in_l,
                    bf(_pool_block_diag(pool_w[l])), pool_scale[l], ssm_d[l], bf(glu_w[l]), bf(branch_w[l]),
                    bf(out_w[l]), n_rows, n_lat, n_ctx)
        xs = _ffn(xs, mod, l, 2, norm_g[l, 2], bf(ffn_in[l, 1]), bf(ffn_out[l, 1]), n_rows, n_lat,
                  final_g=final_g if last else None)
    return xs
```

```python
import functools
import math

import jax
import jax.numpy as jnp
import numpy as np
from jax import lax
from jax.experimental import pallas as pl
from jax.experimental.pallas import tpu as pltpu

F32 = jnp.float32
BF16 = jnp.bfloat16

D_MODEL = 1024
D_FF = 2816
N_SUB = 3
BRANCH_W = 256
HEAD_DIM = 64
GRID_W = 64
BLOCK = 128
EPS = 1e-6
ROPE_THETA = 10000.0
N_BRANCH = 4
POOL_WINDOWS = (2, 4, 8, 16)
SSM_GROUPS = 16
SSM_GROUP = 16
SSM_STATE = 64
N_STATE = SSM_GROUPS * SSM_STATE
PROJ_W = 6 * BRANCH_W
MOD_ROWS = 8
POOL_HALO = 8
SCAN_ROWS = 8
STATE_CHUNK = 128
WIN_BLOCKS = 2
NEG = -1e30
QK_SCALE = HEAD_DIM ** -0.5
LOG2E = math.log2(math.e)
V_ROWS = HEAD_DIM + 16
MAX_SCORE_BOUND = 40.0
TOKEN_TILE = 256
VMEM_LIMIT = 56 * 1024 * 1024


def _cparams(sem):
    return pltpu.CompilerParams(dimension_semantics=sem, vmem_limit_bytes=VMEM_LIMIT)


def _dot(a, b):
    return jnp.dot(a, b, preferred_element_type=F32)


def _dot_nt(a, b):
    return lax.dot_general(a, b, (((1,), (1,)), ((), ())), preferred_element_type=F32)


def _rms(x, g):
    ms = jnp.mean(x * x, axis=-1, keepdims=True)
    return x * lax.rsqrt(ms + EPS) * g


def _silu(x):
    return x * jax.nn.sigmoid(x)


def _const_spec(shape):
    nd = len(shape)
    return pl.BlockSpec(shape, lambda *_: (0,) * nd)


def _mod_kernel(s_ref, w_ref, b_ref, o_ref):
    s = _silu(s_ref[...])
    w = w_ref[0]
    s_hi = s.astype(BF16)
    s_lo = (s - s_hi.astype(F32)).astype(BF16)
    w_hi = w.astype(BF16)
    w_lo = (w - w_hi.astype(F32)).astype(BF16)
    o_ref[0] = _dot(s_hi, w_hi) + _dot(s_hi, w_lo) + _dot(s_lo, w_hi) + b_ref[0]


def _modulation(rows, w_mod, b_mod):
    depth, d, width = w_mod.shape
    tn = 1024
    out = pl.pallas_call(
        _mod_kernel,
        grid=(depth, width // tn),
        in_specs=[
            pl.BlockSpec((MOD_ROWS, d), lambda l, j: (0, 0)),
            pl.BlockSpec((1, d, tn), lambda l, j: (l, 0, j)),
            pl.BlockSpec((1, 1, tn), lambda l, j: (l, 0, j)),
        ],
        out_specs=pl.BlockSpec((1, MOD_ROWS, tn), lambda l, j: (l, 0, j)),
        out_shape=jax.ShapeDtypeStruct((depth, MOD_ROWS, width), F32),
        compiler_params=_cparams(("parallel", "parallel")),
        name="modulation",
    )(rows, w_mod, b_mod.reshape(depth, 1, width))
    return out.reshape(depth, MOD_ROWS, N_SUB * 3, 1, d).transpose(0, 2, 1, 3, 4)


def _mod_spec(layer, vec, row_of):
    return pl.BlockSpec((None, None, None, 1, D_MODEL), lambda b, t: (layer, vec, row_of(b, t), 0, 0))


def _mod_rows(layer, bsz):
    spec = lambda row_of: pl.BlockSpec((None, N_SUB * 3, None, 1, D_MODEL),
                                       lambda b, t: (layer, 0, row_of(b), 0, 0))
    return [spec(lambda b: b), spec(lambda b: bsz)]


def _tail_rows(main_ref, tail_ref, n_main_tiles):
    return jnp.where(pl.program_id(1) >= n_main_tiles, tail_ref[0], main_ref[0])


def _ffn_kernel(x_ref, lat_ref, ctx_ref, ng_ref, win_ref, wout_ref, *rest, sub, final, n_main_tiles, n_lat_tiles):
    o_ref = rest[-1]
    tiles = o_ref.shape[1] // TOKEN_TILE
    step = pl.program_id(1)
    for s in range(tiles):
        rows = slice(TOKEN_TILE * s, TOKEN_TILE * (s + 1))
        x = x_ref[0, rows] if n_main_tiles is None else _tail_rows(x_ref, rest[0], n_main_tiles)
        is_ctx = step * tiles + s >= n_lat_tiles
        shift, scale, gate = (jnp.where(is_ctx, ctx_ref[3 * sub + v], lat_ref[3 * sub + v]) for v in range(3))
        h = (_rms(x, ng_ref[...]) * (1.0 + scale) + shift).astype(BF16)
        gu = _dot(h, win_ref[...])
        a = (_silu(gu[:, :D_FF]) * gu[:, D_FF:]).astype(BF16)
        xn = x + (0.5 * gate) * _dot(a, wout_ref[...])
        if final:
            xn = _rms(xn, rest[0][...])
        o_ref[0, rows] = xn


def _tail_specs(tm, width, n_main_tiles):
    return (pl.BlockSpec((1, tm, width), lambda b, t: (b, jnp.minimum(t, n_main_tiles - 1), 0)),
            pl.BlockSpec((1, tm, width), lambda b, t: (b, jnp.maximum(t - n_main_tiles, 0), 0)))


def _ffn(x, mod, layer, sub, norm_g, w_in, w_out, n_rows, n_lat, final_g=None, tail=None):
    bsz = x.shape[0]
    final = final_g is not None
    assert not (final and tail is not None)
    n_tiles = n_rows // TOKEN_TILE
    per_step = 1 if tail is not None else _largest_divisor(n_tiles, (5, 4, 2, 1))
    tm = per_step * TOKEN_TILE
    n_main_tiles = None if tail is None else x.shape[1] // tm
    x_spec = pl.BlockSpec((1, tm, D_MODEL), lambda b, t: (b, t, 0))
    if tail is not None:
        x_spec, tail_spec = _tail_specs(tm, D_MODEL, n_main_tiles)
    in_specs = [
        x_spec,
        *_mod_rows(layer, bsz),
        _const_spec((1, D_MODEL)),
        pl.BlockSpec((D_MODEL, 2 * D_FF), lambda b, t: (0, 0), pipeline_mode=pl.Buffered(1)),
        pl.BlockSpec((D_FF, D_MODEL), lambda b, t: (0, 0), pipeline_mode=pl.Buffered(1)),
    ]
    args = [x, mod, mod, norm_g.reshape(1, D_MODEL), w_in, w_out]
    if final:
        in_specs.append(_const_spec((1, D_MODEL)))
        args.append(final_g.reshape(1, D_MODEL))
    if tail is not None:
        in_specs.append(tail_spec)
        args.append(tail)
    return pl.pallas_call(
        functools.partial(_ffn_kernel, sub=sub, final=final, n_main_tiles=n_main_tiles,
                          n_lat_tiles=n_lat // TOKEN_TILE),
        grid=(bsz, n_tiles // per_step),
        in_specs=in_specs,
        out_specs=pl.BlockSpec((1, tm, D_MODEL), lambda b, t: (b, t, 0)),
        out_shape=jax.ShapeDtypeStruct((bsz, n_rows, D_MODEL), F32),
        compiler_params=_cparams(("parallel", "parallel")),
        name="ffn_final" if final else "ffn",
    )(*args)


def _rope(x, cos, sin, first_half):
    w = x.shape[-1]
    swapped = jnp.where(first_half, pltpu.roll(x, w - 16, 1), pltpu.roll(x, 16, 1))
    return x * cos + swapped * sin


def _head_rms(x, g, low_head):
    x2 = x * x
    lo = jnp.sum(jnp.where(low_head, x2, 0.0), axis=-1, keepdims=True)
    hi = jnp.sum(jnp.where(low_head, 0.0, x2), axis=-1, keepdims=True)
    ms = jnp.where(low_head, lo, hi) * (1.0 / HEAD_DIM)
    return x * lax.rsqrt(ms + EPS) * g


def _store_padded(ref, x, low_head, rows):
    sw = pltpu.roll(x, HEAD_DIM, 1)
    zero = jnp.zeros_like(x)
    ref[0, 0, 0, rows] = jnp.where(low_head, x, zero).astype(ref.dtype)
    ref[0, 0, 1, rows] = jnp.where(low_head, zero, sw).astype(ref.dtype)
    ref[0, 1, 0, rows] = jnp.where(low_head, sw, zero).astype(ref.dtype)
    ref[0, 1, 1, rows] = jnp.where(low_head, zero, x).astype(ref.dtype)


def _proj_kernel(shift_ref, x_ref, lat_ref, ctx_ref, ng_ref, w_ref, cos_ref, sin_ref, qg_ref, kg_ref,
                 qw_ref, qt_ref, kw_ref, vw_ref, ka_ref, vt_ref, u_ref, xa_ref, *, n_lat_tiles):
    tm = TOKEN_TILE
    tiles = x_ref.shape[1] // tm
    step = pl.program_id(1)
    lane = lax.broadcasted_iota(jnp.int32, (tm, 2 * HEAD_DIM), 1)
    low_head = lane < HEAD_DIM
    first_half = jnp.bitwise_and(lane, 31) < 16
    first_row = lax.broadcasted_iota(jnp.int32, (HEAD_DIM, tm), 0) == 0
    shift_rows = jnp.where(first_row, shift_ref[0], 0.0)
    ones_rows = jnp.where(lax.broadcasted_iota(jnp.int32, (V_ROWS - HEAD_DIM, tm), 0) == 0, 1.0, 0.0)
    for s in range(tiles):
        rows = slice(tm * s, tm * (s + 1))
        x = x_ref[0, rows]
        is_ctx = step * tiles + s >= n_lat_tiles
        shift, scale = (jnp.where(is_ctx, ctx_ref[3 + v], lat_ref[3 + v]) for v in range(2))
        h = (_rms(x, ng_ref[...]) * (1.0 + scale) + shift).astype(BF16)
        p = _dot(h, w_ref[...])
        cos = cos_ref[rows]
        sin = sin_ref[rows]
        rope = lambda z: _rope(z, cos, sin, first_half)
        k_win = rope(p[:, 0:128])
        v_win = p[:, 128:256]
        u_ref[0, rows] = p[:, 256:512]
        k_glb = rope(_head_rms(p[:, 512:640], kg_ref[...], low_head))
        v_glb = p[:, 640:768]
        zero = jnp.zeros_like(k_glb)
        ka_ref[0, 0, rows] = jnp.where(low_head, k_glb, jnp.where(lane == HEAD_DIM, 1.0, zero)).astype(BF16)
        ka_ref[0, 1, rows] = jnp.where(low_head, jnp.where(lane == 0, 1.0, zero), k_glb).astype(BF16)
        for c in range(2):
            qw = rope(p[:, 768 + 128 * c:896 + 128 * c]) * QK_SCALE
            qw_ref[0, rows, 128 * c:128 * (c + 1)] = qw.astype(BF16)
            qd = rope(_head_rms(p[:, 1024 + 128 * c:1152 + 128 * c], qg_ref[...], low_head)) * (QK_SCALE * LOG2E)
            q_t = qd.T
            for g in range(2):
                own = q_t[HEAD_DIM * g:HEAD_DIM * (g + 1)]
                pair = [own, shift_rows] if c == 0 else [shift_rows, own]
                qt_ref[0, 2 * c + g, :, rows] = jnp.concatenate(pair, axis=0).astype(BF16)
        v_t = v_glb.T
        for j in range(2):
            vt_ref[0, j, :, rows] = jnp.concatenate([v_t[HEAD_DIM * j:HEAD_DIM * (j + 1)], ones_rows],
                                                    axis=0).astype(BF16)
        xa_ref[0, rows] = p[:, 1280:1536]
        _store_padded(kw_ref, k_win, low_head, rows)
        _store_padded(vw_ref, v_win, low_head, rows)


def _proj(x, mod, layer, norm_g, w, cos, sin, q_g, k_g, score_bound, n_lat):
    bsz, t_all, _ = x.shape
    tm = _largest_divisor(t_all // TOKEN_TILE, (5, 4, 2, 1)) * TOKEN_TILE
    tok = lambda width: pl.BlockSpec((1, tm, width), lambda b, t: (b, t, 0))
    pad = pl.BlockSpec((1, 2, 2, tm, 128), lambda b, t: (b, 0, 0, t, 0))
    tab = pl.BlockSpec((tm, 128), lambda b, t: (t, 0))
    pad_shape = jax.ShapeDtypeStruct((bsz, 2, 2, t_all, 128), BF16)
    return pl.pallas_call(
        functools.partial(_proj_kernel, n_lat_tiles=n_lat // TOKEN_TILE),
        grid=(bsz, t_all // tm),
        in_specs=[
            pl.BlockSpec(memory_space=pltpu.SMEM),
            tok(D_MODEL),
            *_mod_rows(layer, bsz),
            _const_spec((1, D_MODEL)),
            pl.BlockSpec((D_MODEL, PROJ_W), lambda b, t: (0, 0), pipeline_mode=pl.Buffered(1)),
            tab, tab,
            _const_spec((1, 128)), _const_spec((1, 128)),
        ],
        out_specs=[
            tok(256),
            pl.BlockSpec((1, 4, 128, tm), lambda b, t: (b, 0, 0, t)),
            pad, pad,
            pl.BlockSpec((1, 2, tm, 128), lambda b, t: (b, 0, t, 0)),
            pl.BlockSpec((1, 2, V_ROWS, tm), lambda b, t: (b, 0, 0, t)),
            tok(256), tok(256),
        ],
        out_shape=[
            jax.ShapeDtypeStruct((bsz, t_all, 256), BF16),
            jax.ShapeDtypeStruct((bsz, 4, 128, t_all), BF16),
            pad_shape, pad_shape,
            jax.ShapeDtypeStruct((bsz, 2, t_all, 128), BF16),
            jax.ShapeDtypeStruct((bsz, 2, V_ROWS, t_all), BF16),
            jax.ShapeDtypeStruct((bsz, t_all, 256), F32),
            jax.ShapeDtypeStruct((bsz, t_all, 256), F32),
        ],
        compiler_params=_cparams(("parallel", "parallel")),
        name="proj",
    )((-score_bound * LOG2E).reshape(1), x, mod, mod, norm_g.reshape(1, D_MODEL), w, cos, sin,
      jnp.tile(q_g, 2).reshape(1, 128), jnp.tile(k_g, 2).reshape(1, 128))


def _win_kernel(sink_ref, q_ref, kp_ref, kc_ref, kn_ref, kx_ref, vp_ref, vc_ref, vn_ref, vx_ref,
                o_ref, *, n_lat_blocks):
    i = pl.program_id(1)
    tile = WIN_BLOCKS * BLOCK
    band = tile + 2 * BLOCK
    r = lax.broadcasted_iota(jnp.int32, (tile, band), 0)
    c = lax.broadcasted_iota(jnp.int32, (tile, band), 1)
    blk0 = i * WIN_BLOCKS
    lo = jnp.where(blk0 >= 1, 0, BLOCK)
    hi = jnp.where(blk0 < n_lat_blocks, jnp.minimum((n_lat_blocks - blk0 + 1) * BLOCK, band), 0)
    in_window = (c - r).astype(jnp.uint32) <= 2 * BLOCK
    in_range = (c - lo).astype(jnp.uint32) < (hi - lo).astype(jnp.uint32)
    bias = jnp.where(in_window, jnp.where(in_range, 0.0, NEG), NEG)
    for j in range(2):
        q = q_ref[0, :, 128 * j:128 * (j + 1)]
        acc = jnp.zeros((tile, 128), F32)
        for g in range(2):
            k_band = jnp.concatenate([kp_ref[0, j, g], kc_ref[0, j, g], kn_ref[0, j, g]], axis=0)
            s = jnp.concatenate([_dot_nt(q, k_band) + bias, _dot_nt(q, kx_ref[0, j, g])], axis=1)
            v = jnp.concatenate([vp_ref[0, j, g], vc_ref[0, j, g], vn_ref[0, j, g], vx_ref[0, j, g]], axis=0)
            sink = sink_ref[2 * j + g]
            m = jnp.maximum(s.max(-1, keepdims=True), sink)
            p = jnp.exp(s - m)
            den = p.sum(-1, keepdims=True) + jnp.exp(sink - m)
            acc = acc + _dot(p.astype(BF16), v) / den
        o_ref[0, :, 128 * j:128 * (j + 1)] = acc.astype(o_ref.dtype)


def _window_attention(q, k_pad, v_pad, sink, n_lat, n_ctx, with_ctx):
    bsz, t_all, _ = q.shape
    n_lat_blocks = n_lat // BLOCK
    n_blocks = t_all // BLOCK
    n_q_blocks = n_blocks if with_ctx else n_lat_blocks
    assert n_q_blocks % WIN_BLOCKS == 0 and n_lat_blocks % WIN_BLOCKS == 0
    tile = WIN_BLOCKS * BLOCK
    edge = lambda off: pl.BlockSpec(
        (1, 2, 2, BLOCK, 128), lambda b, i: (b, 0, 0, jnp.clip(i * WIN_BLOCKS + off, 0, n_blocks - 1), 0))
    cur = pl.BlockSpec((1, 2, 2, tile, 128), lambda b, i: (b, 0, 0, i, 0))
    ctx = pl.BlockSpec((1, 2, 2, n_ctx, 128), lambda b, i: (b, 0, 0, n_lat // n_ctx, 0))
    return pl.pallas_call(
        functools.partial(_win_kernel, n_lat_blocks=n_lat_blocks),
        grid=(bsz, n_q_blocks // WIN_BLOCKS),
        in_specs=[
            pl.BlockSpec(memory_space=pltpu.SMEM),
            pl.BlockSpec((1, tile, 256), lambda b, i: (b, i, 0)),
            edge(-1), cur, edge(WIN_BLOCKS), ctx,
            edge(-1), cur, edge(WIN_BLOCKS), ctx,
        ],
        out_specs=pl.BlockSpec((1, tile, 256), lambda b, i: (b, i, 0)),
        out_shape=jax.ShapeDtypeStruct((bsz, n_q_blocks * BLOCK, 256), BF16),
        compiler_params=_cparams(("parallel", "parallel")),
        name="window_attention",
    )(sink, q, k_pad, k_pad, k_pad, k_pad, v_pad, v_pad, v_pad, v_pad)


def _glb_finish(acc_ref, o_ref):
    o_t = jnp.concatenate([acc_ref[g, 0:HEAD_DIM] / acc_ref[g, HEAD_DIM:HEAD_DIM + 1] for g in range(2)],
                          axis=0)
    o_ref[0] = o_t.T.astype(o_ref.dtype)


def _glb_bounded_kernel(qt_ref, k_ref, vt_ref, o_ref, acc_ref, *, tk):
    for ki in range(k_ref.shape[2] // tk):
        keys = slice(ki * tk, (ki + 1) * tk)
        for g in range(2):
            p = jnp.exp2(_dot(k_ref[0, 0, keys], qt_ref[0, g])).astype(BF16)
            pv = _dot(vt_ref[0, 0, :, keys], p)
            acc_ref[g] = pv if ki == 0 else acc_ref[g] + pv
    _glb_finish(acc_ref, o_ref)


def _glb_online_kernel(qt_ref, k_ref, vt_ref, o_ref, acc_ref, m_ref):
    ki = pl.program_id(3)

    @pl.when(ki == 0)
    def _():
        acc_ref[...] = jnp.zeros_like(acc_ref)
        m_ref[...] = jnp.full_like(m_ref, NEG)

    for g in range(2):
        s = _dot(k_ref[0, 0], qt_ref[0, g])
        m_prev = m_ref[g, 0:1]
        m_new = jnp.maximum(m_prev, s.max(axis=0, keepdims=True))
        p = jnp.exp2(s - m_new).astype(BF16)
        acc_ref[g] = jnp.exp2(m_prev - m_new) * acc_ref[g] + _dot(vt_ref[0, 0], p)
        m_ref[g] = jnp.broadcast_to(m_new, m_ref.shape[1:])

    @pl.when(ki == pl.num_programs(3) - 1)
    def _():
        _glb_finish(acc_ref, o_ref)


def _global_attention(qt, k_aug, vt_aug, q_row0, n_q, k_row0, n_k, tq, tk, bounded):
    bsz = qt.shape[0]
    q0, k0 = q_row0 // tq, k_row0 // tk
    scratch = [pltpu.VMEM((2, V_ROWS, tq), F32)]
    if bounded:
        once = dict(pipeline_mode=pl.Buffered(1))
        return pl.pallas_call(
            functools.partial(_glb_bounded_kernel, tk=tk),
            grid=(bsz, 2, n_q // tq),
            in_specs=[
                pl.BlockSpec((1, 2, 128, tq), lambda b, j, qi: (b, j, 0, q0 + qi)),
                pl.BlockSpec((1, 1, n_k, 128), lambda b, j, qi: (b, j, k_row0 // n_k, 0), **once),
                pl.BlockSpec((1, 1, V_ROWS, n_k), lambda b, j, qi: (b, j, 0, k_row0 // n_k), **once),
            ],
            out_specs=pl.BlockSpec((1, tq, 128), lambda b, j, qi: (b, qi, j)),
            out_shape=jax.ShapeDtypeStruct((bsz, n_q, 256), BF16),
            scratch_shapes=scratch,
            compiler_params=_cparams(("parallel", "parallel", "parallel")),
            name="global_attention",
        )(qt, k_aug, vt_aug)
    scratch.append(pltpu.VMEM((2, 8, tq), F32))
    return pl.pallas_call(
        _glb_online_kernel,
        grid=(bsz, 2, n_q // tq, n_k // tk),
        in_specs=[
            pl.BlockSpec((1, 2, 128, tq), lambda b, j, qi, ki: (b, j, 0, q0 + qi)),
            pl.BlockSpec((1, 1, tk, 128), lambda b, j, qi, ki: (b, j, k0 + ki, 0)),
            pl.BlockSpec((1, 1, V_ROWS, tk), lambda b, j, qi, ki: (b, j, 0, k0 + ki)),
        ],
        out_specs=pl.BlockSpec((1, tq, 128), lambda b, j, qi, ki: (b, qi, j)),
        out_shape=jax.ShapeDtypeStruct((bsz, n_q, 256), BF16),
        scratch_shapes=scratch,
        compiler_params=_cparams(("parallel", "parallel", "parallel", "arbitrary")),
        name="global_attention_online",
    )(qt, k_aug, vt_aug)


def _s5_tile(u, bmat_ref, cmat_ref, a_ref, pc_ref, carry_ref, reverse):
    rows = u.shape[0]
    n_groups = rows // SCAN_ROWS
    w = STATE_CHUNK
    ub = u.astype(BF16)
    order = range(n_groups - 1, -1, -1) if reverse else range(n_groups)
    last = 0 if reverse else SCAN_ROWS - 1
    y = None
    for c in range(N_STATE // w):
        bu = _dot(ub, bmat_ref[0, c])
        pr, pi = pc_ref[0, c, :, :w], pc_ref[0, c, :, w:]
        cr, ci = carry_ref[c, :, :w], carry_ref[c, :, w:]
        states = [None] * n_groups
        for g in order:
            vr = bu[SCAN_ROWS * g:SCAN_ROWS * (g + 1), :w]
            vi = bu[SCAN_ROWS * g:SCAN_ROWS * (g + 1), w:]
            for s, k in enumerate((1, 2, 4)):
                ar, ai = a_ref[0, s, c, :, :w], a_ref[0, s, c, :, w:]
                shift = SCAN_ROWS - k if reverse else k
                sr = pltpu.roll(vr, shift, 0)
                si = pltpu.roll(vi, shift, 0)
                vr, vi = vr + ar * sr - ai * si, vi + ar * si + ai * sr
            vr, vi = vr + pr * cr - pi * ci, vi + pr * ci + pi * cr
            cr = jnp.broadcast_to(vr[last:last + 1], (SCAN_ROWS, w))
            ci = jnp.broadcast_to(vi[last:last + 1], (SCAN_ROWS, w))
            states[g] = jnp.concatenate([vr, vi], axis=1)
        carry_ref[c, :, :w] = cr
        carry_ref[c, :, w:] = ci
        part = _dot(jnp.concatenate(states, axis=0).astype(BF16), cmat_ref[0, c])
        y = part if y is None else y + part
    return y


def _s5_kernel(u_ref, bmat_ref, cmat_ref, a_ref, pc_ref, y_ref, carry_ref):
    direction = pl.program_id(0)
    step = pl.program_id(2)

    @pl.when(step == 0)
    def _():
        carry_ref[...] = jnp.zeros_like(carry_ref)

    for reverse in (False, True):
        @pl.when(direction == int(reverse))
        def _(reverse=reverse):
            y_ref[0, 0] = _s5_tile(u_ref[0], bmat_ref, cmat_ref, a_ref, pc_ref, carry_ref, reverse)


def _s5_tables(a_re, a_im, log_dt, b_re, b_im, c_re, c_im):
    dt = jnp.exp(log_dt)[..., None]
    re_dt, im_dt = a_re * dt, a_im * dt
    lb_re, lb_im = jnp.exp(re_dt) * jnp.cos(im_dt), jnp.exp(re_dt) * jnp.sin(im_dt)
    den = a_re * a_re + a_im * a_im
    f_re = ((lb_re - 1.0) * a_re + lb_im * a_im) / den
    f_im = (lb_im * a_re - (lb_re - 1.0) * a_im) / den
    bb_re = f_re[..., None] * b_re - f_im[..., None] * b_im
    bb_im = f_re[..., None] * b_im + f_im[..., None] * b_re
    eye = jnp.eye(SSM_GROUPS, dtype=F32)
    in_w, st_w, n_chunks = SSM_GROUPS * SSM_GROUP, N_STATE, N_STATE // STATE_CHUNK
    b_blk = lambda z: jnp.einsum('dgph,gk->dghkp', z, eye).reshape(2, in_w, n_chunks, STATE_CHUNK)
    bmat = jnp.concatenate([b_blk(bb_re), b_blk(bb_im)], axis=-1).transpose(0, 2, 1, 3).astype(BF16)
    c_blk = lambda z: jnp.einsum('dghp,gk->dgpkh', z, eye).reshape(2, n_chunks, STATE_CHUNK, in_w)
    cmat = jnp.concatenate([c_blk(c_re), -c_blk(c_im)], axis=2).astype(BF16)
    steps = jnp.arange(1, SCAN_ROWS + 1, dtype=F32)[None, :, None, None]
    mag = jnp.exp(re_dt[:, None] * steps)
    ang = im_dt[:, None] * steps
    chunked = lambda z: z.reshape(2, SCAN_ROWS, n_chunks, STATE_CHUNK)
    powers = jnp.concatenate([chunked(mag * jnp.cos(ang)), chunked(mag * jnp.sin(ang))], axis=-1)
    row = jnp.arange(SCAN_ROWS)[:, None, None]
    pc = jnp.stack([powers[0], powers[1][::-1]]).transpose(0, 2, 1, 3)
    a_tabs = []
    for k in (1, 2, 4):
        fwd = jnp.where(row >= k, powers[0, k - 1][None], 0.0)
        bwd = jnp.where(row <= SCAN_ROWS - 1 - k, powers[1, k - 1][None], 0.0)
        a_tabs.append(jnp.stack([fwd, bwd]).transpose(0, 2, 1, 3))
    return bmat, cmat, jnp.stack(a_tabs, axis=1), pc


def _s5_scan(u, tables, n_lat, n_ctx):
    bsz, t_all, width = u.shape
    tm = TOKEN_TILE
    assert n_ctx == tm
    nt = n_lat // tm
    bmat, cmat, a_tab, pc = tables
    n_chunks, lanes = N_STATE // STATE_CHUNK, 2 * STATE_CHUNK
    tile_of = lambda d, s: jnp.where(s == 0, nt, jnp.where(d == 0, s - 1, nt - s))
    return pl.pallas_call(
        _s5_kernel,
        grid=(2, bsz, nt + 1),
        in_specs=[
            pl.BlockSpec((1, tm, width), lambda d, b, s: (b, tile_of(d, s), 0)),
            pl.BlockSpec((1, n_chunks, width, lanes), lambda d, b, s: (d, 0, 0, 0)),
            pl.BlockSpec((1, n_chunks, lanes, width), lambda d, b, s: (d, 0, 0, 0)),
            pl.BlockSpec((1, 3, n_chunks, SCAN_ROWS, lanes), lambda d, b, s: (d, 0, 0, 0, 0)),
            pl.BlockSpec((1, n_chunks, SCAN_ROWS, lanes), lambda d, b, s: (d, 0, 0, 0)),
        ],
        out_specs=pl.BlockSpec((1, 1, tm, width), lambda d, b, s: (d, b, tile_of(d, s), 0)),
        out_shape=jax.ShapeDtypeStruct((2, bsz, t_all, width), F32),
        scratch_shapes=[pltpu.VMEM((n_chunks, SCAN_ROWS, lanes), F32)],
        compiler_params=_cparams(("parallel", "parallel", "arbitrary")),
        name="s5_scan",
    )(u, bmat, cmat, a_tab, pc)


def _gelu_tanh(x):
    c = math.sqrt(2.0 / math.pi)
    return x * (0.5 * (1.0 + jnp.tanh(c * (x + 0.044715 * (x * x * x)))))


def _merge_kernel(x_ref, sh_ref, sc_ref, gt_ref, ng_ref, xa_ref, xap_ref, xan_ref, yb_ref, yf_ref, yr_ref,
                  u_ref, yd_ref, ydt_ref, wg_ref, pw_ref, ps_ref, dsk_ref, glu_ref, wb_ref, wo_ref, o_ref, ext_ref,
                  *, n_lat_tiles, n_lat, n_ctx):
    t = pl.program_id(1)
    x = x_ref[0]
    tm = x.shape[0]
    bw = BRANCH_W
    h = (_rms(x, ng_ref[...]) * (1.0 + sc_ref[...]) + sh_ref[...]).astype(BF16)
    gates = [jax.nn.sigmoid(_dot(h, wg_ref[:, PROJ_W + D_MODEL * k:PROJ_W + D_MODEL * (k + 1)]))
             for k in range(N_BRANCH)]
    is_ctx = t >= n_lat_tiles
    seg_tile = jnp.where(is_ctx, t - n_lat_tiles, t)
    seg_tiles = jnp.where(is_ctx, n_ctx // tm, n_lat_tiles)
    seg_len = jnp.where(is_ctx, n_ctx, n_lat)
    xa = xa_ref[0]
    halo, span = POOL_HALO, tm + 2 * POOL_HALO
    for k in range(4):
        ext_ref[k, 0:halo, :] = jnp.zeros((halo, bw), F32)
        ext_ref[k, halo + span:, :] = jnp.zeros((halo, bw), F32)
    ext_ref[0, halo:2 * halo, :] = jnp.where(seg_tile > 0, xap_ref[0], 0.0)
    ext_ref[0, 2 * halo:2 * halo + tm, :] = xa
    ext_ref[0, 2 * halo + tm:halo + span, :] = jnp.where(seg_tile + 1 < seg_tiles, xan_ref[0], 0.0)
    wide = lambda k, off: ext_ref[k, halo + off:halo + off + span, :]
    ext_ref[1, halo:halo + span, :] = wide(0, -1) + wide(0, 0)
    ext_ref[2, halo:halo + span, :] = wide(1, -1) + wide(1, 1)
    ext_ref[3, halo:halo + span, :] = wide(2, -2) + wide(2, 2)
    own = lambda k, off: ext_ref[k, 2 * halo + off:2 * halo + off + tm, :]
    sums = [own(1, 0), own(2, 0), own(3, 0), own(3, -4) + own(3, 4)]
    lane = lax.broadcasted_iota(jnp.int32, (tm, bw), 1)
    pos = seg_tile * tm + lax.broadcasted_iota(jnp.int32, (tm, bw), 0)
    group = lane // (bw // len(POOL_WINDOWS))
    half_w = jnp.left_shift(1, group)
    count = jnp.minimum(pos + half_w, seg_len) - jnp.maximum(pos - half_w, 0)
    window_sum = jnp.where(group == 0, sums[0], jnp.where(group == 1, sums[1],
                                                           jnp.where(group == 2, sums[2], sums[3])))
    diff = window_sum / count.astype(F32) - xa
    y_a = _dot(diff.astype(BF16), pw_ref[...]) * ps_ref[...]
    y_s = yf_ref[0, 0] + yr_ref[0, 0] + dsk_ref[...] * u_ref[0]
    z = _dot(_gelu_tanh(y_s).astype(BF16), glu_ref[...])
    y_c = z[:, :bw] * jax.nn.sigmoid(z[:, bw:])
    branches = (y_a.astype(BF16), yb_ref[0], y_c.astype(BF16), _tail_rows(yd_ref, ydt_ref, n_lat_tiles))
    total = None
    for gate, y_k, k in zip(gates, branches, range(N_BRANCH)):
        term = gate * _dot(y_k, wb_ref[k])
        total = term if total is None else total + term
    o_ref[0] = x + gt_ref[...] * _dot(total.astype(BF16), wo_ref[...])


def _merge(x, mod, layer, row_of, norm_g, xa, y_b, y_s5, u, y_d, y_d_tail, w_gate, pool_bd, pool_scale, d_skip,
           glu_w, branch_w, out_w, n_rows, n_lat, n_ctx):
    bsz = x.shape[0]
    tm = TOKEN_TILE
    assert n_ctx % tm == 0 or n_rows == n_lat
    halo_blocks = xa.shape[1] // POOL_HALO
    per_tile = tm // POOL_HALO
    tok = lambda width: pl.BlockSpec((1, tm, width), lambda b, t: (b, t, 0))
    single = dict(pipeline_mode=pl.Buffered(1))
    return pl.pallas_call(
        functools.partial(_merge_kernel, n_lat_tiles=n_lat // tm, n_lat=n_lat, n_ctx=n_ctx),
        grid=(bsz, n_rows // tm),
        in_specs=[
            tok(D_MODEL),
            _mod_spec(layer, 3, row_of), _mod_spec(layer, 4, row_of), _mod_spec(layer, 5, row_of),
            _const_spec((1, D_MODEL)),
            tok(256),
            pl.BlockSpec((1, POOL_HALO, 256), lambda b, t: (b, jnp.maximum(t * per_tile - 1, 0), 0)),
            pl.BlockSpec((1, POOL_HALO, 256),
                         lambda b, t: (b, jnp.minimum((t + 1) * per_tile, halo_blocks - 1), 0)),
            tok(256),
            pl.BlockSpec((1, 1, tm, 256), lambda b, t: (0, b, t, 0)),
            pl.BlockSpec((1, 1, tm, 256), lambda b, t: (1, b, t, 0)),
            tok(256), *_tail_specs(tm, 256, n_lat // tm),
            pl.BlockSpec((D_MODEL, PROJ_W + N_BRANCH * D_MODEL), lambda b, t: (0, 0), **single),
            _const_spec((256, 256)), _const_spec((1, 256)), _const_spec((1, 256)),
            _const_spec((256, 512)),
            pl.BlockSpec((N_BRANCH, 256, D_MODEL), lambda b, t: (0, 0, 0), **single),
            pl.BlockSpec((D_MODEL, D_MODEL), lambda b, t: (0, 0), **single),
        ],
        out_specs=tok(D_MODEL),
        out_shape=jax.ShapeDtypeStruct((bsz, n_rows, D_MODEL), F32),
        scratch_shapes=[pltpu.VMEM((4, tm + 4 * POOL_HALO, 256), F32)],
        compiler_params=_cparams(("parallel", "parallel")),
        name="merge",
    )(x, mod, mod, mod, norm_g.reshape(1, D_MODEL), xa, xa, xa, y_b, y_s5, y_s5, u, y_d, y_d_tail,
      w_gate, pool_bd, pool_scale.reshape(1, 256), d_skip.reshape(1, 256), glu_w, branch_w, out_w)


def _rope_tables(n_lat, n_ctx):
    rows = n_lat // GRID_W
    n_freq = HEAD_DIM // 4
    row = jnp.repeat(jnp.arange(rows), GRID_W)
    col = jnp.tile(jnp.arange(GRID_W), rows)
    inv = ROPE_THETA ** (-jnp.arange(n_freq, dtype=F32) / n_freq)
    ang = jnp.stack([row, col], axis=-1).astype(F32)[..., None] * inv
    cos = jnp.broadcast_to(jnp.cos(ang)[:, :, None, :], (n_lat, 2, 2, n_freq)).reshape(n_lat, HEAD_DIM)
    sign = jnp.array([-1.0, 1.0], F32)[None, None, :, None]
    sin = (jnp.sin(ang)[:, :, None, :] * sign).reshape(n_lat, HEAD_DIM)
    cos = jnp.concatenate([cos, jnp.ones((n_ctx, HEAD_DIM), F32)], axis=0)
    sin = jnp.concatenate([sin, jnp.zeros((n_ctx, HEAD_DIM), F32)], axis=0)
    return jnp.tile(cos, (1, 2)), jnp.tile(sin, (1, 2))


def _largest_divisor(n, candidates):
    for c in candidates:
        if n % c == 0:
            return c
    raise ValueError(f"no tile in {candidates} divides {n}")


def _pool_block_diag(pool_w):
    groups, width, _ = pool_w.shape
    eye = jnp.eye(groups, dtype=pool_w.dtype)
    return jnp.einsum('gcd,gk->gckd', pool_w, eye).reshape(groups * width, groups * width)


def kernel(x, c, ctx, c_ctx, w_mod, b_mod, norm_g, ffn_in, ffn_out, w_in, win_sink, qk_norm, pool_w, pool_scale, ssm_a_re, ssm_a_im, ssm_log_dt, ssm_b_re, ssm_b_im, ssm_c_re, ssm_c_im, ssm_d, glu_w, branch_w, out_w, final_g):
    bsz, n_lat, d = x.shape
    n_ctx = ctx.shape[1]
    depth = w_mod.shape[0]
    t_all = n_lat + n_ctx
    assert d == D_MODEL and bsz < MOD_ROWS and n_lat % TOKEN_TILE == 0 and n_ctx == TOKEN_TILE
    assert n_lat % GRID_W == 0

    cond_rows = jnp.zeros((MOD_ROWS, d), F32).at[:bsz].set(c).at[bsz].set(c_ctx)
    mod = _modulation(cond_rows, w_mod, b_mod)
    n_lat_tiles = n_lat // TOKEN_TILE
    row_of = lambda b, t: jnp.where(t >= n_lat_tiles, bsz, b)

    cos, sin = _rope_tables(n_lat, n_ctx)
    tq = _largest_divisor(n_lat, (1024, 512, 256))
    tk = _largest_divisor(t_all, (3328, 1280, 1024, 768, 512, 256))

    xs = x
    for l in range(depth):
        last = l == depth - 1
        bf = lambda w: w.astype(BF16)
        xs = _ffn(xs, mod, l, 0, norm_g[l, 0], bf(ffn_in[l, 0]), bf(ffn_out[l, 0]), t_all, n_lat,
                  tail=ctx if l == 0 else None)

        score_bound = 1.02 * HEAD_DIM ** 0.5 * jnp.max(jnp.abs(qk_norm[l, 0])) * jnp.max(jnp.abs(qk_norm[l, 1]))
        w_in_l = bf(w_in[l])
        qw, qt, kw, vw, ka, vt, u, xa = _proj(xs, mod, l, norm_g[l, 1], w_in_l, cos, sin,
                                              qk_norm[l, 0], qk_norm[l, 1], score_bound, n_lat)
        y_b = _window_attention(qw, kw, vw, win_sink[l], n_lat, n_ctx, not last)

        def global_attention(q_row0, n_q, k_row0, n_k, tq_, tk_):
            run = lambda bounded: lambda: _global_attention(qt, ka, vt, q_row0, n_q, k_row0, n_k, tq_, tk_,
                                                            bounded)
            return lax.cond(score_bound <= MAX_SCORE_BOUND, run(True), run(False))

        y_d = global_attention(0, n_lat, 0, t_all, tq, tk)
        tables = _s5_tables(ssm_a_re[l], ssm_a_im[l], ssm_log_dt[l], ssm_b_re[l], ssm_b_im[l],
                            ssm_c_re[l], ssm_c_im[l])
        y_s5 = _s5_scan(u, tables, n_lat, n_ctx)
        n_rows, y_d_ctx = n_lat, y_d
        if not last:
            y_d_ctx = global_attention(n_lat, n_ctx, n_lat, n_ctx, n_ctx, n_ctx)
            n_rows = t_all
        xs = _merge(xs, mod, l, row_of, norm_g[l, 1], xa, y_b, y_s5, u, y_d, y_d_ctx, w_in_l,
                    bf(_pool_block_diag(pool_w[l])), pool_scale[l], ssm_d[l], bf(glu_w[l]), bf(branch_w[l]),
                    bf(out_w[l]), n_rows, n_lat, n_ctx)
        xs = _ffn(xs, mod, l, 2, norm_g[l, 2], bf(ffn_in[l, 1]), bf(ffn_out[l, 1]), n_rows, n_lat,
                  final_g=final_g if last else None)
    return xs
```

```python
import functools
import math

import jax
import jax.numpy as jnp
import numpy as np
from jax import lax
from jax.experimental import pallas as pl
from jax.experimental.pallas import tpu as pltpu

F32 = jnp.float32
BF16 = jnp.bfloat16

D_MODEL = 1024
D_FF = 2816
N_SUB = 3
BRANCH_W = 256
HEAD_DIM = 64
GRID_W = 64
BLOCK = 128
EPS = 1e-6
ROPE_THETA = 10000.0
N_BRANCH = 4
POOL_WINDOWS = (2, 4, 8, 16)
SSM_GROUPS = 16
SSM_GROUP = 16
SSM_STATE = 64
N_STATE = SSM_GROUPS * SSM_STATE
PROJ_W = 6 * BRANCH_W
MOD_ROWS = 8
POOL_HALO = 8
SCAN_ROWS = 8
STATE_CHUNK = 128
WIN_BLOCKS = 2
NEG = -1e30
QK_SCALE = HEAD_DIM ** -0.5
LOG2E = math.log2(math.e)
V_ROWS = HEAD_DIM + 16
MAX_SCORE_BOUND = 40.0
TOKEN_TILE = 256
VMEM_LIMIT = 56 * 1024 * 1024


def _cparams(sem):
    return pltpu.CompilerParams(dimension_semantics=sem, vmem_limit_bytes=VMEM_LIMIT)


def _dot(a, b):
    return jnp.dot(a, b, preferred_element_type=F32)


def _dot_nt(a, b):
    return lax.dot_general(a, b, (((1,), (1,)), ((), ())), preferred_element_type=F32)


def _rms(x, g):
    ms = jnp.mean(x * x, axis=-1, keepdims=True)
    return x * lax.rsqrt(ms + EPS) * g


def _silu(x):
    return x * jax.nn.sigmoid(x)


def _const_spec(shape):
    nd = len(shape)
    return pl.BlockSpec(shape, lambda *_: (0,) * nd)


def _mod_kernel(s_ref, w_ref, b_ref, o_ref):
    s = _silu(s_ref[...])
    w = w_ref[0]
    s_hi = s.astype(BF16)
    s_lo = (s - s_hi.astype(F32)).astype(BF16)
    w_hi = w.astype(BF16)
    w_lo = (w - w_hi.astype(F32)).astype(BF16)
    o_ref[0] = _dot(s_hi, w_hi) + _dot(s_hi, w_lo) + _dot(s_lo, w_hi) + b_ref[0]


def _modulation(rows, w_mod, b_mod):
    depth, d, width = w_mod.shape
    tn = 1024
    out = pl.pallas_call(
        _mod_kernel,
        grid=(depth, width // tn),
        in_specs=[
            pl.BlockSpec((MOD_ROWS, d), lambda l, j: (0, 0)),
            pl.BlockSpec((1, d, tn), lambda l, j: (l, 0, j)),
            pl.BlockSpec((1, 1, tn), lambda l, j: (l, 0, j)),
        ],
        out_specs=pl.BlockSpec((1, MOD_ROWS, tn), lambda l, j: (l, 0, j)),
        out_shape=jax.ShapeDtypeStruct((depth, MOD_ROWS, width), F32),
        compiler_params=_cparams(("parallel", "parallel")),
        name="modulation",
    )(rows, w_mod, b_mod.reshape(depth, 1, width))
    return out.reshape(depth, MOD_ROWS, N_SUB * 3, 1, d).transpose(0, 2, 1, 3, 4)


def _mod_spec(layer, vec, row_of):
    return pl.BlockSpec((None, None, None, 1, D_MODEL), lambda b, t: (layer, vec, row_of(b, t), 0, 0))


def _mod_rows(layer, bsz):
    spec = lambda row_of: pl.BlockSpec((None, N_SUB * 3, None, 1, D_MODEL),
                                       lambda b, t: (layer, 0, row_of(b), 0, 0))
    return [spec(lambda b: b), spec(lambda b: bsz)]


def _tail_rows(main_ref, tail_ref, n_main_tiles):
    return jnp.where(pl.program_id(1) >= n_main_tiles, tail_ref[0], main_ref[0])


def _ffn_kernel(x_ref, lat_ref, ctx_ref, ng_ref, win_ref, wout_ref, *rest, sub, final, n_main_tiles, n_lat_tiles):
    o_ref = rest[-1]
    tiles = o_ref.shape[1] // TOKEN_TILE
    step = pl.program_id(1)
    for s in range(tiles):
        rows = slice(TOKEN_TILE * s, TOKEN_TILE * (s + 1))
        x = x_ref[0, rows] if n_main_tiles is None else _tail_rows(x_ref, rest[0], n_main_tiles)
        is_ctx = step * tiles + s >= n_lat_tiles
        shift, scale, gate = (jnp.where(is_ctx, ctx_ref[3 * sub + v], lat_ref[3 * sub + v]) for v in range(3))
        h = (_rms(x, ng_ref[...]) * (1.0 + scale) + shift).astype(BF16)
        gu = _dot(h, win_ref[...])
        a = (_silu(gu[:, :D_FF]) * gu[:, D_FF:]).astype(BF16)
        xn = x + (0.5 * gate) * _dot(a, wout_ref[...])
        if final:
            xn = _rms(xn, rest[0][...])
        o_ref[0, rows] = xn


def _tail_specs(tm, width, n_main_tiles):
    return (pl.BlockSpec((1, tm, width), lambda b, t: (b, jnp.minimum(t, n_main_tiles - 1), 0)),
            pl.BlockSpec((1, tm, width), lambda b, t: (b, jnp.maximum(t - n_main_tiles, 0), 0)))


def _ffn(x, mod, layer, sub, norm_g, w_in, w_out, n_rows, n_lat, final_g=None, tail=None):
    bsz = x.shape[0]
    final = final_g is not None
    assert not (final and tail is not None)
    n_tiles = n_rows // TOKEN_TILE
    per_step = 1 if tail is not None else _largest_divisor(n_tiles, (5, 4, 2, 1))
    tm = per_step * TOKEN_TILE
    n_main_tiles = None if tail is None else x.shape[1] // tm
    x_spec = pl.BlockSpec((1, tm, D_MODEL), lambda b, t: (b, t, 0))
    if tail is not None:
        x_spec, tail_spec = _tail_specs(tm, D_MODEL, n_main_tiles)
    in_specs = [
        x_spec,
        *_mod_rows(layer, bsz),
        _const_spec((1, D_MODEL)),
        pl.BlockSpec((D_MODEL, 2 * D_FF), lambda b, t: (0, 0), pipeline_mode=pl.Buffered(1)),
        pl.BlockSpec((D_FF, D_MODEL), lambda b, t: (0, 0), pipeline_mode=pl.Buffered(1)),
    ]
    args = [x, mod, mod, norm_g.reshape(1, D_MODEL), w_in, w_out]
    if final:
        in_specs.append(_const_spec((1, D_MODEL)))
        args.append(final_g.reshape(1, D_MODEL))
    if tail is not None:
        in_specs.append(tail_spec)
        args.append(tail)
    return pl.pallas_call(
        functools.partial(_ffn_kernel, sub=sub, final=final, n_main_tiles=n_main_tiles,
                          n_lat_tiles=n_lat // TOKEN_TILE),
        grid=(bsz, n_tiles // per_step),
        in_specs=in_specs,
        out_specs=pl.BlockSpec((1, tm, D_MODEL), lambda b, t: (b, t, 0)),
        out_shape=jax.ShapeDtypeStruct((bsz, n_rows, D_MODEL), F32),
        compiler_params=_cparams(("parallel", "parallel")),
        name="ffn_final" if final else "ffn",
    )(*args)


def _rope(x, cos, sin, first_half):
    w = x.shape[-1]
    swapped = jnp.where(first_half, pltpu.roll(x, w - 16, 1), pltpu.roll(x, 16, 1))
    return x * cos + swapped * sin


def _head_rms(x, g, low_head):
    x2 = x * x
    lo = jnp.sum(jnp.where(low_head, x2, 0.0), axis=-1, keepdims=True)
    hi = jnp.sum(jnp.where(low_head, 0.0, x2), axis=-1, keepdims=True)
    ms = jnp.where(low_head, lo, hi) * (1.0 / HEAD_DIM)
    return x * lax.rsqrt(ms + EPS) * g


def _store_padded(ref, x, low_head, rows):
    sw = pltpu.roll(x, HEAD_DIM, 1)
    zero = jnp.zeros_like(x)
    ref[0, 0, 0, rows] = jnp.where(low_head, x, zero).astype(ref.dtype)
    ref[0, 0, 1, rows] = jnp.where(low_head, zero, sw).astype(ref.dtype)
    ref[0, 1, 0, rows] = jnp.where(low_head, sw, zero).astype(ref.dtype)
    ref[0, 1, 1, rows] = jnp.where(low_head, zero, x).astype(ref.dtype)


def _proj_kernel(shift_ref, x_ref, lat_ref, ctx_ref, ng_ref, w_ref, cos_ref, sin_ref, qg_ref, kg_ref,
                 qw_ref, qt_ref, kw_ref, vw_ref, ka_ref, vt_ref, u_ref, xa_ref, *, n_lat_tiles):
    tm = TOKEN_TILE
    tiles = x_ref.shape[1] // tm
    step = pl.program_id(1)
    lane = lax.broadcasted_iota(jnp.int32, (tm, 2 * HEAD_DIM), 1)
    low_head = lane < HEAD_DIM
    first_half = jnp.bitwise_and(lane, 31) < 16
    first_row = lax.broadcasted_iota(jnp.int32, (HEAD_DIM, tm), 0) == 0
    shift_rows = jnp.where(first_row, shift_ref[0], 0.0)
    ones_rows = jnp.where(lax.broadcasted_iota(jnp.int32, (V_ROWS - HEAD_DIM, tm), 0) == 0, 1.0, 0.0)
    for s in range(tiles):
        rows = slice(tm * s, tm * (s + 1))
        x = x_ref[0, rows]
        is_ctx = step * tiles + s >= n_lat_tiles
        shift, scale = (jnp.where(is_ctx, ctx_ref[3 + v], lat_ref[3 + v]) for v in range(2))
        h = (_rms(x, ng_ref[...]) * (1.0 + scale) + shift).astype(BF16)
        p = _dot(h, w_ref[...])
        cos = cos_ref[rows]
        sin = sin_ref[rows]
        rope = lambda z: _rope(z, cos, sin, first_half)
        k_win = rope(p[:, 0:128])
        v_win = p[:, 128:256]
        u_ref[0, rows] = p[:, 256:512]
        k_glb = rope(_head_rms(p[:, 512:640], kg_ref[...], low_head))
        v_glb = p[:, 640:768]
        zero = jnp.zeros_like(k_glb)
        ka_ref[0, 0, rows] = jnp.where(low_head, k_glb, jnp.where(lane == HEAD_DIM, 1.0, zero)).astype(BF16)
        ka_ref[0, 1, rows] = jnp.where(low_head, jnp.where(lane == 0, 1.0, zero), k_glb).astype(BF16)
        for c in range(2):
            qw = rope(p[:, 768 + 128 * c:896 + 128 * c]) * QK_SCALE
            qw_ref[0, rows, 128 * c:128 * (c + 1)] = qw.astype(BF16)
            qd = rope(_head_rms(p[:, 1024 + 128 * c:1152 + 128 * c], qg_ref[...], low_head)) * (QK_SCALE * LOG2E)
            q_t = qd.T
            for g in range(2):
                own = q_t[HEAD_DIM * g:HEAD_DIM * (g + 1)]
                pair = [own, shift_rows] if c == 0 else [shift_rows, own]
                qt_ref[0, 2 * c + g, :, rows] = jnp.concatenate(pair, axis=0).astype(BF16)
        v_t = v_glb.T
        for j in range(2):
            vt_ref[0, j, :, rows] = jnp.concatenate([v_t[HEAD_DIM * j:HEAD_DIM * (j + 1)], ones_rows],
                                                    axis=0).astype(BF16)
        xa_ref[0, rows] = p[:, 1280:1536]
        _store_padded(kw_ref, k_win, low_head, rows)
        _store_padded(vw_ref, v_win, low_head, rows)


def _proj(x, mod, layer, norm_g, w, cos, sin, q_g, k_g, score_bound, n_lat):
    bsz, t_all, _ = x.shape
    tm = _largest_divisor(t_all // TOKEN_TILE, (5, 4, 2, 1)) * TOKEN_TILE
    tok = lambda width: pl.BlockSpec((1, tm, width), lambda b, t: (b, t, 0))
    pad = pl.BlockSpec((1, 2, 2, tm, 128), lambda b, t: (b, 0, 0, t, 0))
    tab = pl.BlockSpec((tm, 128), lambda b, t: (t, 0))
    pad_shape = jax.ShapeDtypeStruct((bsz, 2, 2, t_all, 128), BF16)
    return pl.pallas_call(
        functools.partial(_proj_kernel, n_lat_tiles=n_lat // TOKEN_TILE),
        grid=(bsz, t_all // tm),
        in_specs=[
            pl.BlockSpec(memory_space=pltpu.SMEM),
            tok(D_MODEL),
            *_mod_rows(layer, bsz),
            _const_spec((1, D_MODEL)),
            pl.BlockSpec((D_MODEL, PROJ_W), lambda b, t: (0, 0), pipeline_mode=pl.Buffered(1)),
            tab, tab,
            _const_spec((1, 128)), _const_spec((1, 128)),
        ],
        out_specs=[
            tok(256),
            pl.BlockSpec((1, 4, 128, tm), lambda b, t: (b, 0, 0, t)),
            pad, pad,
            pl.BlockSpec((1, 2, tm, 128), lambda b, t: (b, 0, t, 0)),
            pl.BlockSpec((1, 2, V_ROWS, tm), lambda b, t: (b, 0, 0, t)),
            tok(256), tok(256),
        ],
        out_shape=[
            jax.ShapeDtypeStruct((bsz, t_all, 256), BF16),
            jax.ShapeDtypeStruct((bsz, 4, 128, t_all), BF16),
            pad_shape, pad_shape,
            jax.ShapeDtypeStruct((bsz, 2, t_all, 128), BF16),
            jax.ShapeDtypeStruct((bsz, 2, V_ROWS, t_all), BF16),
            jax.ShapeDtypeStruct((bsz, t_all, 256), F32),
            jax.ShapeDtypeStruct((bsz, t_all, 256), F32),
        ],
        compiler_params=_cparams(("parallel", "parallel")),
        name="proj",
    )((-score_bound * LOG2E).reshape(1), x, mod, mod, norm_g.reshape(1, D_MODEL), w, cos, sin,
      jnp.tile(q_g, 2).reshape(1, 128), jnp.tile(k_g, 2).reshape(1, 128))


def _win_kernel(sink_ref, q_ref, kp_ref, kc_ref, kn_ref, kx_ref, vp_ref, vc_ref, vn_ref, vx_ref,
                o_ref, *, n_lat_blocks):
    i = pl.program_id(1)
    tile = WIN_BLOCKS * BLOCK
    band = tile + 2 * BLOCK
    tiles = q_ref.shape[1] // tile
    r = lax.broadcasted_iota(jnp.int32, (tile, band), 0)
    c = lax.broadcasted_iota(jnp.int32, (tile, band), 1)
    in_window = (c - r).astype(jnp.uint32) <= 2 * BLOCK

    def band_rows(edge_before, cur, edge_after, sub, j, g):
        first, last = tile * sub - BLOCK, tile * (sub + 1) + BLOCK
        parts = [edge_before[0, j, g]] if sub == 0 else []
        parts.append(cur[0, j, g, max(first, 0):min(last, tile * tiles)])
        if sub == tiles - 1:
            parts.append(edge_after[0, j, g])
        return parts

    for sub in range(tiles):
        rows = slice(tile * sub, tile * (sub + 1))
        blk0 = (i * tiles + sub) * WIN_BLOCKS
        lo = jnp.where(blk0 >= 1, 0, BLOCK)
        hi = jnp.where(blk0 < n_lat_blocks, jnp.minimum((n_lat_blocks - blk0 + 1) * BLOCK, band), 0)
        in_range = (c - lo).astype(jnp.uint32) < (hi - lo).astype(jnp.uint32)
        bias = jnp.where(in_window, jnp.where(in_range, 0.0, NEG), NEG)
        for j in range(2):
            q = q_ref[0, rows, 128 * j:128 * (j + 1)]
            acc = jnp.zeros((tile, 128), F32)
            for g in range(2):
                k_band = jnp.concatenate(band_rows(kp_ref, kc_ref, kn_ref, sub, j, g), axis=0)
                s = jnp.concatenate([_dot_nt(q, k_band) + bias, _dot_nt(q, kx_ref[0, j, g])], axis=1)
                v = jnp.concatenate(band_rows(vp_ref, vc_ref, vn_ref, sub, j, g) + [vx_ref[0, j, g]], axis=0)
                sink = sink_ref[2 * j + g]
                m = jnp.maximum(s.max(-1, keepdims=True), sink)
                p = jnp.exp(s - m)
                den = p.sum(-1, keepdims=True) + jnp.exp(sink - m)
                acc = acc + _dot(p.astype(BF16), v) / den
            o_ref[0, rows, 128 * j:128 * (j + 1)] = acc.astype(o_ref.dtype)


def _window_attention(q, k_pad, v_pad, sink, n_lat, n_ctx, with_ctx):
    bsz, t_all, _ = q.shape
    n_lat_blocks = n_lat // BLOCK
    n_blocks = t_all // BLOCK
    n_q_blocks = n_blocks if with_ctx else n_lat_blocks
    assert n_q_blocks % WIN_BLOCKS == 0 and n_lat_blocks % WIN_BLOCKS == 0
    n_tiles = n_q_blocks // WIN_BLOCKS
    per_step = _largest_divisor(n_tiles, (5, 4, 2, 1))
    step_blocks = per_step * WIN_BLOCKS
    tile = step_blocks * BLOCK
    edge = lambda off: pl.BlockSpec(
        (1, 2, 2, BLOCK, 128), lambda b, i: (b, 0, 0, jnp.clip(i * step_blocks + off, 0, n_blocks - 1), 0))
    cur = pl.BlockSpec((1, 2, 2, tile, 128), lambda b, i: (b, 0, 0, i, 0))
    ctx = pl.BlockSpec((1, 2, 2, n_ctx, 128), lambda b, i: (b, 0, 0, n_lat // n_ctx, 0))
    return pl.pallas_call(
        functools.partial(_win_kernel, n_lat_blocks=n_lat_blocks),
        grid=(bsz, n_tiles // per_step),
        in_specs=[
            pl.BlockSpec(memory_space=pltpu.SMEM),
            pl.BlockSpec((1, tile, 256), lambda b, i: (b, i, 0)),
            edge(-1), cur, edge(step_blocks), ctx,
            edge(-1), cur, edge(step_blocks), ctx,
        ],
        out_specs=pl.BlockSpec((1, tile, 256), lambda b, i: (b, i, 0)),
        out_shape=jax.ShapeDtypeStruct((bsz, n_q_blocks * BLOCK, 256), BF16),
        compiler_params=_cparams(("parallel", "parallel")),
        name="window_attention",
    )(sink, q, k_pad, k_pad, k_pad, k_pad, v_pad, v_pad, v_pad, v_pad)


def _glb_finish(acc_ref, o_ref):
    o_t = jnp.concatenate([acc_ref[g, 0:HEAD_DIM] / acc_ref[g, HEAD_DIM:HEAD_DIM + 1] for g in range(2)],
                          axis=0)
    o_ref[0] = o_t.T.astype(o_ref.dtype)


def _glb_bounded_kernel(qt_ref, k_ref, vt_ref, o_ref, acc_ref, *, tk):
    for ki in range(k_ref.shape[2] // tk):
        keys = slice(ki * tk, (ki + 1) * tk)
        for g in range(2):
            p = jnp.exp2(_dot(k_ref[0, 0, keys], qt_ref[0, g])).astype(BF16)
            pv = _dot(vt_ref[0, 0, :, keys], p)
            acc_ref[g] = pv if ki == 0 else acc_ref[g] + pv
    _glb_finish(acc_ref, o_ref)


def _glb_online_kernel(qt_ref, k_ref, vt_ref, o_ref, acc_ref, m_ref):
    ki = pl.program_id(3)

    @pl.when(ki == 0)
    def _():
        acc_ref[...] = jnp.zeros_like(acc_ref)
        m_ref[...] = jnp.full_like(m_ref, NEG)

    for g in range(2):
        s = _dot(k_ref[0, 0], qt_ref[0, g])
        m_prev = m_ref[g, 0:1]
        m_new = jnp.maximum(m_prev, s.max(axis=0, keepdims=True))
        p = jnp.exp2(s - m_new).astype(BF16)
        acc_ref[g] = jnp.exp2(m_prev - m_new) * acc_ref[g] + _dot(vt_ref[0, 0], p)
        m_ref[g] = jnp.broadcast_to(m_new, m_ref.shape[1:])

    @pl.when(ki == pl.num_programs(3) - 1)
    def _():
        _glb_finish(acc_ref, o_ref)


def _global_attention(qt, k_aug, vt_aug, q_row0, n_q, k_row0, n_k, tq, tk, bounded):
    bsz = qt.shape[0]
    q0, k0 = q_row0 // tq, k_row0 // tk
    scratch = [pltpu.VMEM((2, V_ROWS, tq), F32)]
    if bounded:
        once = dict(pipeline_mode=pl.Buffered(1))
        return pl.pallas_call(
            functools.partial(_glb_bounded_kernel, tk=tk),
            grid=(bsz, 2, n_q // tq),
            in_specs=[
                pl.BlockSpec((1, 2, 128, tq), lambda b, j, qi: (b, j, 0, q0 + qi)),
                pl.BlockSpec((1, 1, n_k, 128), lambda b, j, qi: (b, j, k_row0 // n_k, 0), **once),
                pl.BlockSpec((1, 1, V_ROWS, n_k), lambda b, j, qi: (b, j, 0, k_row0 // n_k), **once),
            ],
            out_specs=pl.BlockSpec((1, tq, 128), lambda b, j, qi: (b, qi, j)),
            out_shape=jax.ShapeDtypeStruct((bsz, n_q, 256), BF16),
            scratch_shapes=scratch,
            compiler_params=_cparams(("parallel", "parallel", "parallel")),
            name="global_attention",
        )(qt, k_aug, vt_aug)
    scratch.append(pltpu.VMEM((2, 8, tq), F32))
    return pl.pallas_call(
        _glb_online_kernel,
        grid=(bsz, 2, n_q // tq, n_k // tk),
        in_specs=[
            pl.BlockSpec((1, 2, 128, tq), lambda b, j, qi, ki: (b, j, 0, q0 + qi)),
            pl.BlockSpec((1, 1, tk, 128), lambda b, j, qi, ki: (b, j, k0 + ki, 0)),
            pl.BlockSpec((1, 1, V_ROWS, tk), lambda b, j, qi, ki: (b, j, 0, k0 + ki)),
        ],
        out_specs=pl.BlockSpec((1, tq, 128), lambda b, j, qi, ki: (b, qi, j)),
        out_shape=jax.ShapeDtypeStruct((bsz, n_q, 256), BF16),
        scratch_shapes=scratch,
        compiler_params=_cparams(("parallel", "parallel", "parallel", "arbitrary")),
        name="global_attention_online",
    )(qt, k_aug, vt_aug)


def _s5_tile(u, bmat_ref, cmat_ref, a_ref, pc_ref, carry_ref, reverse):
    rows = u.shape[0]
    n_groups = rows // SCAN_ROWS
    w = STATE_CHUNK
    ub = u.astype(BF16)
    order = range(n_groups - 1, -1, -1) if reverse else range(n_groups)
    last = 0 if reverse else SCAN_ROWS - 1
    y = None
    for c in range(N_STATE // w):
        bu = _dot(ub, bmat_ref[0, c])
        pr, pi = pc_ref[0, c, :, :w], pc_ref[0, c, :, w:]
        cr, ci = carry_ref[c, :, :w], carry_ref[c, :, w:]
        states = [None] * n_groups
        for g in order:
            vr = bu[SCAN_ROWS * g:SCAN_ROWS * (g + 1), :w]
            vi = bu[SCAN_ROWS * g:SCAN_ROWS * (g + 1), w:]
            for s, k in enumerate((1, 2, 4)):
                ar, ai = a_ref[0, s, c, :, :w], a_ref[0, s, c, :, w:]
                shift = SCAN_ROWS - k if reverse else k
                sr = pltpu.roll(vr, shift, 0)
                si = pltpu.roll(vi, shift, 0)
                vr, vi = vr + ar * sr - ai * si, vi + ar * si + ai * sr
            vr, vi = vr + pr * cr - pi * ci, vi + pr * ci + pi * cr
            cr = jnp.broadcast_to(vr[last:last + 1], (SCAN_ROWS, w))
            ci = jnp.broadcast_to(vi[last:last + 1], (SCAN_ROWS, w))
            states[g] = jnp.concatenate([vr, vi], axis=1)
        carry_ref[c, :, :w] = cr
        carry_ref[c, :, w:] = ci
        part = _dot(jnp.concatenate(states, axis=0).astype(BF16), cmat_ref[0, c])
        y = part if y is None else y + part
    return y


def _s5_kernel(u_ref, bmat_ref, cmat_ref, a_ref, pc_ref, y_ref, carry_ref):
    direction = pl.program_id(0)
    step = pl.program_id(2)

    @pl.when(step == 0)
    def _():
        carry_ref[...] = jnp.zeros_like(carry_ref)

    for reverse in (False, True):
        @pl.when(direction == int(reverse))
        def _(reverse=reverse):
            y_ref[0, 0] = _s5_tile(u_ref[0], bmat_ref, cmat_ref, a_ref, pc_ref, carry_ref, reverse)


def _s5_tables(a_re, a_im, log_dt, b_re, b_im, c_re, c_im):
    dt = jnp.exp(log_dt)[..., None]
    re_dt, im_dt = a_re * dt, a_im * dt
    lb_re, lb_im = jnp.exp(re_dt) * jnp.cos(im_dt), jnp.exp(re_dt) * jnp.sin(im_dt)
    den = a_re * a_re + a_im * a_im
    f_re = ((lb_re - 1.0) * a_re + lb_im * a_im) / den
    f_im = (lb_im * a_re - (lb_re - 1.0) * a_im) / den
    bb_re = f_re[..., None] * b_re - f_im[..., None] * b_im
    bb_im = f_re[..., None] * b_im + f_im[..., None] * b_re
    eye = jnp.eye(SSM_GROUPS, dtype=F32)
    in_w, st_w, n_chunks = SSM_GROUPS * SSM_GROUP, N_STATE, N_STATE // STATE_CHUNK
    b_blk = lambda z: jnp.einsum('dgph,gk->dghkp', z, eye).reshape(2, in_w, n_chunks, STATE_CHUNK)
    bmat = jnp.concatenate([b_blk(bb_re), b_blk(bb_im)], axis=-1).transpose(0, 2, 1, 3).astype(BF16)
    c_blk = lambda z: jnp.einsum('dghp,gk->dgpkh', z, eye).reshape(2, n_chunks, STATE_CHUNK, in_w)
    cmat = jnp.concatenate([c_blk(c_re), -c_blk(c_im)], axis=2).astype(BF16)
    steps = jnp.arange(1, SCAN_ROWS + 1, dtype=F32)[None, :, None, None]
    mag = jnp.exp(re_dt[:, None] * steps)
    ang = im_dt[:, None] * steps
    chunked = lambda z: z.reshape(2, SCAN_ROWS, n_chunks, STATE_CHUNK)
    powers = jnp.concatenate([chunked(mag * jnp.cos(ang)), chunked(mag * jnp.sin(ang))], axis=-1)
    row = jnp.arange(SCAN_ROWS)[:, None, None]
    pc = jnp.stack([powers[0], powers[1][::-1]]).transpose(0, 2, 1, 3)
    a_tabs = []
    for k in (1, 2, 4):
        fwd = jnp.where(row >= k, powers[0, k - 1][None], 0.0)
        bwd = jnp.where(row <= SCAN_ROWS - 1 - k, powers[1, k - 1][None], 0.0)
        a_tabs.append(jnp.stack([fwd, bwd]).transpose(0, 2, 1, 3))
    return bmat, cmat, jnp.stack(a_tabs, axis=1), pc


def _s5_scan(u, tables, n_lat, n_ctx):
    bsz, t_all, width = u.shape
    tm = TOKEN_TILE
    assert n_ctx == tm
    nt = n_lat // tm
    bmat, cmat, a_tab, pc = tables
    n_chunks, lanes = N_STATE // STATE_CHUNK, 2 * STATE_CHUNK
    tile_of = lambda d, s: jnp.where(s == 0, nt, jnp.where(d == 0, s - 1, nt - s))
    return pl.pallas_call(
        _s5_kernel,
        grid=(2, bsz, nt + 1),
        in_specs=[
            pl.BlockSpec((1, tm, width), lambda d, b, s: (b, tile_of(d, s), 0)),
            pl.BlockSpec((1, n_chunks, width, lanes), lambda d, b, s: (d, 0, 0, 0)),
            pl.BlockSpec((1, n_chunks, lanes, width), lambda d, b, s: (d, 0, 0, 0)),
            pl.BlockSpec((1, 3, n_chunks, SCAN_ROWS, lanes), lambda d, b, s: (d, 0, 0, 0, 0)),
            pl.BlockSpec((1, n_chunks, SCAN_ROWS, lanes), lambda d, b, s: (d, 0, 0, 0)),
        ],
        out_specs=pl.BlockSpec((1, 1, tm, width), lambda d, b, s: (d, b, tile_of(d, s), 0)),
        out_shape=jax.ShapeDtypeStruct((2, bsz, t_all, width), F32),
        scratch_shapes=[pltpu.VMEM((n_chunks, SCAN_ROWS, lanes), F32)],
        compiler_params=_cparams(("parallel", "parallel", "arbitrary")),
        name="s5_scan",
    )(u, bmat, cmat, a_tab, pc)


def _gelu_tanh(x):
    c = math.sqrt(2.0 / math.pi)
    return x * (0.5 * (1.0 + jnp.tanh(c * (x + 0.044715 * (x * x * x)))))


def _merge_kernel(x_ref, sh_ref, sc_ref, gt_ref, ng_ref, xa_ref, xap_ref, xan_ref, yb_ref, yf_ref, yr_ref,
                  u_ref, yd_ref, ydt_ref, wg_ref, pw_ref, ps_ref, dsk_ref, glu_ref, wb_ref, wo_ref, o_ref, ext_ref,
                  *, n_lat_tiles, n_lat, n_ctx):
    t = pl.program_id(1)
    x = x_ref[0]
    tm = x.shape[0]
    bw = BRANCH_W
    h = (_rms(x, ng_ref[...]) * (1.0 + sc_ref[...]) + sh_ref[...]).astype(BF16)
    gates = [jax.nn.sigmoid(_dot(h, wg_ref[:, PROJ_W + D_MODEL * k:PROJ_W + D_MODEL * (k + 1)]))
             for k in range(N_BRANCH)]
    is_ctx = t >= n_lat_tiles
    seg_tile = jnp.where(is_ctx, t - n_lat_tiles, t)
    seg_tiles = jnp.where(is_ctx, n_ctx // tm, n_lat_tiles)
    seg_len = jnp.where(is_ctx, n_ctx, n_lat)
    xa = xa_ref[0]
    halo, span = POOL_HALO, tm + 2 * POOL_HALO
    for k in range(4):
        ext_ref[k, 0:halo, :] = jnp.zeros((halo, bw), F32)
        ext_ref[k, halo + span:, :] = jnp.zeros((halo, bw), F32)
    ext_ref[0, halo:2 * halo, :] = jnp.where(seg_tile > 0, xap_ref[0], 0.0)
    ext_ref[0, 2 * halo:2 * halo + tm, :] = xa
    ext_ref[0, 2 * halo + tm:halo + span, :] = jnp.where(seg_tile + 1 < seg_tiles, xan_ref[0], 0.0)
    wide = lambda k, off: ext_ref[k, halo + off:halo + off + span, :]
    ext_ref[1, halo:halo + span, :] = wide(0, -1) + wide(0, 0)
    ext_ref[2, halo:halo + span, :] = wide(1, -1) + wide(1, 1)
    ext_ref[3, halo:halo + span, :] = wide(2, -2) + wide(2, 2)
    own = lambda k, off: ext_ref[k, 2 * halo + off:2 * halo + off + tm, :]
    sums = [own(1, 0), own(2, 0), own(3, 0), own(3, -4) + own(3, 4)]
    lane = lax.broadcasted_iota(jnp.int32, (tm, bw), 1)
    pos = seg_tile * tm + lax.broadcasted_iota(jnp.int32, (tm, bw), 0)
    group = lane // (bw // len(POOL_WINDOWS))
    half_w = jnp.left_shift(1, group)
    count = jnp.minimum(pos + half_w, seg_len) - jnp.maximum(pos - half_w, 0)
    window_sum = jnp.where(group == 0, sums[0], jnp.where(group == 1, sums[1],
                                                           jnp.where(group == 2, sums[2], sums[3])))
    diff = window_sum / count.astype(F32) - xa
    y_a = _dot(diff.astype(BF16), pw_ref[...]) * ps_ref[...]
    y_s = yf_ref[0, 0] + yr_ref[0, 0] + dsk_ref[...] * u_ref[0]
    z = _dot(_gelu_tanh(y_s).astype(BF16), glu_ref[...])
    y_c = z[:, :bw] * jax.nn.sigmoid(z[:, bw:])
    branches = (y_a.astype(BF16), yb_ref[0], y_c.astype(BF16), _tail_rows(yd_ref, ydt_ref, n_lat_tiles))
    total = None
    for gate, y_k, k in zip(gates, branches, range(N_BRANCH)):
        term = gate * _dot(y_k, wb_ref[k])
        total = term if total is None else total + term
    o_ref[0] = x + gt_ref[...] * _dot(total.astype(BF16), wo_ref[...])


def _merge(x, mod, layer, row_of, norm_g, xa, y_b, y_s5, u, y_d, y_d_tail, w_gate, pool_bd, pool_scale, d_skip,
           glu_w, branch_w, out_w, n_rows, n_lat, n_ctx):
    bsz = x.shape[0]
    tm = TOKEN_TILE
    assert n_ctx % tm == 0 or n_rows == n_lat
    halo_blocks = xa.shape[1] // POOL_HALO
    per_tile = tm // POOL_HALO
    tok = lambda width: pl.BlockSpec((1, tm, width), lambda b, t: (b, t, 0))
    single = dict(pipeline_mode=pl.Buffered(1))
    return pl.pallas_call(
        functools.partial(_merge_kernel, n_lat_tiles=n_lat // tm, n_lat=n_lat, n_ctx=n_ctx),
        grid=(bsz, n_rows // tm),
        in_specs=[
            tok(D_MODEL),
            _mod_spec(layer, 3, row_of), _mod_spec(layer, 4, row_of), _mod_spec(layer, 5, row_of),
            _const_spec((1, D_MODEL)),
            tok(256),
            pl.BlockSpec((1, POOL_HALO, 256), lambda b, t: (b, jnp.maximum(t * per_tile - 1, 0), 0)),
            pl.BlockSpec((1, POOL_HALO, 256),
                         lambda b, t: (b, jnp.minimum((t + 1) * per_tile, halo_blocks - 1), 0)),
            tok(256),
            pl.BlockSpec((1, 1, tm, 256), lambda b, t: (0, b, t, 0)),
            pl.BlockSpec((1, 1, tm, 256), lambda b, t: (1, b, t, 0)),
            tok(256), *_tail_specs(tm, 256, n_lat // tm),
            pl.BlockSpec((D_MODEL, PROJ_W + N_BRANCH * D_MODEL), lambda b, t: (0, 0), **single),
            _const_spec((256, 256)), _const_spec((1, 256)), _const_spec((1, 256)),
            _const_spec((256, 512)),
            pl.BlockSpec((N_BRANCH, 256, D_MODEL), lambda b, t: (0, 0, 0), **single),
            pl.BlockSpec((D_MODEL, D_MODEL), lambda b, t: (0, 0), **single),
        ],
        out_specs=tok(D_MODEL),
        out_shape=jax.ShapeDtypeStruct((bsz, n_rows, D_MODEL), F32),
        scratch_shapes=[pltpu.VMEM((4, tm + 4 * POOL_HALO, 256), F32)],
        compiler_params=_cparams(("parallel", "parallel")),
        name="merge",
    )(x, mod, mod, mod, norm_g.reshape(1, D_MODEL), xa, xa, xa, y_b, y_s5, y_s5, u, y_d, y_d_tail,
      w_gate, pool_bd, pool_scale.reshape(1, 256), d_skip.reshape(1, 256), glu_w, branch_w, out_w)


def _rope_tables(n_lat, n_ctx):
    rows = n_lat // GRID_W
    n_freq = HEAD_DIM // 4
    row = jnp.repeat(jnp.arange(rows), GRID_W)
    col = jnp.tile(jnp.arange(GRID_W), rows)
    inv = ROPE_THETA ** (-jnp.arange(n_freq, dtype=F32) / n_freq)
    ang = jnp.stack([row, col], axis=-1).astype(F32)[..., None] * inv
    cos = jnp.broadcast_to(jnp.cos(ang)[:, :, None, :], (n_lat, 2, 2, n_freq)).reshape(n_lat, HEAD_DIM)
    sign = jnp.array([-1.0, 1.0], F32)[None, None, :, None]
    sin = (jnp.sin(ang)[:, :, None, :] * sign).reshape(n_lat, HEAD_DIM)
    cos = jnp.concatenate([cos, jnp.ones((n_ctx, HEAD_DIM), F32)], axis=0)
    sin = jnp.concatenate([sin, jnp.zeros((n_ctx, HEAD_DIM), F32)], axis=0)
    return jnp.tile(cos, (1, 2)), jnp.tile(sin, (1, 2))


def _largest_divisor(n, candidates):
    for c in candidates:
        if n % c == 0:
            return c
    raise ValueError(f"no tile in {candidates} divides {n}")


def _pool_block_diag(pool_w):
    groups, width, _ = pool_w.shape
    eye = jnp.eye(groups, dtype=pool_w.dtype)
    return jnp.einsum('gcd,gk->gckd', pool_w, eye).reshape(groups * width, groups * width)


def kernel(x, c, ctx, c_ctx, w_mod, b_mod, norm_g, ffn_in, ffn_out, w_in, win_sink, qk_norm, pool_w, pool_scale, ssm_a_re, ssm_a_im, ssm_log_dt, ssm_b_re, ssm_b_im, ssm_c_re, ssm_c_im, ssm_d, glu_w, branch_w, out_w, final_g):
    bsz, n_lat, d = x.shape
    n_ctx = ctx.shape[1]
    depth = w_mod.shape[0]
    t_all = n_lat + n_ctx
    assert d == D_MODEL and bsz < MOD_ROWS and n_lat % TOKEN_TILE == 0 and n_ctx == TOKEN_TILE
    assert n_lat % GRID_W == 0

    cond_rows = jnp.zeros((MOD_ROWS, d), F32).at[:bsz].set(c).at[bsz].set(c_ctx)
    mod = _modulation(cond_rows, w_mod, b_mod)
    n_lat_tiles = n_lat // TOKEN_TILE
    row_of = lambda b, t: jnp.where(t >= n_lat_tiles, bsz, b)

    cos, sin = _rope_tables(n_lat, n_ctx)
    tq = _largest_divisor(n_lat, (1024, 512, 256))
    tk = _largest_divisor(t_all, (3328, 1280, 1024, 768, 512, 256))

    xs = x
    for l in range(depth):
        last = l == depth - 1
        bf = lambda w: w.astype(BF16)
        xs = _ffn(xs, mod, l, 0, norm_g[l, 0], bf(ffn_in[l, 0]), bf(ffn_out[l, 0]), t_all, n_lat,
                  tail=ctx if l == 0 else None)

        score_bound = 1.02 * HEAD_DIM ** 0.5 * jnp.max(jnp.abs(qk_norm[l, 0])) * jnp.max(jnp.abs(qk_norm[l, 1]))
        w_in_l = bf(w_in[l])
        qw, qt, kw, vw, ka, vt, u, xa = _proj(xs, mod, l, norm_g[l, 1], w_in_l, cos, sin,
                                              qk_norm[l, 0], qk_norm[l, 1], score_bound, n_lat)
        y_b = _window_attention(qw, kw, vw, win_sink[l], n_lat, n_ctx, not last)

        def global_attention(q_row0, n_q, k_row0, n_k, tq_, tk_):
            run = lambda bounded: lambda: _global_attention(qt, ka, vt, q_row0, n_q, k_row0, n_k, tq_, tk_,
                                                            bounded)
            return lax.cond(score_bound <= MAX_SCORE_BOUND, run(True), run(False))

        y_d = global_attention(0, n_lat, 0, t_all, tq, tk)
        tables = _s5_tables(ssm_a_re[l], ssm_a_im[l], ssm_log_dt[l], ssm_b_re[l], ssm_b_im[l],
                            ssm_c_re[l], ssm_c_im[l])
        y_s5 = _s5_scan(u, tables, n_lat, n_ctx)
        n_rows, y_d_ctx = n_lat, y_d
        if not last:
            y_d_ctx = global_attention(n_lat, n_ctx, n_lat, n_ctx, n_ctx, n_ctx)
            n_rows = t_all
        xs = _merge(xs, mod, l, row_of, norm_g[l, 1], xa, y_b, y_s5, u, y_d, y_d_ctx, w_in_l,
                    bf(_pool_block_diag(pool_w[l])), pool_scale[l], ssm_d[l], bf(glu_w[l]), bf(branch_w[l]),
                    bf(out_w[l]), n_rows, n_lat, n_ctx)
        xs = _ffn(xs, mod, l, 2, norm_g[l, 2], bf(ffn_in[l, 1]), bf(ffn_out[l, 1]), n_rows, n_lat,
                  final_g=final_g if last else None)
    return xs
```

```python
import functools
import math

import jax
import jax.numpy as jnp
import numpy as np
from jax import lax
from jax.experimental import pallas as pl
from jax.experimental.pallas import tpu as pltpu

F32 = jnp.float32
BF16 = jnp.bfloat16

D_MODEL = 1024
D_FF = 2816
N_SUB = 3
BRANCH_W = 256
HEAD_DIM = 64
GRID_W = 64
BLOCK = 128
EPS = 1e-6
ROPE_THETA = 10000.0
N_BRANCH = 4
POOL_WINDOWS = (2, 4, 8, 16)
SSM_GROUPS = 16
SSM_GROUP = 16
SSM_STATE = 64
N_STATE = SSM_GROUPS * SSM_STATE
PROJ_W = 6 * BRANCH_W
MOD_ROWS = 8
POOL_HALO = 8
SCAN_ROWS = 8
STATE_CHUNK = 128
WIN_BLOCKS = 2
NEG = -1e30
QK_SCALE = HEAD_DIM ** -0.5
LOG2E = math.log2(math.e)
V_ROWS = HEAD_DIM + 16
MAX_SCORE_BOUND = 40.0
TOKEN_TILE = 256
VMEM_LIMIT = 56 * 1024 * 1024


def _cparams(sem):
    return pltpu.CompilerParams(dimension_semantics=sem, vmem_limit_bytes=VMEM_LIMIT)


def _dot(a, b):
    return jnp.dot(a, b, preferred_element_type=F32)


def _dot_nt(a, b):
    return lax.dot_general(a, b, (((1,), (1,)), ((), ())), preferred_element_type=F32)


def _rms(x, g):
    ms = jnp.mean(x * x, axis=-1, keepdims=True)
    return x * lax.rsqrt(ms + EPS) * g


def _silu(x):
    return x * jax.nn.sigmoid(x)


def _const_spec(shape):
    nd = len(shape)
    return pl.BlockSpec(shape, lambda *_: (0,) * nd)


def _mod_kernel(s_ref, w_ref, b_ref, o_ref):
    s = _silu(s_ref[...])
    w = w_ref[0]
    s_hi = s.astype(BF16)
    s_lo = (s - s_hi.astype(F32)).astype(BF16)
    w_hi = w.astype(BF16)
    w_lo = (w - w_hi.astype(F32)).astype(BF16)
    o_ref[0] = _dot(s_hi, w_hi) + _dot(s_hi, w_lo) + _dot(s_lo, w_hi) + b_ref[0]


def _modulation(rows, w_mod, b_mod):
    depth, d, width = w_mod.shape
    tn = 1024
    out = pl.pallas_call(
        _mod_kernel,
        grid=(depth, width // tn),
        in_specs=[
            pl.BlockSpec((MOD_ROWS, d), lambda l, j: (0, 0)),
            pl.BlockSpec((1, d, tn), lambda l, j: (l, 0, j)),
            pl.BlockSpec((1, 1, tn), lambda l, j: (l, 0, j)),
        ],
        out_specs=pl.BlockSpec((1, MOD_ROWS, tn), lambda l, j: (l, 0, j)),
        out_shape=jax.ShapeDtypeStruct((depth, MOD_ROWS, width), F32),
        compiler_params=_cparams(("parallel", "parallel")),
        name="modulation",
    )(rows, w_mod, b_mod.reshape(depth, 1, width))
    return out.reshape(depth, MOD_ROWS, N_SUB * 3, 1, d).transpose(0, 2, 1, 3, 4)


def _mod_spec(layer, vec, row_of):
    return pl.BlockSpec((None, None, None, 1, D_MODEL), lambda b, t: (layer, vec, row_of(b, t), 0, 0))


def _mod_rows(layer, bsz):
    spec = lambda row_of: pl.BlockSpec((None, N_SUB * 3, None, 1, D_MODEL),
                                       lambda b, t: (layer, 0, row_of(b), 0, 0))
    return [spec(lambda b: b), spec(lambda b: bsz)]


def _tail_rows(main_ref, tail_ref, n_main_tiles):
    return jnp.where(pl.program_id(1) >= n_main_tiles, tail_ref[0], main_ref[0])


def _ffn_kernel(x_ref, lat_ref, ctx_ref, ng_ref, win_ref, wout_ref, *rest, sub, final, n_main_tiles, n_lat_tiles):
    o_ref = rest[-1]
    tiles = o_ref.shape[1] // TOKEN_TILE
    step = pl.program_id(1)
    for s in range(tiles):
        rows = slice(TOKEN_TILE * s, TOKEN_TILE * (s + 1))
        x = x_ref[0, rows] if n_main_tiles is None else _tail_rows(x_ref, rest[0], n_main_tiles)
        is_ctx = step * tiles + s >= n_lat_tiles
        shift, scale, gate = (jnp.where(is_ctx, ctx_ref[3 * sub + v], lat_ref[3 * sub + v]) for v in range(3))
        h = (_rms(x, ng_ref[...]) * (1.0 + scale) + shift).astype(BF16)
        gu = _dot(h, win_ref[...])
        a = (_silu(gu[:, :D_FF]) * gu[:, D_FF:]).astype(BF16)
        xn = x + (0.5 * gate) * _dot(a, wout_ref[...])
        if final:
            xn = _rms(xn, rest[0][...])
        o_ref[0, rows] = xn


def _tail_specs(tm, width, n_main_tiles):
    return (pl.BlockSpec((1, tm, width), lambda b, t: (b, jnp.minimum(t, n_main_tiles - 1), 0)),
            pl.BlockSpec((1, tm, width), lambda b, t: (b, jnp.maximum(t - n_main_tiles, 0), 0)))


def _ffn(x, mod, layer, sub, norm_g, w_in, w_out, n_rows, n_lat, final_g=None, tail=None):
    bsz = x.shape[0]
    final = final_g is not None
    assert not (final and tail is not None)
    n_tiles = n_rows // TOKEN_TILE
    per_step = 1 if tail is not None else _largest_divisor(n_tiles, (5, 4, 2, 1))
    tm = per_step * TOKEN_TILE
    n_main_tiles = None if tail is None else x.shape[1] // tm
    x_spec = pl.BlockSpec((1, tm, D_MODEL), lambda b, t: (b, t, 0))
    if tail is not None:
        x_spec, tail_spec = _tail_specs(tm, D_MODEL, n_main_tiles)
    in_specs = [
        x_spec,
        *_mod_rows(layer, bsz),
        _const_spec((1, D_MODEL)),
        pl.BlockSpec((D_MODEL, 2 * D_FF), lambda b, t: (0, 0), pipeline_mode=pl.Buffered(1)),
        pl.BlockSpec((D_FF, D_MODEL), lambda b, t: (0, 0), pipeline_mode=pl.Buffered(1)),
    ]
    args = [x, mod, mod, norm_g.reshape(1, D_MODEL), w_in, w_out]
    if final:
        in_specs.append(_const_spec((1, D_MODEL)))
        args.append(final_g.reshape(1, D_MODEL))
    if tail is not None:
        in_specs.append(tail_spec)
        args.append(tail)
    return pl.pallas_call(
        functools.partial(_ffn_kernel, sub=sub, final=final, n_main_tiles=n_main_tiles,
                          n_lat_tiles=n_lat // TOKEN_TILE),
        grid=(bsz, n_tiles // per_step),
        in_specs=in_specs,
        out_specs=pl.BlockSpec((1, tm, D_MODEL), lambda b, t: (b, t, 0)),
        out_shape=jax.ShapeDtypeStruct((bsz, n_rows, D_MODEL), F32),
        compiler_params=_cparams(("parallel", "parallel")),
        name="ffn_final" if final else "ffn",
    )(*args)


def _rope(x, cos, sin, first_half):
    w = x.shape[-1]
    swapped = jnp.where(first_half, pltpu.roll(x, w - 16, 1), pltpu.roll(x, 16, 1))
    return x * cos + swapped * sin


def _head_rms(x, g, low_head):
    x2 = x * x
    lo = jnp.sum(jnp.where(low_head, x2, 0.0), axis=-1, keepdims=True)
    hi = jnp.sum(jnp.where(low_head, 0.0, x2), axis=-1, keepdims=True)
    ms = jnp.where(low_head, lo, hi) * (1.0 / HEAD_DIM)
    return x * lax.rsqrt(ms + EPS) * g


def _store_padded(ref, x, low_head, rows):
    sw = pltpu.roll(x, HEAD_DIM, 1)
    zero = jnp.zeros_like(x)
    ref[0, 0, 0, rows] = jnp.where(low_head, x, zero).astype(ref.dtype)
    ref[0, 0, 1, rows] = jnp.where(low_head, zero, sw).astype(ref.dtype)
    ref[0, 1, 0, rows] = jnp.where(low_head, sw, zero).astype(ref.dtype)
    ref[0, 1, 1, rows] = jnp.where(low_head, zero, x).astype(ref.dtype)


def _proj_kernel(shift_ref, x_ref, lat_ref, ctx_ref, ng_ref, w_ref, cos_ref, sin_ref, qg_ref, kg_ref,
                 qw_ref, qt_ref, kw_ref, vw_ref, ka_ref, vt_ref, u_ref, xa_ref, *, n_lat_tiles):
    tm = TOKEN_TILE
    tiles = x_ref.shape[1] // tm
    step = pl.program_id(1)
    lane = lax.broadcasted_iota(jnp.int32, (tm, 2 * HEAD_DIM), 1)
    low_head = lane < HEAD_DIM
    first_half = jnp.bitwise_and(lane, 31) < 16
    first_row = lax.broadcasted_iota(jnp.int32, (HEAD_DIM, tm), 0) == 0
    shift_rows = jnp.where(first_row, shift_ref[0], 0.0)
    ones_rows = jnp.where(lax.broadcasted_iota(jnp.int32, (V_ROWS - HEAD_DIM, tm), 0) == 0, 1.0, 0.0)
    for s in range(tiles):
        rows = slice(tm * s, tm * (s + 1))
        x = x_ref[0, rows]
        is_ctx = step * tiles + s >= n_lat_tiles
        shift, scale = (jnp.where(is_ctx, ctx_ref[3 + v], lat_ref[3 + v]) for v in range(2))
        h = (_rms(x, ng_ref[...]) * (1.0 + scale) + shift).astype(BF16)
        p = _dot(h, w_ref[...])
        cos = cos_ref[rows]
        sin = sin_ref[rows]
        rope = lambda z: _rope(z, cos, sin, first_half)
        k_win = rope(p[:, 0:128])
        v_win = p[:, 128:256]
        u_ref[0, rows] = p[:, 256:512]
        k_glb = rope(_head_rms(p[:, 512:640], kg_ref[...], low_head))
        v_glb = p[:, 640:768]
        zero = jnp.zeros_like(k_glb)
        ka_ref[0, 0, rows] = jnp.where(low_head, k_glb, jnp.where(lane == HEAD_DIM, 1.0, zero)).astype(BF16)
        ka_ref[0, 1, rows] = jnp.where(low_head, jnp.where(lane == 0, 1.0, zero), k_glb).astype(BF16)
        for c in range(2):
            qw = rope(p[:, 768 + 128 * c:896 + 128 * c]) * QK_SCALE
            qw_ref[0, rows, 128 * c:128 * (c + 1)] = qw.astype(BF16)
            qd = rope(_head_rms(p[:, 1024 + 128 * c:1152 + 128 * c], qg_ref[...], low_head)) * (QK_SCALE * LOG2E)
            q_t = qd.T
            for g in range(2):
                own = q_t[HEAD_DIM * g:HEAD_DIM * (g + 1)]
                pair = [own, shift_rows] if c == 0 else [shift_rows, own]
                qt_ref[0, 2 * c + g, :, rows] = jnp.concatenate(pair, axis=0).astype(BF16)
        v_t = v_glb.T
        for j in range(2):
            vt_ref[0, j, :, rows] = jnp.concatenate([v_t[HEAD_DIM * j:HEAD_DIM * (j + 1)], ones_rows],
                                                    axis=0).astype(BF16)
        xa_ref[0, rows] = p[:, 1280:1536]
        _store_padded(kw_ref, k_win, low_head, rows)
        _store_padded(vw_ref, v_win, low_head, rows)


def _proj(x, mod, layer, norm_g, w, cos, sin, q_g, k_g, score_bound, n_lat):
    bsz, t_all, _ = x.shape
    tm = _largest_divisor(t_all // TOKEN_TILE, (5, 4, 2, 1)) * TOKEN_TILE
    tok = lambda width: pl.BlockSpec((1, tm, width), lambda b, t: (b, t, 0))
    pad = pl.BlockSpec((1, 2, 2, tm, 128), lambda b, t: (b, 0, 0, t, 0))
    tab = pl.BlockSpec((tm, 128), lambda b, t: (t, 0))
    pad_shape = jax.ShapeDtypeStruct((bsz, 2, 2, t_all, 128), BF16)
    return pl.pallas_call(
        functools.partial(_proj_kernel, n_lat_tiles=n_lat // TOKEN_TILE),
        grid=(bsz, t_all // tm),
        in_specs=[
            pl.BlockSpec(memory_space=pltpu.SMEM),
            tok(D_MODEL),
            *_mod_rows(layer, bsz),
            _const_spec((1, D_MODEL)),
            pl.BlockSpec((D_MODEL, PROJ_W), lambda b, t: (0, 0), pipeline_mode=pl.Buffered(1)),
            tab, tab,
            _const_spec((1, 128)), _const_spec((1, 128)),
        ],
        out_specs=[
            tok(256),
            pl.BlockSpec((1, 4, 128, tm), lambda b, t: (b, 0, 0, t)),
            pad, pad,
            pl.BlockSpec((1, 2, tm, 128), lambda b, t: (b, 0, t, 0)),
            pl.BlockSpec((1, 2, V_ROWS, tm), lambda b, t: (b, 0, 0, t)),
            tok(256), tok(256),
        ],
        out_shape=[
            jax.ShapeDtypeStruct((bsz, t_all, 256), BF16),
            jax.ShapeDtypeStruct((bsz, 4, 128, t_all), BF16),
            pad_shape, pad_shape,
            jax.ShapeDtypeStruct((bsz, 2, t_all, 128), BF16),
            jax.ShapeDtypeStruct((bsz, 2, V_ROWS, t_all), BF16),
            jax.ShapeDtypeStruct((bsz, t_all, 256), F32),
            jax.ShapeDtypeStruct((bsz, t_all, 256), F32),
        ],
        compiler_params=_cparams(("parallel", "parallel")),
        name="proj",
    )((-score_bound * LOG2E).reshape(1), x, mod, mod, norm_g.reshape(1, D_MODEL), w, cos, sin,
      jnp.tile(q_g, 2).reshape(1, 128), jnp.tile(k_g, 2).reshape(1, 128))


def _win_kernel(sink_ref, q_ref, kp_ref, kc_ref, kn_ref, kx_ref, vp_ref, vc_ref, vn_ref, vx_ref,
                o_ref, *, n_lat_blocks):
    i = pl.program_id(1)
    tile = WIN_BLOCKS * BLOCK
    band = tile + 2 * BLOCK
    tiles = q_ref.shape[1] // tile
    r = lax.broadcasted_iota(jnp.int32, (tile, band), 0)
    c = lax.broadcasted_iota(jnp.int32, (tile, band), 1)
    in_window = (c - r).astype(jnp.uint32) <= 2 * BLOCK

    def band_rows(edge_before, cur, edge_after, sub, j, g):
        first, last = tile * sub - BLOCK, tile * (sub + 1) + BLOCK
        parts = [edge_before[0, j, g]] if sub == 0 else []
        parts.append(cur[0, j, g, max(first, 0):min(last, tile * tiles)])
        if sub == tiles - 1:
            parts.append(edge_after[0, j, g])
        return parts

    for sub in range(tiles):
        rows = slice(tile * sub, tile * (sub + 1))
        blk0 = (i * tiles + sub) * WIN_BLOCKS
        lo = jnp.where(blk0 >= 1, 0, BLOCK)
        hi = jnp.where(blk0 < n_lat_blocks, jnp.minimum((n_lat_blocks - blk0 + 1) * BLOCK, band), 0)
        in_range = (c - lo).astype(jnp.uint32) < (hi - lo).astype(jnp.uint32)
        bias = jnp.where(in_window, jnp.where(in_range, 0.0, NEG), NEG)
        for j in range(2):
            q = q_ref[0, rows, 128 * j:128 * (j + 1)]
            acc = jnp.zeros((tile, 128), F32)
            for g in range(2):
                k_band = jnp.concatenate(band_rows(kp_ref, kc_ref, kn_ref, sub, j, g), axis=0)
                s = jnp.concatenate([_dot_nt(q, k_band) + bias, _dot_nt(q, kx_ref[0, j, g])], axis=1)
                v = jnp.concatenate(band_rows(vp_ref, vc_ref, vn_ref, sub, j, g) + [vx_ref[0, j, g]], axis=0)
                sink = sink_ref[2 * j + g]
                m = jnp.maximum(s.max(-1, keepdims=True), sink)
                p = jnp.exp(s - m)
                den = p.sum(-1, keepdims=True) + jnp.exp(sink - m)
                acc = acc + _dot(p.astype(BF16), v) / den
            o_ref[0, rows, 128 * j:128 * (j + 1)] = acc.astype(o_ref.dtype)


def _window_attention(q, k_pad, v_pad, sink, n_lat, n_ctx, with_ctx):
    bsz, t_all, _ = q.shape
    n_lat_blocks = n_lat // BLOCK
    n_blocks = t_all // BLOCK
    n_q_blocks = n_blocks if with_ctx else n_lat_blocks
    assert n_q_blocks % WIN_BLOCKS == 0 and n_lat_blocks % WIN_BLOCKS == 0
    n_tiles = n_q_blocks // WIN_BLOCKS
    per_step = _largest_divisor(n_tiles, (5, 4, 2, 1))
    step_blocks = per_step * WIN_BLOCKS
    tile = step_blocks * BLOCK
    edge = lambda off: pl.BlockSpec(
        (1, 2, 2, BLOCK, 128), lambda b, i: (b, 0, 0, jnp.clip(i * step_blocks + off, 0, n_blocks - 1), 0))
    cur = pl.BlockSpec((1, 2, 2, tile, 128), lambda b, i: (b, 0, 0, i, 0))
    ctx = pl.BlockSpec((1, 2, 2, n_ctx, 128), lambda b, i: (b, 0, 0, n_lat // n_ctx, 0))
    return pl.pallas_call(
        functools.partial(_win_kernel, n_lat_blocks=n_lat_blocks),
        grid=(bsz, n_tiles // per_step),
        in_specs=[
            pl.BlockSpec(memory_space=pltpu.SMEM),
            pl.BlockSpec((1, tile, 256), lambda b, i: (b, i, 0)),
            edge(-1), cur, edge(step_blocks), ctx,
            edge(-1), cur, edge(step_blocks), ctx,
        ],
        out_specs=pl.BlockSpec((1, tile, 256), lambda b, i: (b, i, 0)),
        out_shape=jax.ShapeDtypeStruct((bsz, n_q_blocks * BLOCK, 256), BF16),
        compiler_params=_cparams(("parallel", "parallel")),
        name="window_attention",
    )(sink, q, k_pad, k_pad, k_pad, k_pad, v_pad, v_pad, v_pad, v_pad)


def _glb_finish(acc_ref, o_ref):
    o_t = jnp.concatenate([acc_ref[g, 0:HEAD_DIM] / acc_ref[g, HEAD_DIM:HEAD_DIM + 1] for g in range(2)],
                          axis=0)
    o_ref[0] = o_t.T.astype(o_ref.dtype)


def _glb_bounded_kernel(qt_ref, k_ref, vt_ref, o_ref, acc_ref, *, tk):
    for ki in range(k_ref.shape[2] // tk):
        keys = slice(ki * tk, (ki + 1) * tk)
        for g in range(2):
            p = jnp.exp2(_dot(k_ref[0, 0, keys], qt_ref[0, g])).astype(BF16)
            pv = _dot(vt_ref[0, 0, :, keys], p)
            acc_ref[g] = pv if ki == 0 else acc_ref[g] + pv
    _glb_finish(acc_ref, o_ref)


def _glb_online_kernel(qt_ref, k_ref, vt_ref, o_ref, acc_ref, m_ref):
    ki = pl.program_id(3)

    @pl.when(ki == 0)
    def _():
        acc_ref[...] = jnp.zeros_like(acc_ref)
        m_ref[...] = jnp.full_like(m_ref, NEG)

    for g in range(2):
        s = _dot(k_ref[0, 0], qt_ref[0, g])
        m_prev = m_ref[g, 0:1]
        m_new = jnp.maximum(m_prev, s.max(axis=0, keepdims=True))
        p = jnp.exp2(s - m_new).astype(BF16)
        acc_ref[g] = jnp.exp2(m_prev - m_new) * acc_ref[g] + _dot(vt_ref[0, 0], p)
        m_ref[g] = jnp.broadcast_to(m_new, m_ref.shape[1:])

    @pl.when(ki == pl.num_programs(3) - 1)
    def _():
        _glb_finish(acc_ref, o_ref)


def _global_attention(qt, k_aug, vt_aug, q_row0, n_q, k_row0, n_k, tq, tk, bounded):
    bsz = qt.shape[0]
    q0, k0 = q_row0 // tq, k_row0 // tk
    scratch = [pltpu.VMEM((2, V_ROWS, tq), F32)]
    if bounded:
        once = dict(pipeline_mode=pl.Buffered(1))
        return pl.pallas_call(
            functools.partial(_glb_bounded_kernel, tk=tk),
            grid=(bsz, 2, n_q // tq),
            in_specs=[
                pl.BlockSpec((1, 2, 128, tq), lambda b, j, qi: (b, j, 0, q0 + qi)),
                pl.BlockSpec((1, 1, n_k, 128), lambda b, j, qi: (b, j, k_row0 // n_k, 0), **once),
                pl.BlockSpec((1, 1, V_ROWS, n_k), lambda b, j, qi: (b, j, 0, k_row0 // n_k), **once),
            ],
            out_specs=pl.BlockSpec((1, tq, 128), lambda b, j, qi: (b, qi, j)),
            out_shape=jax.ShapeDtypeStruct((bsz, n_q, 256), BF16),
            scratch_shapes=scratch,
            compiler_params=_cparams(("parallel", "parallel", "parallel")),
            name="global_attention",
        )(qt, k_aug, vt_aug)
    scratch.append(pltpu.VMEM((2, 8, tq), F32))
    return pl.pallas_call(
        _glb_online_kernel,
        grid=(bsz, 2, n_q // tq, n_k // tk),
        in_specs=[
            pl.BlockSpec((1, 2, 128, tq), lambda b, j, qi, ki: (b, j, 0, q0 + qi)),
            pl.BlockSpec((1, 1, tk, 128), lambda b, j, qi, ki: (b, j, k0 + ki, 0)),
            pl.BlockSpec((1, 1, V_ROWS, tk), lambda b, j, qi, ki: (b, j, 0, k0 + ki)),
        ],
        out_specs=pl.BlockSpec((1, tq, 128), lambda b, j, qi, ki: (b, qi, j)),
        out_shape=jax.ShapeDtypeStruct((bsz, n_q, 256), BF16),
        scratch_shapes=scratch,
        compiler_params=_cparams(("parallel", "parallel", "parallel", "arbitrary")),
        name="global_attention_online",
    )(qt, k_aug, vt_aug)


def _s5_tile(u, bmat_ref, cmat_ref, a_ref, pc_ref, carry_ref, reverse):
    rows = u.shape[0]
    n_groups = rows // SCAN_ROWS
    w = STATE_CHUNK
    ub = u.astype(BF16)
    order = range(n_groups - 1, -1, -1) if reverse else range(n_groups)
    last = 0 if reverse else SCAN_ROWS - 1
    y = None
    for c in range(N_STATE // w):
        bu = _dot(ub, bmat_ref[0, c])
        pr, pi = pc_ref[0, c, :, :w], pc_ref[0, c, :, w:]
        cr, ci = carry_ref[c, :, :w], carry_ref[c, :, w:]
        states = [None] * n_groups
        for g in order:
            vr = bu[SCAN_ROWS * g:SCAN_ROWS * (g + 1), :w]
            vi = bu[SCAN_ROWS * g:SCAN_ROWS * (g + 1), w:]
            for s, k in enumerate((1, 2, 4)):
                ar, ai = a_ref[0, s, c, :, :w], a_ref[0, s, c, :, w:]
                shift = SCAN_ROWS - k if reverse else k
                sr = pltpu.roll(vr, shift, 0)
                si = pltpu.roll(vi, shift, 0)
                vr, vi = vr + ar * sr - ai * si, vi + ar * si + ai * sr
            vr, vi = vr + pr * cr - pi * ci, vi + pr * ci + pi * cr
            cr = jnp.broadcast_to(vr[last:last + 1], (SCAN_ROWS, w))
            ci = jnp.broadcast_to(vi[last:last + 1], (SCAN_ROWS, w))
            states[g] = jnp.concatenate([vr, vi], axis=1)
        carry_ref[c, :, :w] = cr
        carry_ref[c, :, w:] = ci
        part = _dot(jnp.concatenate(states, axis=0).astype(BF16), cmat_ref[0, c])
        y = part if y is None else y + part
    return y


def _s5_kernel(ul_ref, uc_ref, bmat_ref, cmat_ref, a_ref, pc_ref, y_ref, yc_ref, carry_ref):
    direction = pl.program_id(0)
    step = pl.program_id(2)
    tiles = ul_ref.shape[1] // TOKEN_TILE
    for reverse in (False, True):
        @pl.when(direction == int(reverse))
        def _(reverse=reverse):
            tile = functools.partial(_s5_tile, bmat_ref=bmat_ref, cmat_ref=cmat_ref, a_ref=a_ref,
                                     pc_ref=pc_ref, carry_ref=carry_ref, reverse=reverse)

            @pl.when(step == 0)
            def _():
                carry_ref[...] = jnp.zeros_like(carry_ref)
                yc_ref[0, 0] = tile(uc_ref[0])

            for s in (range(tiles - 1, -1, -1) if reverse else range(tiles)):
                rows = slice(TOKEN_TILE * s, TOKEN_TILE * (s + 1))
                y_ref[0, 0, rows] = tile(ul_ref[0, rows])


def _s5_tables(a_re, a_im, log_dt, b_re, b_im, c_re, c_im):
    dt = jnp.exp(log_dt)[..., None]
    re_dt, im_dt = a_re * dt, a_im * dt
    lb_re, lb_im = jnp.exp(re_dt) * jnp.cos(im_dt), jnp.exp(re_dt) * jnp.sin(im_dt)
    den = a_re * a_re + a_im * a_im
    f_re = ((lb_re - 1.0) * a_re + lb_im * a_im) / den
    f_im = (lb_im * a_re - (lb_re - 1.0) * a_im) / den
    bb_re = f_re[..., None] * b_re - f_im[..., None] * b_im
    bb_im = f_re[..., None] * b_im + f_im[..., None] * b_re
    eye = jnp.eye(SSM_GROUPS, dtype=F32)
    in_w, st_w, n_chunks = SSM_GROUPS * SSM_GROUP, N_STATE, N_STATE // STATE_CHUNK
    b_blk = lambda z: jnp.einsum('dgph,gk->dghkp', z, eye).reshape(2, in_w, n_chunks, STATE_CHUNK)
    bmat = jnp.concatenate([b_blk(bb_re), b_blk(bb_im)], axis=-1).transpose(0, 2, 1, 3).astype(BF16)
    c_blk = lambda z: jnp.einsum('dghp,gk->dgpkh', z, eye).reshape(2, n_chunks, STATE_CHUNK, in_w)
    cmat = jnp.concatenate([c_blk(c_re), -c_blk(c_im)], axis=2).astype(BF16)
    steps = jnp.arange(1, SCAN_ROWS + 1, dtype=F32)[None, :, None, None]
    mag = jnp.exp(re_dt[:, None] * steps)
    ang = im_dt[:, None] * steps
    chunked = lambda z: z.reshape(2, SCAN_ROWS, n_chunks, STATE_CHUNK)
    powers = jnp.concatenate([chunked(mag * jnp.cos(ang)), chunked(mag * jnp.sin(ang))], axis=-1)
    row = jnp.arange(SCAN_ROWS)[:, None, None]
    pc = jnp.stack([powers[0], powers[1][::-1]]).transpose(0, 2, 1, 3)
    a_tabs = []
    for k in (1, 2, 4):
        fwd = jnp.where(row >= k, powers[0, k - 1][None], 0.0)
        bwd = jnp.where(row <= SCAN_ROWS - 1 - k, powers[1, k - 1][None], 0.0)
        a_tabs.append(jnp.stack([fwd, bwd]).transpose(0, 2, 1, 3))
    return bmat, cmat, jnp.stack(a_tabs, axis=1), pc


def _s5_scan(u, tables, n_lat, n_ctx):
    bsz, t_all, width = u.shape
    assert n_ctx == TOKEN_TILE
    n_tiles = n_lat // TOKEN_TILE
    per_step = _largest_divisor(n_tiles, (4, 2, 1))
    tm = per_step * TOKEN_TILE
    n_steps = n_tiles // per_step
    bmat, cmat, a_tab, pc = tables
    n_chunks, lanes = N_STATE // STATE_CHUNK, 2 * STATE_CHUNK
    block_of = lambda d, s: s + d * (n_steps - 1 - 2 * s)
    return pl.pallas_call(
        _s5_kernel,
        grid=(2, bsz, n_steps),
        in_specs=[
            pl.BlockSpec((1, tm, width), lambda d, b, s: (b, block_of(d, s), 0)),
            pl.BlockSpec((1, n_ctx, width), lambda d, b, s: (b, n_lat // n_ctx, 0)),
            pl.BlockSpec((1, n_chunks, width, lanes), lambda d, b, s: (d, 0, 0, 0)),
            pl.BlockSpec((1, n_chunks, lanes, width), lambda d, b, s: (d, 0, 0, 0)),
            pl.BlockSpec((1, 3, n_chunks, SCAN_ROWS, lanes), lambda d, b, s: (d, 0, 0, 0, 0)),
            pl.BlockSpec((1, n_chunks, SCAN_ROWS, lanes), lambda d, b, s: (d, 0, 0, 0)),
        ],
        out_specs=[
            pl.BlockSpec((1, 1, tm, width), lambda d, b, s: (d, b, block_of(d, s), 0)),
            pl.BlockSpec((1, 1, n_ctx, width), lambda d, b, s: (d, b, 0, 0)),
        ],
        out_shape=[
            jax.ShapeDtypeStruct((2, bsz, n_lat, width), F32),
            jax.ShapeDtypeStruct((2, bsz, n_ctx, width), F32),
        ],
        scratch_shapes=[pltpu.VMEM((n_chunks, SCAN_ROWS, lanes), F32)],
        compiler_params=_cparams(("parallel", "parallel", "arbitrary")),
        name="s5_scan",
    )(u, u, bmat, cmat, a_tab, pc)


def _gelu_tanh(x):
    c = math.sqrt(2.0 / math.pi)
    return x * (0.5 * (1.0 + jnp.tanh(c * (x + 0.044715 * (x * x * x)))))


def _merge_kernel(x_ref, sh_ref, sc_ref, gt_ref, ng_ref, xa_ref, xap_ref, xan_ref, yb_ref, yf_ref, yr_ref,
                  yft_ref, yrt_ref, u_ref, yd_ref, ydt_ref, wg_ref, pw_ref, ps_ref, dsk_ref, glu_ref, wb_ref,
                  wo_ref, o_ref, ext_ref, *, n_lat_tiles, n_lat, n_ctx):
    t = pl.program_id(1)
    x = x_ref[0]
    tm = x.shape[0]
    bw = BRANCH_W
    h = (_rms(x, ng_ref[...]) * (1.0 + sc_ref[...]) + sh_ref[...]).astype(BF16)
    gates = [jax.nn.sigmoid(_dot(h, wg_ref[:, PROJ_W + D_MODEL * k:PROJ_W + D_MODEL * (k + 1)]))
             for k in range(N_BRANCH)]
    is_ctx = t >= n_lat_tiles
    seg_tile = jnp.where(is_ctx, t - n_lat_tiles, t)
    seg_tiles = jnp.where(is_ctx, n_ctx // tm, n_lat_tiles)
    seg_len = jnp.where(is_ctx, n_ctx, n_lat)
    xa = xa_ref[0]
    halo, span = POOL_HALO, tm + 2 * POOL_HALO
    for k in range(4):
        ext_ref[k, 0:halo, :] = jnp.zeros((halo, bw), F32)
        ext_ref[k, halo + span:, :] = jnp.zeros((halo, bw), F32)
    ext_ref[0, halo:2 * halo, :] = jnp.where(seg_tile > 0, xap_ref[0], 0.0)
    ext_ref[0, 2 * halo:2 * halo + tm, :] = xa
    ext_ref[0, 2 * halo + tm:halo + span, :] = jnp.where(seg_tile + 1 < seg_tiles, xan_ref[0], 0.0)
    wide = lambda k, off: ext_ref[k, halo + off:halo + off + span, :]
    ext_ref[1, halo:halo + span, :] = wide(0, -1) + wide(0, 0)
    ext_ref[2, halo:halo + span, :] = wide(1, -1) + wide(1, 1)
    ext_ref[3, halo:halo + span, :] = wide(2, -2) + wide(2, 2)
    own = lambda k, off: ext_ref[k, 2 * halo + off:2 * halo + off + tm, :]
    sums = [own(1, 0), own(2, 0), own(3, 0), own(3, -4) + own(3, 4)]
    lane = lax.broadcasted_iota(jnp.int32, (tm, bw), 1)
    pos = seg_tile * tm + lax.broadcasted_iota(jnp.int32, (tm, bw), 0)
    group = lane // (bw // len(POOL_WINDOWS))
    half_w = jnp.left_shift(1, group)
    count = jnp.minimum(pos + half_w, seg_len) - jnp.maximum(pos - half_w, 0)
    window_sum = jnp.where(group == 0, sums[0], jnp.where(group == 1, sums[1],
                                                           jnp.where(group == 2, sums[2], sums[3])))
    diff = window_sum / count.astype(F32) - xa
    y_a = _dot(diff.astype(BF16), pw_ref[...]) * ps_ref[...]
    y_f = jnp.where(is_ctx, yft_ref[0, 0], yf_ref[0, 0])
    y_r = jnp.where(is_ctx, yrt_ref[0, 0], yr_ref[0, 0])
    y_s = y_f + y_r + dsk_ref[...] * u_ref[0]
    z = _dot(_gelu_tanh(y_s).astype(BF16), glu_ref[...])
    y_c = z[:, :bw] * jax.nn.sigmoid(z[:, bw:])
    branches = (y_a.astype(BF16), yb_ref[0], y_c.astype(BF16), _tail_rows(yd_ref, ydt_ref, n_lat_tiles))
    total = None
    for gate, y_k, k in zip(gates, branches, range(N_BRANCH)):
        term = gate * _dot(y_k, wb_ref[k])
        total = term if total is None else total + term
    o_ref[0] = x + gt_ref[...] * _dot(total.astype(BF16), wo_ref[...])


def _merge(x, mod, layer, row_of, norm_g, xa, y_b, y_s5, y_s5_tail, u, y_d, y_d_tail, w_gate, pool_bd, pool_scale,
           d_skip, glu_w, branch_w, out_w, n_rows, n_lat, n_ctx):
    bsz = x.shape[0]
    tm = TOKEN_TILE
    assert n_ctx % tm == 0 or n_rows == n_lat
    halo_blocks = xa.shape[1] // POOL_HALO
    per_tile = tm // POOL_HALO
    tok = lambda width: pl.BlockSpec((1, tm, width), lambda b, t: (b, t, 0))
    single = dict(pipeline_mode=pl.Buffered(1))
    return pl.pallas_call(
        functools.partial(_merge_kernel, n_lat_tiles=n_lat // tm, n_lat=n_lat, n_ctx=n_ctx),
        grid=(bsz, n_rows // tm),
        in_specs=[
            tok(D_MODEL),
            _mod_spec(layer, 3, row_of), _mod_spec(layer, 4, row_of), _mod_spec(layer, 5, row_of),
            _const_spec((1, D_MODEL)),
            tok(256),
            pl.BlockSpec((1, POOL_HALO, 256), lambda b, t: (b, jnp.maximum(t * per_tile - 1, 0), 0)),
            pl.BlockSpec((1, POOL_HALO, 256),
                         lambda b, t: (b, jnp.minimum((t + 1) * per_tile, halo_blocks - 1), 0)),
            tok(256),
            *[pl.BlockSpec((1, 1, tm, 256), lambda b, t, d=d: (d, b, jnp.minimum(t, n_lat // tm - 1), 0))
              for d in range(2)],
            *[pl.BlockSpec((1, 1, tm, 256), lambda b, t, d=d: (d, b, 0, 0)) for d in range(2)],
            tok(256), *_tail_specs(tm, 256, n_lat // tm),
            pl.BlockSpec((D_MODEL, PROJ_W + N_BRANCH * D_MODEL), lambda b, t: (0, 0), **single),
            _const_spec((256, 256)), _const_spec((1, 256)), _const_spec((1, 256)),
            _const_spec((256, 512)),
            pl.BlockSpec((N_BRANCH, 256, D_MODEL), lambda b, t: (0, 0, 0), **single),
            pl.BlockSpec((D_MODEL, D_MODEL), lambda b, t: (0, 0), **single),
        ],
        out_specs=tok(D_MODEL),
        out_shape=jax.ShapeDtypeStruct((bsz, n_rows, D_MODEL), F32),
        scratch_shapes=[pltpu.VMEM((4, tm + 4 * POOL_HALO, 256), F32)],
        compiler_params=_cparams(("parallel", "parallel")),
        name="merge",
    )(x, mod, mod, mod, norm_g.reshape(1, D_MODEL), xa, xa, xa, y_b, y_s5, y_s5, y_s5_tail, y_s5_tail, u, y_d, y_d_tail,
      w_gate, pool_bd, pool_scale.reshape(1, 256), d_skip.reshape(1, 256), glu_w, branch_w, out_w)


def _rope_tables(n_lat, n_ctx):
    rows = n_lat // GRID_W
    n_freq = HEAD_DIM // 4
    row = jnp.repeat(jnp.arange(rows), GRID_W)
    col = jnp.tile(jnp.arange(GRID_W), rows)
    inv = ROPE_THETA ** (-jnp.arange(n_freq, dtype=F32) / n_freq)
    ang = jnp.stack([row, col], axis=-1).astype(F32)[..., None] * inv
    cos = jnp.broadcast_to(jnp.cos(ang)[:, :, None, :], (n_lat, 2, 2, n_freq)).reshape(n_lat, HEAD_DIM)
    sign = jnp.array([-1.0, 1.0], F32)[None, None, :, None]
    sin = (jnp.sin(ang)[:, :, None, :] * sign).reshape(n_lat, HEAD_DIM)
    cos = jnp.concatenate([cos, jnp.ones((n_ctx, HEAD_DIM), F32)], axis=0)
    sin = jnp.concatenate([sin, jnp.zeros((n_ctx, HEAD_DIM), F32)], axis=0)
    return jnp.tile(cos, (1, 2)), jnp.tile(sin, (1, 2))


def _largest_divisor(n, candidates):
    for c in candidates:
        if n % c == 0:
            return c
    raise ValueError(f"no tile in {candidates} divides {n}")


def _pool_block_diag(pool_w):
    groups, width, _ = pool_w.shape
    eye = jnp.eye(groups, dtype=pool_w.dtype)
    return jnp.einsum('gcd,gk->gckd', pool_w, eye).reshape(groups * width, groups * width)


def kernel(x, c, ctx, c_ctx, w_mod, b_mod, norm_g, ffn_in, ffn_out, w_in, win_sink, qk_norm, pool_w, pool_scale, ssm_a_re, ssm_a_im, ssm_log_dt, ssm_b_re, ssm_b_im, ssm_c_re, ssm_c_im, ssm_d, glu_w, branch_w, out_w, final_g):
    bsz, n_lat, d = x.shape
    n_ctx = ctx.shape[1]
    depth = w_mod.shape[0]
    t_all = n_lat + n_ctx
    assert d == D_MODEL and bsz < MOD_ROWS and n_lat % TOKEN_TILE == 0 and n_ctx == TOKEN_TILE
    assert n_lat % GRID_W == 0

    cond_rows = jnp.zeros((MOD_ROWS, d), F32).at[:bsz].set(c).at[bsz].set(c_ctx)
    mod = _modulation(cond_rows, w_mod, b_mod)
    n_lat_tiles = n_lat // TOKEN_TILE
    row_of = lambda b, t: jnp.where(t >= n_lat_tiles, bsz, b)

    cos, sin = _rope_tables(n_lat, n_ctx)
    tq = _largest_divisor(n_lat, (1024, 512, 256))
    tk = _largest_divisor(t_all, (3328, 1280, 1024, 768, 512, 256))

    xs = x
    for l in range(depth):
        last = l == depth - 1
        bf = lambda w: w.astype(BF16)
        xs = _ffn(xs, mod, l, 0, norm_g[l, 0], bf(ffn_in[l, 0]), bf(ffn_out[l, 0]), t_all, n_lat,
                  tail=ctx if l == 0 else None)

        score_bound = 1.02 * HEAD_DIM ** 0.5 * jnp.max(jnp.abs(qk_norm[l, 0])) * jnp.max(jnp.abs(qk_norm[l, 1]))
        w_in_l = bf(w_in[l])
        qw, qt, kw, vw, ka, vt, u, xa = _proj(xs, mod, l, norm_g[l, 1], w_in_l, cos, sin,
                                              qk_norm[l, 0], qk_norm[l, 1], score_bound, n_lat)
        y_b = _window_attention(qw, kw, vw, win_sink[l], n_lat, n_ctx, not last)

        def global_attention(q_row0, n_q, k_row0, n_k, tq_, tk_):
            run = lambda bounded: lambda: _global_attention(qt, ka, vt, q_row0, n_q, k_row0, n_k, tq_, tk_,
                                                            bounded)
            return lax.cond(score_bound <= MAX_SCORE_BOUND, run(True), run(False))

        y_d = global_attention(0, n_lat, 0, t_all, tq, tk)
        tables = _s5_tables(ssm_a_re[l], ssm_a_im[l], ssm_log_dt[l], ssm_b_re[l], ssm_b_im[l],
                            ssm_c_re[l], ssm_c_im[l])
        y_s5, y_s5_ctx = _s5_scan(u, tables, n_lat, n_ctx)
        n_rows, y_d_ctx = n_lat, y_d
        if not last:
            y_d_ctx = global_attention(n_lat, n_ctx, n_lat, n_ctx, n_ctx, n_ctx)
            n_rows = t_all
        xs = _merge(xs, mod, l, row_of, norm_g[l, 1], xa, y_b, y_s5, y_s5_ctx, u, y_d, y_d_ctx, w_in_l,
                    bf(_pool_block_diag(pool_w[l])), pool_scale[l], ssm_d[l], bf(glu_w[l]), bf(branch_w[l]),
                    bf(out_w[l]), n_rows, n_lat, n_ctx)
        xs = _ffn(xs, mod, l, 2, norm_g[l, 2], bf(ffn_in[l, 1]), bf(ffn_out[l, 1]), n_rows, n_lat,
                  final_g=final_g if last else None)
    return xs
```

```python
import functools
import math

import jax
import jax.numpy as jnp
import numpy as np
from jax import lax
from jax.experimental import pallas as pl
from jax.experimental.pallas import tpu as pltpu

F32 = jnp.float32
BF16 = jnp.bfloat16

D_MODEL = 1024
D_FF = 2816
N_SUB = 3
BRANCH_W = 256
HEAD_DIM = 64
GRID_W = 64
BLOCK = 128
EPS = 1e-6
ROPE_THETA = 10000.0
N_BRANCH = 4
POOL_WINDOWS = (2, 4, 8, 16)
SSM_GROUPS = 16
SSM_GROUP = 16
SSM_STATE = 64
N_STATE = SSM_GROUPS * SSM_STATE
PROJ_W = 6 * BRANCH_W
MOD_ROWS = 8
POOL_HALO = 8
SCAN_ROWS = 8
STATE_CHUNK = 128
WIN_BLOCKS = 2
NEG = -1e30
QK_SCALE = HEAD_DIM ** -0.5
LOG2E = math.log2(math.e)
V_ROWS = HEAD_DIM + 16
MAX_SCORE_BOUND = 40.0
TOKEN_TILE = 256
VMEM_LIMIT = 56 * 1024 * 1024


def _cparams(sem):
    return pltpu.CompilerParams(dimension_semantics=sem, vmem_limit_bytes=VMEM_LIMIT)


def _dot(a, b):
    return jnp.dot(a, b, preferred_element_type=F32)


def _dot_nt(a, b):
    return lax.dot_general(a, b, (((1,), (1,)), ((), ())), preferred_element_type=F32)


def _rms(x, g):
    ms = jnp.mean(x * x, axis=-1, keepdims=True)
    return x * lax.rsqrt(ms + EPS) * g


def _silu(x):
    return x * jax.nn.sigmoid(x)


def _const_spec(shape):
    nd = len(shape)
    return pl.BlockSpec(shape, lambda *_: (0,) * nd)


def _mod_kernel(s_ref, w_ref, b_ref, o_ref):
    s = _silu(s_ref[...])
    w = w_ref[0]
    s_hi = s.astype(BF16)
    s_lo = (s - s_hi.astype(F32)).astype(BF16)
    w_hi = w.astype(BF16)
    w_lo = (w - w_hi.astype(F32)).astype(BF16)
    o_ref[0] = _dot(s_hi, w_hi) + _dot(s_hi, w_lo) + _dot(s_lo, w_hi) + b_ref[0]


def _modulation(rows, w_mod, b_mod):
    depth, d, width = w_mod.shape
    tn = 1024
    out = pl.pallas_call(
        _mod_kernel,
        grid=(depth, width // tn),
        in_specs=[
            pl.BlockSpec((MOD_ROWS, d), lambda l, j: (0, 0)),
            pl.BlockSpec((1, d, tn), lambda l, j: (l, 0, j)),
            pl.BlockSpec((1, 1, tn), lambda l, j: (l, 0, j)),
        ],
        out_specs=pl.BlockSpec((1, MOD_ROWS, tn), lambda l, j: (l, 0, j)),
        out_shape=jax.ShapeDtypeStruct((depth, MOD_ROWS, width), F32),
        compiler_params=_cparams(("parallel", "parallel")),
        name="modulation",
    )(rows, w_mod, b_mod.reshape(depth, 1, width))
    return out.reshape(depth, MOD_ROWS, N_SUB * 3, 1, d).transpose(0, 2, 1, 3, 4)


def _mod_spec(layer, vec, row_of):
    return pl.BlockSpec((None, None, None, 1, D_MODEL), lambda b, t: (layer, vec, row_of(b, t), 0, 0))


def _mod_rows(layer, bsz):
    spec = lambda row_of: pl.BlockSpec((None, N_SUB * 3, None, 1, D_MODEL),
                                       lambda b, t: (layer, 0, row_of(b), 0, 0))
    return [spec(lambda b: b), spec(lambda b: bsz)]


def _tail_rows(main_ref, tail_ref, n_main_tiles):
    return jnp.where(pl.program_id(1) >= n_main_tiles, tail_ref[0], main_ref[0])


def _ffn_kernel(x_ref, lat_ref, ctx_ref, ng_ref, win_ref, wout_ref, *rest, sub, final, n_main_tiles, n_lat_tiles):
    o_ref = rest[-1]
    tiles = o_ref.shape[1] // TOKEN_TILE
    step = pl.program_id(1)
    for s in range(tiles):
        rows = slice(TOKEN_TILE * s, TOKEN_TILE * (s + 1))
        x = x_ref[0, rows] if n_main_tiles is None else _tail_rows(x_ref, rest[0], n_main_tiles)
        is_ctx = step * tiles + s >= n_lat_tiles
        shift, scale, gate = (jnp.where(is_ctx, ctx_ref[3 * sub + v], lat_ref[3 * sub + v]) for v in range(3))
        h = (_rms(x, ng_ref[...]) * (1.0 + scale) + shift).astype(BF16)
        gu = _dot(h, win_ref[...])
        a = (_silu(gu[:, :D_FF]) * gu[:, D_FF:]).astype(BF16)
        xn = x + (0.5 * gate) * _dot(a, wout_ref[...])
        if final:
            xn = _rms(xn, rest[0][...])
        o_ref[0, rows] = xn


def _tail_specs(tm, width, n_main_tiles):
    return (pl.BlockSpec((1, tm, width), lambda b, t: (b, jnp.minimum(t, n_main_tiles - 1), 0)),
            pl.BlockSpec((1, tm, width), lambda b, t: (b, jnp.maximum(t - n_main_tiles, 0), 0)))


def _ffn(x, mod, layer, sub, norm_g, w_in, w_out, n_rows, n_lat, final_g=None, tail=None):
    bsz = x.shape[0]
    final = final_g is not None
    assert not (final and tail is not None)
    n_tiles = n_rows // TOKEN_TILE
    per_step = 1 if tail is not None else _largest_divisor(n_tiles, (5, 4, 2, 1))
    tm = per_step * TOKEN_TILE
    n_main_tiles = None if tail is None else x.shape[1] // tm
    x_spec = pl.BlockSpec((1, tm, D_MODEL), lambda b, t: (b, t, 0))
    if tail is not None:
        x_spec, tail_spec = _tail_specs(tm, D_MODEL, n_main_tiles)
    in_specs = [
        x_spec,
        *_mod_rows(layer, bsz),
        _const_spec((1, D_MODEL)),
        pl.BlockSpec((D_MODEL, 2 * D_FF), lambda b, t: (0, 0), pipeline_mode=pl.Buffered(1)),
        pl.BlockSpec((D_FF, D_MODEL), lambda b, t: (0, 0), pipeline_mode=pl.Buffered(1)),
    ]
    args = [x, mod, mod, norm_g.reshape(1, D_MODEL), w_in, w_out]
    if final:
        in_specs.append(_const_spec((1, D_MODEL)))
        args.append(final_g.reshape(1, D_MODEL))
    if tail is not None:
        in_specs.append(tail_spec)
        args.append(tail)
    return pl.pallas_call(
        functools.partial(_ffn_kernel, sub=sub, final=final, n_main_tiles=n_main_tiles,
                          n_lat_tiles=n_lat // TOKEN_TILE),
        grid=(bsz, n_tiles // per_step),
        in_specs=in_specs,
        out_specs=pl.BlockSpec((1, tm, D_MODEL), lambda b, t: (b, t, 0)),
        out_shape=jax.ShapeDtypeStruct((bsz, n_rows, D_MODEL), F32),
        compiler_params=_cparams(("parallel", "parallel")),
        name="ffn_final" if final else "ffn",
    )(*args)


def _rope(x, cos, sin, first_half):
    w = x.shape[-1]
    swapped = jnp.where(first_half, pltpu.roll(x, w - 16, 1), pltpu.roll(x, 16, 1))
    return x * cos + swapped * sin


def _head_rms(x, g, low_head):
    x2 = x * x
    lo = jnp.sum(jnp.where(low_head, x2, 0.0), axis=-1, keepdims=True)
    hi = jnp.sum(jnp.where(low_head, 0.0, x2), axis=-1, keepdims=True)
    ms = jnp.where(low_head, lo, hi) * (1.0 / HEAD_DIM)
    return x * lax.rsqrt(ms + EPS) * g


def _store_padded(ref, x, low_head, rows):
    sw = pltpu.roll(x, HEAD_DIM, 1)
    zero = jnp.zeros_like(x)
    ref[0, 0, 0, rows] = jnp.where(low_head, x, zero).astype(ref.dtype)
    ref[0, 0, 1, rows] = jnp.where(low_head, zero, sw).astype(ref.dtype)
    ref[0, 1, 0, rows] = jnp.where(low_head, sw, zero).astype(ref.dtype)
    ref[0, 1, 1, rows] = jnp.where(low_head, zero, x).astype(ref.dtype)


def _proj_kernel(shift_ref, x_ref, lat_ref, ctx_ref, ng_ref, w_ref, cos_ref, sin_ref, qg_ref, kg_ref,
                 qw_ref, qt_ref, kw_ref, vw_ref, ka_ref, vt_ref, u_ref, xa_ref, *, n_lat_tiles):
    tm = TOKEN_TILE
    tiles = x_ref.shape[1] // tm
    step = pl.program_id(1)
    lane = lax.broadcasted_iota(jnp.int32, (tm, 2 * HEAD_DIM), 1)
    low_head = lane < HEAD_DIM
    first_half = jnp.bitwise_and(lane, 31) < 16
    first_row = lax.broadcasted_iota(jnp.int32, (HEAD_DIM, tm), 0) == 0
    shift_rows = jnp.where(first_row, shift_ref[0], 0.0)
    ones_rows = jnp.where(lax.broadcasted_iota(jnp.int32, (V_ROWS - HEAD_DIM, tm), 0) == 0, 1.0, 0.0)
    for s in range(tiles):
        rows = slice(tm * s, tm * (s + 1))
        x = x_ref[0, rows]
        is_ctx = step * tiles + s >= n_lat_tiles
        shift, scale = (jnp.where(is_ctx, ctx_ref[3 + v], lat_ref[3 + v]) for v in range(2))
        h = (_rms(x, ng_ref[...]) * (1.0 + scale) + shift).astype(BF16)
        p = _dot(h, w_ref[...])
        cos = cos_ref[rows]
        sin = sin_ref[rows]
        rope = lambda z: _rope(z, cos, sin, first_half)
        k_win = rope(p[:, 0:128])
        v_win = p[:, 128:256]
        u_ref[0, rows] = p[:, 256:512]
        k_glb = rope(_head_rms(p[:, 512:640], kg_ref[...], low_head))
        v_glb = p[:, 640:768]
        zero = jnp.zeros_like(k_glb)
        ka_ref[0, 0, rows] = jnp.where(low_head, k_glb, jnp.where(lane == HEAD_DIM, 1.0, zero)).astype(BF16)
        ka_ref[0, 1, rows] = jnp.where(low_head, jnp.where(lane == 0, 1.0, zero), k_glb).astype(BF16)
        for c in range(2):
            qw = rope(p[:, 768 + 128 * c:896 + 128 * c]) * QK_SCALE
            qw_ref[0, rows, 128 * c:128 * (c + 1)] = qw.astype(BF16)
            qd = rope(_head_rms(p[:, 1024 + 128 * c:1152 + 128 * c], qg_ref[...], low_head)) * (QK_SCALE * LOG2E)
            q_t = qd.T
            for g in range(2):
                own = q_t[HEAD_DIM * g:HEAD_DIM * (g + 1)]
                pair = [own, shift_rows] if c == 0 else [shift_rows, own]
                qt_ref[0, 2 * c + g, :, rows] = jnp.concatenate(pair, axis=0).astype(BF16)
        v_t = v_glb.T
        for j in range(2):
            vt_ref[0, j, :, rows] = jnp.concatenate([v_t[HEAD_DIM * j:HEAD_DIM * (j + 1)], ones_rows],
                                                    axis=0).astype(BF16)
        xa_ref[0, rows] = p[:, 1280:1536]
        _store_padded(kw_ref, k_win, low_head, rows)
        _store_padded(vw_ref, v_win, low_head, rows)


def _proj(x, mod, layer, norm_g, w, cos, sin, q_g, k_g, score_bound, n_lat):
    bsz, t_all, _ = x.shape
    tm = _largest_divisor(t_all // TOKEN_TILE, (5, 4, 2, 1)) * TOKEN_TILE
    tok = lambda width: pl.BlockSpec((1, tm, width), lambda b, t: (b, t, 0))
    pad = pl.BlockSpec((1, 2, 2, tm, 128), lambda b, t: (b, 0, 0, t, 0))
    tab = pl.BlockSpec((tm, 128), lambda b, t: (t, 0))
    pad_shape = jax.ShapeDtypeStruct((bsz, 2, 2, t_all, 128), BF16)
    return pl.pallas_call(
        functools.partial(_proj_kernel, n_lat_tiles=n_lat // TOKEN_TILE),
        grid=(bsz, t_all // tm),
        in_specs=[
            pl.BlockSpec(memory_space=pltpu.SMEM),
            tok(D_MODEL),
            *_mod_rows(layer, bsz),
            _const_spec((1, D_MODEL)),
            pl.BlockSpec((D_MODEL, PROJ_W), lambda b, t: (0, 0), pipeline_mode=pl.Buffered(1)),
            tab, tab,
            _const_spec((1, 128)), _const_spec((1, 128)),
        ],
        out_specs=[
            tok(256),
            pl.BlockSpec((1, 4, 128, tm), lambda b, t: (b, 0, 0, t)),
            pad, pad,
            pl.BlockSpec((1, 2, tm, 128), lambda b, t: (b, 0, t, 0)),
            pl.BlockSpec((1, 2, V_ROWS, tm), lambda b, t: (b, 0, 0, t)),
            tok(256), tok(256),
        ],
        out_shape=[
            jax.ShapeDtypeStruct((bsz, t_all, 256), BF16),
            jax.ShapeDtypeStruct((bsz, 4, 128, t_all), BF16),
            pad_shape, pad_shape,
            jax.ShapeDtypeStruct((bsz, 2, t_all, 128), BF16),
            jax.ShapeDtypeStruct((bsz, 2, V_ROWS, t_all), BF16),
            jax.ShapeDtypeStruct((bsz, t_all, 256), F32),
            jax.ShapeDtypeStruct((bsz, t_all, 256), F32),
        ],
        compiler_params=_cparams(("parallel", "parallel")),
        name="proj",
    )((-score_bound * LOG2E).reshape(1), x, mod, mod, norm_g.reshape(1, D_MODEL), w, cos, sin,
      jnp.tile(q_g, 2).reshape(1, 128), jnp.tile(k_g, 2).reshape(1, 128))


def _win_kernel(sink_ref, q_ref, kp_ref, kc_ref, kn_ref, kx_ref, vp_ref, vc_ref, vn_ref, vx_ref,
                o_ref, *, n_lat_blocks):
    i = pl.program_id(1)
    tile = WIN_BLOCKS * BLOCK
    band = tile + 2 * BLOCK
    tiles = q_ref.shape[1] // tile
    r = lax.broadcasted_iota(jnp.int32, (tile, band), 0)
    c = lax.broadcasted_iota(jnp.int32, (tile, band), 1)
    in_window = (c - r).astype(jnp.uint32) <= 2 * BLOCK

    def band_rows(edge_before, cur, edge_after, sub, j, g):
        first, last = tile * sub - BLOCK, tile * (sub + 1) + BLOCK
        parts = [edge_before[0, j, g]] if sub == 0 else []
        parts.append(cur[0, j, g, max(first, 0):min(last, tile * tiles)])
        if sub == tiles - 1:
            parts.append(edge_after[0, j, g])
        return parts

    for sub in range(tiles):
        rows = slice(tile * sub, tile * (sub + 1))
        blk0 = (i * tiles + sub) * WIN_BLOCKS
        lo = jnp.where(blk0 >= 1, 0, BLOCK)
        hi = jnp.where(blk0 < n_lat_blocks, jnp.minimum((n_lat_blocks - blk0 + 1) * BLOCK, band), 0)
        in_range = (c - lo).astype(jnp.uint32) < (hi - lo).astype(jnp.uint32)
        bias = jnp.where(in_window, jnp.where(in_range, 0.0, NEG), NEG)
        for j in range(2):
            q = q_ref[0, rows, 128 * j:128 * (j + 1)]
            acc = jnp.zeros((tile, 128), F32)
            for g in range(2):
                k_band = jnp.concatenate(band_rows(kp_ref, kc_ref, kn_ref, sub, j, g), axis=0)
                s = jnp.concatenate([_dot_nt(q, k_band) + bias, _dot_nt(q, kx_ref[0, j, g])], axis=1)
                v = jnp.concatenate(band_rows(vp_ref, vc_ref, vn_ref, sub, j, g) + [vx_ref[0, j, g]], axis=0)
                sink = sink_ref[2 * j + g]
                m = jnp.maximum(s.max(-1, keepdims=True), sink)
                p = jnp.exp(s - m)
                den = p.sum(-1, keepdims=True) + jnp.exp(sink - m)
                acc = acc + _dot(p.astype(BF16), v) / den
            o_ref[0, rows, 128 * j:128 * (j + 1)] = acc.astype(o_ref.dtype)


def _window_attention(q, k_pad, v_pad, sink, n_lat, n_ctx, with_ctx):
    bsz, t_all, _ = q.shape
    n_lat_blocks = n_lat // BLOCK
    n_blocks = t_all // BLOCK
    n_q_blocks = n_blocks if with_ctx else n_lat_blocks
    assert n_q_blocks % WIN_BLOCKS == 0 and n_lat_blocks % WIN_BLOCKS == 0
    n_tiles = n_q_blocks // WIN_BLOCKS
    per_step = _largest_divisor(n_tiles, (5, 4, 2, 1))
    step_blocks = per_step * WIN_BLOCKS
    tile = step_blocks * BLOCK
    edge = lambda off: pl.BlockSpec(
        (1, 2, 2, BLOCK, 128), lambda b, i: (b, 0, 0, jnp.clip(i * step_blocks + off, 0, n_blocks - 1), 0))
    cur = pl.BlockSpec((1, 2, 2, tile, 128), lambda b, i: (b, 0, 0, i, 0))
    ctx = pl.BlockSpec((1, 2, 2, n_ctx, 128), lambda b, i: (b, 0, 0, n_lat // n_ctx, 0))
    return pl.pallas_call(
        functools.partial(_win_kernel, n_lat_blocks=n_lat_blocks),
        grid=(bsz, n_tiles // per_step),
        in_specs=[
            pl.BlockSpec(memory_space=pltpu.SMEM),
            pl.BlockSpec((1, tile, 256), lambda b, i: (b, i, 0)),
            edge(-1), cur, edge(step_blocks), ctx,
            edge(-1), cur, edge(step_blocks), ctx,
        ],
        out_specs=pl.BlockSpec((1, tile, 256), lambda b, i: (b, i, 0)),
        out_shape=jax.ShapeDtypeStruct((bsz, n_q_blocks * BLOCK, 256), BF16),
        compiler_params=_cparams(("parallel", "parallel")),
        name="window_attention",
    )(sink, q, k_pad, k_pad, k_pad, k_pad, v_pad, v_pad, v_pad, v_pad)


def _glb_finish(acc_ref, o_ref):
    o_t = jnp.concatenate([acc_ref[g, 0:HEAD_DIM] / acc_ref[g, HEAD_DIM:HEAD_DIM + 1] for g in range(2)],
                          axis=0)
    o_ref[0] = o_t.T.astype(o_ref.dtype)


def _glb_bounded_kernel(qt_ref, k_ref, vt_ref, o_ref, acc_ref, *, tk):
    for ki in range(k_ref.shape[2] // tk):
        keys = slice(ki * tk, (ki + 1) * tk)
        for g in range(2):
            p = jnp.exp2(_dot(k_ref[0, 0, keys], qt_ref[0, g])).astype(BF16)
            pv = _dot(vt_ref[0, 0, :, keys], p)
            acc_ref[g] = pv if ki == 0 else acc_ref[g] + pv
    _glb_finish(acc_ref, o_ref)


def _glb_online_kernel(qt_ref, k_ref, vt_ref, o_ref, acc_ref, m_ref):
    ki = pl.program_id(3)

    @pl.when(ki == 0)
    def _():
        acc_ref[...] = jnp.zeros_like(acc_ref)
        m_ref[...] = jnp.full_like(m_ref, NEG)

    for g in range(2):
        s = _dot(k_ref[0, 0], qt_ref[0, g])
        m_prev = m_ref[g, 0:1]
        m_new = jnp.maximum(m_prev, s.max(axis=0, keepdims=True))
        p = jnp.exp2(s - m_new).astype(BF16)
        acc_ref[g] = jnp.exp2(m_prev - m_new) * acc_ref[g] + _dot(vt_ref[0, 0], p)
        m_ref[g] = jnp.broadcast_to(m_new, m_ref.shape[1:])

    @pl.when(ki == pl.num_programs(3) - 1)
    def _():
        _glb_finish(acc_ref, o_ref)


def _global_attention(qt, k_aug, vt_aug, q_row0, n_q, k_row0, n_k, tq, tk, bounded):
    bsz = qt.shape[0]
    q0, k0 = q_row0 // tq, k_row0 // tk
    scratch = [pltpu.VMEM((2, V_ROWS, tq), F32)]
    if bounded:
        once = dict(pipeline_mode=pl.Buffered(1))
        return pl.pallas_call(
            functools.partial(_glb_bounded_kernel, tk=tk),
            grid=(bsz, 2, n_q // tq),
            in_specs=[
                pl.BlockSpec((1, 2, 128, tq), lambda b, j, qi: (b, j, 0, q0 + qi)),
                pl.BlockSpec((1, 1, n_k, 128), lambda b, j, qi: (b, j, k_row0 // n_k, 0), **once),
                pl.BlockSpec((1, 1, V_ROWS, n_k), lambda b, j, qi: (b, j, 0, k_row0 // n_k), **once),
            ],
            out_specs=pl.BlockSpec((1, tq, 128), lambda b, j, qi: (b, qi, j)),
            out_shape=jax.ShapeDtypeStruct((bsz, n_q, 256), BF16),
            scratch_shapes=scratch,
            compiler_params=_cparams(("parallel", "parallel", "parallel")),
            name="global_attention",
        )(qt, k_aug, vt_aug)
    scratch.append(pltpu.VMEM((2, 8, tq), F32))
    return pl.pallas_call(
        _glb_online_kernel,
        grid=(bsz, 2, n_q // tq, n_k // tk),
        in_specs=[
            pl.BlockSpec((1, 2, 128, tq), lambda b, j, qi, ki: (b, j, 0, q0 + qi)),
            pl.BlockSpec((1, 1, tk, 128), lambda b, j, qi, ki: (b, j, k0 + ki, 0)),
            pl.BlockSpec((1, 1, V_ROWS, tk), lambda b, j, qi, ki: (b, j, 0, k0 + ki)),
        ],
        out_specs=pl.BlockSpec((1, tq, 128), lambda b, j, qi, ki: (b, qi, j)),
        out_shape=jax.ShapeDtypeStruct((bsz, n_q, 256), BF16),
        scratch_shapes=scratch,
        compiler_params=_cparams(("parallel", "parallel", "parallel", "arbitrary")),
        name="global_attention_online",
    )(qt, k_aug, vt_aug)


def _s5_tile(u, bmat_ref, cmat_ref, a_ref, pc_ref, carry_ref, reverse):
    rows = u.shape[0]
    n_groups = rows // SCAN_ROWS
    w = STATE_CHUNK
    ub = u.astype(BF16)
    order = range(n_groups - 1, -1, -1) if reverse else range(n_groups)
    last = 0 if reverse else SCAN_ROWS - 1
    y = None
    for c in range(N_STATE // w):
        bu = _dot(ub, bmat_ref[0, c])
        pr, pi = pc_ref[0, c, :, :w], pc_ref[0, c, :, w:]
        cr, ci = carry_ref[c, :, :w], carry_ref[c, :, w:]
        states = [None] * n_groups
        for g in order:
            vr = bu[SCAN_ROWS * g:SCAN_ROWS * (g + 1), :w]
            vi = bu[SCAN_ROWS * g:SCAN_ROWS * (g + 1), w:]
            for s, k in enumerate((1, 2, 4)):
                ar, ai = a_ref[0, s, c, :, :w], a_ref[0, s, c, :, w:]
                shift = SCAN_ROWS - k if reverse else k
                sr = pltpu.roll(vr, shift, 0)
                si = pltpu.roll(vi, shift, 0)
                vr, vi = vr + ar * sr - ai * si, vi + ar * si + ai * sr
            vr, vi = vr + pr * cr - pi * ci, vi + pr * ci + pi * cr
            cr = jnp.broadcast_to(vr[last:last + 1], (SCAN_ROWS, w))
            ci = jnp.broadcast_to(vi[last:last + 1], (SCAN_ROWS, w))
            states[g] = jnp.concatenate([vr, vi], axis=1)
        carry_ref[c, :, :w] = cr
        carry_ref[c, :, w:] = ci
        part = _dot(jnp.concatenate(states, axis=0).astype(BF16), cmat_ref[0, c])
        y = part if y is None else y + part
    return y


def _s5_kernel(ul_ref, uc_ref, bmat_ref, cmat_ref, a_ref, pc_ref, y_ref, yc_ref, carry_ref):
    direction = pl.program_id(0)
    step = pl.program_id(2)
    tiles = ul_ref.shape[1] // TOKEN_TILE
    for reverse in (False, True):
        @pl.when(direction == int(reverse))
        def _(reverse=reverse):
            tile = functools.partial(_s5_tile, bmat_ref=bmat_ref, cmat_ref=cmat_ref, a_ref=a_ref,
                                     pc_ref=pc_ref, carry_ref=carry_ref, reverse=reverse)

            @pl.when(step == 0)
            def _():
                carry_ref[...] = jnp.zeros_like(carry_ref)
                yc_ref[0, 0] = tile(uc_ref[0])

            for s in (range(tiles - 1, -1, -1) if reverse else range(tiles)):
                rows = slice(TOKEN_TILE * s, TOKEN_TILE * (s + 1))
                y_ref[0, 0, rows] = tile(ul_ref[0, rows])


def _s5_tables(a_re, a_im, log_dt, b_re, b_im, c_re, c_im):
    dt = jnp.exp(log_dt)[..., None]
    re_dt, im_dt = a_re * dt, a_im * dt
    lb_re, lb_im = jnp.exp(re_dt) * jnp.cos(im_dt), jnp.exp(re_dt) * jnp.sin(im_dt)
    den = a_re * a_re + a_im * a_im
    f_re = ((lb_re - 1.0) * a_re + lb_im * a_im) / den
    f_im = (lb_im * a_re - (lb_re - 1.0) * a_im) / den
    bb_re = f_re[..., None] * b_re - f_im[..., None] * b_im
    bb_im = f_re[..., None] * b_im + f_im[..., None] * b_re
    eye = jnp.eye(SSM_GROUPS, dtype=F32)
    in_w, st_w, n_chunks = SSM_GROUPS * SSM_GROUP, N_STATE, N_STATE // STATE_CHUNK
    b_blk = lambda z: jnp.einsum('dgph,gk->dghkp', z, eye).reshape(2, in_w, n_chunks, STATE_CHUNK)
    bmat = jnp.concatenate([b_blk(bb_re), b_blk(bb_im)], axis=-1).transpose(0, 2, 1, 3).astype(BF16)
    c_blk = lambda z: jnp.einsum('dghp,gk->dgpkh', z, eye).reshape(2, n_chunks, STATE_CHUNK, in_w)
    cmat = jnp.concatenate([c_blk(c_re), -c_blk(c_im)], axis=2).astype(BF16)
    steps = jnp.arange(1, SCAN_ROWS + 1, dtype=F32)[None, :, None, None]
    mag = jnp.exp(re_dt[:, None] * steps)
    ang = im_dt[:, None] * steps
    chunked = lambda z: z.reshape(2, SCAN_ROWS, n_chunks, STATE_CHUNK)
    powers = jnp.concatenate([chunked(mag * jnp.cos(ang)), chunked(mag * jnp.sin(ang))], axis=-1)
    row = jnp.arange(SCAN_ROWS)[:, None, None]
    pc = jnp.stack([powers[0], powers[1][::-1]]).transpose(0, 2, 1, 3)
    a_tabs = []
    for k in (1, 2, 4):
        fwd = jnp.where(row >= k, powers[0, k - 1][None], 0.0)
        bwd = jnp.where(row <= SCAN_ROWS - 1 - k, powers[1, k - 1][None], 0.0)
        a_tabs.append(jnp.stack([fwd, bwd]).transpose(0, 2, 1, 3))
    return bmat, cmat, jnp.stack(a_tabs, axis=1), pc


def _s5_scan(u, tables, n_lat, n_ctx):
    bsz, t_all, width = u.shape
    assert n_ctx == TOKEN_TILE
    n_tiles = n_lat // TOKEN_TILE
    per_step = _largest_divisor(n_tiles, (4, 2, 1))
    tm = per_step * TOKEN_TILE
    n_steps = n_tiles // per_step
    bmat, cmat, a_tab, pc = tables
    n_chunks, lanes = N_STATE // STATE_CHUNK, 2 * STATE_CHUNK
    block_of = lambda d, s: s + d * (n_steps - 1 - 2 * s)
    return pl.pallas_call(
        _s5_kernel,
        grid=(2, bsz, n_steps),
        in_specs=[
            pl.BlockSpec((1, tm, width), lambda d, b, s: (b, block_of(d, s), 0)),
            pl.BlockSpec((1, n_ctx, width), lambda d, b, s: (b, n_lat // n_ctx, 0)),
            pl.BlockSpec((1, n_chunks, width, lanes), lambda d, b, s: (d, 0, 0, 0)),
            pl.BlockSpec((1, n_chunks, lanes, width), lambda d, b, s: (d, 0, 0, 0)),
            pl.BlockSpec((1, 3, n_chunks, SCAN_ROWS, lanes), lambda d, b, s: (d, 0, 0, 0, 0)),
            pl.BlockSpec((1, n_chunks, SCAN_ROWS, lanes), lambda d, b, s: (d, 0, 0, 0)),
        ],
        out_specs=[
            pl.BlockSpec((1, 1, tm, width), lambda d, b, s: (d, b, block_of(d, s), 0)),
            pl.BlockSpec((1, 1, n_ctx, width), lambda d, b, s: (d, b, 0, 0)),
        ],
        out_shape=[
            jax.ShapeDtypeStruct((2, bsz, n_lat, width), F32),
            jax.ShapeDtypeStruct((2, bsz, n_ctx, width), F32),
        ],
        scratch_shapes=[pltpu.VMEM((n_chunks, SCAN_ROWS, lanes), F32)],
        compiler_params=_cparams(("parallel", "parallel", "arbitrary")),
        name="s5_scan",
    )(u, u, bmat, cmat, a_tab, pc)


def _gelu_tanh(x):
    c = math.sqrt(2.0 / math.pi)
    return x * (0.5 * (1.0 + jnp.tanh(c * (x + 0.044715 * (x * x * x)))))


def _merge_kernel(x_ref, sh_ref, sc_ref, gt_ref, ng_ref, xa_ref, xap_ref, xan_ref, yb_ref, yf_ref, yr_ref,
                  yft_ref, yrt_ref, u_ref, yd_ref, ydt_ref, wg_ref, pw_ref, ps_ref, dsk_ref, glu_ref, wb_ref,
                  wo_ref, o_ref, ext_ref, *, n_lat_tiles, n_lat, n_ctx):
    t = pl.program_id(1)
    x = x_ref[0]
    tm = x.shape[0]
    bw = BRANCH_W
    h = (_rms(x, ng_ref[...]) * (1.0 + sc_ref[...]) + sh_ref[...]).astype(BF16)
    gates = [jax.nn.sigmoid(_dot(h, wg_ref[:, PROJ_W + D_MODEL * k:PROJ_W + D_MODEL * (k + 1)]))
             for k in range(N_BRANCH)]
    is_ctx = t >= n_lat_tiles
    seg_tile = jnp.where(is_ctx, t - n_lat_tiles, t)
    seg_tiles = jnp.where(is_ctx, n_ctx // tm, n_lat_tiles)
    seg_len = jnp.where(is_ctx, n_ctx, n_lat)
    xa = xa_ref[0]
    halo, span = POOL_HALO, tm + 2 * POOL_HALO
    for k in range(4):
        ext_ref[k, 0:halo, :] = jnp.zeros((halo, bw), F32)
        ext_ref[k, halo + span:, :] = jnp.zeros((halo, bw), F32)
    ext_ref[0, halo:2 * halo, :] = jnp.where(seg_tile > 0, xap_ref[0], 0.0)
    ext_ref[0, 2 * halo:2 * halo + tm, :] = xa
    ext_ref[0, 2 * halo + tm:halo + span, :] = jnp.where(seg_tile + 1 < seg_tiles, xan_ref[0], 0.0)
    wide = lambda k, off: ext_ref[k, halo + off:halo + off + span, :]
    ext_ref[1, halo:halo + span, :] = wide(0, -1) + wide(0, 0)
    ext_ref[2, halo:halo + span, :] = wide(1, -1) + wide(1, 1)
    ext_ref[3, halo:halo + span, :] = wide(2, -2) + wide(2, 2)
    own = lambda k, off: ext_ref[k, 2 * halo + off:2 * halo + off + tm, :]
    sums = [own(1, 0), own(2, 0), own(3, 0), own(3, -4) + own(3, 4)]
    lane = lax.broadcasted_iota(jnp.int32, (tm, bw), 1)
    pos = seg_tile * tm + lax.broadcasted_iota(jnp.int32, (tm, bw), 0)
    group = lane // (bw // len(POOL_WINDOWS))
    half_w = jnp.left_shift(1, group)
    count = jnp.minimum(pos + half_w, seg_len) - jnp.maximum(pos - half_w, 0)
    window_sum = jnp.where(group == 0, sums[0], jnp.where(group == 1, sums[1],
                                                           jnp.where(group == 2, sums[2], sums[3])))
    diff = window_sum / count.astype(F32) - xa
    y_a = _dot(diff.astype(BF16), pw_ref[...]) * ps_ref[...]
    y_f = jnp.where(is_ctx, yft_ref[0, 0], yf_ref[0, 0])
    y_r = jnp.where(is_ctx, yrt_ref[0, 0], yr_ref[0, 0])
    y_s = y_f + y_r + dsk_ref[...] * u_ref[0]
    z = _dot(_gelu_tanh(y_s).astype(BF16), glu_ref[...])
    y_c = z[:, :bw] * jax.nn.sigmoid(z[:, bw:])
    branches = (y_a.astype(BF16), yb_ref[0], y_c.astype(BF16), _tail_rows(yd_ref, ydt_ref, n_lat_tiles))
    total = None
    for gate, y_k, k in zip(gates, branches, range(N_BRANCH)):
        term = gate * _dot(y_k, wb_ref[k])
        total = term if total is None else total + term
    o_ref[0] = x + gt_ref[...] * _dot(total.astype(BF16), wo_ref[...])


def _merge(x, mod, layer, row_of, norm_g, xa, y_b, y_s5, y_s5_tail, u, y_d, y_d_tail, w_gate, pool_bd, pool_scale,
           d_skip, glu_w, branch_w, out_w, n_rows, n_lat, n_ctx):
    bsz = x.shape[0]
    tm = TOKEN_TILE
    assert n_ctx % tm == 0 or n_rows == n_lat
    halo_blocks = xa.shape[1] // POOL_HALO
    per_tile = tm // POOL_HALO
    tok = lambda width: pl.BlockSpec((1, tm, width), lambda b, t: (b, t, 0))
    single = dict(pipeline_mode=pl.Buffered(1))
    return pl.pallas_call(
        functools.partial(_merge_kernel, n_lat_tiles=n_lat // tm, n_lat=n_lat, n_ctx=n_ctx),
        grid=(bsz, n_rows // tm),
        in_specs=[
            tok(D_MODEL),
            _mod_spec(layer, 3, row_of), _mod_spec(layer, 4, row_of), _mod_spec(layer, 5, row_of),
            _const_spec((1, D_MODEL)),
            tok(256),
            pl.BlockSpec((1, POOL_HALO, 256), lambda b, t: (b, jnp.maximum(t * per_tile - 1, 0), 0)),
            pl.BlockSpec((1, POOL_HALO, 256),
                         lambda b, t: (b, jnp.minimum((t + 1) * per_tile, halo_blocks - 1), 0)),
            tok(256),
            *[pl.BlockSpec((1, 1, tm, 256), lambda b, t, d=d: (d, b, jnp.minimum(t, n_lat // tm - 1), 0))
              for d in range(2)],
            *[pl.BlockSpec((1, 1, tm, 256), lambda b, t, d=d: (d, b, 0, 0)) for d in range(2)],
            tok(256), *_tail_specs(tm, 256, n_lat // tm),
            pl.BlockSpec((D_MODEL, PROJ_W + N_BRANCH * D_MODEL), lambda b, t: (0, 0), **single),
            _const_spec((256, 256)), _const_spec((1, 256)), _const_spec((1, 256)),
            _const_spec((256, 512)),
            pl.BlockSpec((N_BRANCH, 256, D_MODEL), lambda b, t: (0, 0, 0), **single),
            pl.BlockSpec((D_MODEL, D_MODEL), lambda b, t: (0, 0), **single),
        ],
        out_specs=tok(D_MODEL),
        out_shape=jax.ShapeDtypeStruct((bsz, n_rows, D_MODEL), F32),
        scratch_shapes=[pltpu.VMEM((4, tm + 4 * POOL_HALO, 256), F32)],
        compiler_params=_cparams(("parallel", "parallel")),
        name="merge",
    )(x, mod, mod, mod, norm_g.reshape(1, D_MODEL), xa, xa, xa, y_b, y_s5, y_s5, y_s5_tail, y_s5_tail, u, y_d, y_d_tail,
      w_gate, pool_bd, pool_scale.reshape(1, 256), d_skip.reshape(1, 256), glu_w, branch_w, out_w)


def _rope_tables(n_lat, n_ctx):
    rows = n_lat // GRID_W
    n_freq = HEAD_DIM // 4
    row = jnp.repeat(jnp.arange(rows), GRID_W)
    col = jnp.tile(jnp.arange(GRID_W), rows)
    inv = ROPE_THETA ** (-jnp.arange(n_freq, dtype=F32) / n_freq)
    ang = jnp.stack([row, col], axis=-1).astype(F32)[..., None] * inv
    cos = jnp.broadcast_to(jnp.cos(ang)[:, :, None, :], (n_lat, 2, 2, n_freq)).reshape(n_lat, HEAD_DIM)
    sign = jnp.array([-1.0, 1.0], F32)[None, None, :, None]
    sin = (jnp.sin(ang)[:, :, None, :] * sign).reshape(n_lat, HEAD_DIM)
    cos = jnp.concatenate([cos, jnp.ones((n_ctx, HEAD_DIM), F32)], axis=0)
    sin = jnp.concatenate([sin, jnp.zeros((n_ctx, HEAD_DIM), F32)], axis=0)
    return jnp.tile(cos, (1, 2)), jnp.tile(sin, (1, 2))


def _largest_divisor(n, candidates):
    for c in candidates:
        if n % c == 0:
            return c
    raise ValueError(f"no tile in {candidates} divides {n}")


def _pool_block_diag(pool_w):
    groups, width, _ = pool_w.shape
    eye = jnp.eye(groups, dtype=pool_w.dtype)
    return jnp.einsum('gcd,gk->gckd', pool_w, eye).reshape(groups * width, groups * width)


def kernel(x, c, ctx, c_ctx, w_mod, b_mod, norm_g, ffn_in, ffn_out, w_in, win_sink, qk_norm, pool_w, pool_scale, ssm_a_re, ssm_a_im, ssm_log_dt, ssm_b_re, ssm_b_im, ssm_c_re, ssm_c_im, ssm_d, glu_w, branch_w, out_w, final_g):
    bsz, n_lat, d = x.shape
    n_ctx = ctx.shape[1]
    depth = w_mod.shape[0]
    t_all = n_lat + n_ctx
    assert d == D_MODEL and bsz < MOD_ROWS and n_lat % TOKEN_TILE == 0 and n_ctx == TOKEN_TILE
    assert n_lat % GRID_W == 0

    cond_rows = jnp.zeros((MOD_ROWS, d), F32).at[:bsz].set(c).at[bsz].set(c_ctx)
    mod = _modulation(cond_rows, w_mod, b_mod)
    n_lat_tiles = n_lat // TOKEN_TILE
    row_of = lambda b, t: jnp.where(t >= n_lat_tiles, bsz, b)

    cos, sin = _rope_tables(n_lat, n_ctx)
    tq = _largest_divisor(n_lat, (1024, 512, 256))
    tk = _largest_divisor(t_all, (1664, 1280, 1024, 768, 512, 256))

    xs = x
    for l in range(depth):
        last = l == depth - 1
        bf = lambda w: w.astype(BF16)
        xs = _ffn(xs, mod, l, 0, norm_g[l, 0], bf(ffn_in[l, 0]), bf(ffn_out[l, 0]), t_all, n_lat,
                  tail=ctx if l == 0 else None)

        score_bound = 1.02 * HEAD_DIM ** 0.5 * jnp.max(jnp.abs(qk_norm[l, 0])) * jnp.max(jnp.abs(qk_norm[l, 1]))
        w_in_l = bf(w_in[l])
        qw, qt, kw, vw, ka, vt, u, xa = _proj(xs, mod, l, norm_g[l, 1], w_in_l, cos, sin,
                                              qk_norm[l, 0], qk_norm[l, 1], score_bound, n_lat)
        y_b = _window_attention(qw, kw, vw, win_sink[l], n_lat, n_ctx, not last)

        def global_attention(q_row0, n_q, k_row0, n_k, tq_, tk_):
            run = lambda bounded: lambda: _global_attention(qt, ka, vt, q_row0, n_q, k_row0, n_k, tq_, tk_,
                                                            bounded)
            return lax.cond(score_bound <= MAX_SCORE_BOUND, run(True), run(False))

        y_d = global_attention(0, n_lat, 0, t_all, tq, tk)
        tables = _s5_tables(ssm_a_re[l], ssm_a_im[l], ssm_log_dt[l], ssm_b_re[l], ssm_b_im[l],
                            ssm_c_re[l], ssm_c_im[l])
        y_s5, y_s5_ctx = _s5_scan(u, tables, n_lat, n_ctx)
        n_rows, y_d_ctx = n_lat, y_d
        if not last:
            y_d_ctx = global_attention(n_lat, n_ctx, n_lat, n_ctx, n_ctx, n_ctx)
            n_rows = t_all
        xs = _merge(xs, mod, l, row_of, norm_g[l, 1], xa, y_b, y_s5, y_s5_ctx, u, y_d, y_d_ctx, w_in_l,
                    bf(_pool_block_diag(pool_w[l])), pool_scale[l], ssm_d[l], bf(glu_w[l]), bf(branch_w[l]),
                    bf(out_w[l]), n_rows, n_lat, n_ctx)
        xs = _ffn(xs, mod, l, 2, norm_g[l, 2], bf(ffn_in[l, 1]), bf(ffn_out[l, 1]), n_rows, n_lat,
                  final_g=final_g if last else None)
    return xs
```

```python
import functools
import math

import jax
import jax.numpy as jnp
import numpy as np
from jax import lax
from jax.experimental import pallas as pl
from jax.experimental.pallas import tpu as pltpu

F32 = jnp.float32
BF16 = jnp.bfloat16

D_MODEL = 1024
D_FF = 2816
N_SUB = 3
BRANCH_W = 256
HEAD_DIM = 64
GRID_W = 64
BLOCK = 128
EPS = 1e-6
ROPE_THETA = 10000.0
N_BRANCH = 4
POOL_WINDOWS = (2, 4, 8, 16)
SSM_GROUPS = 16
SSM_GROUP = 16
SSM_STATE = 64
N_STATE = SSM_GROUPS * SSM_STATE
PROJ_W = 6 * BRANCH_W
MOD_ROWS = 8
POOL_HALO = 8
SCAN_ROWS = 8
STATE_CHUNK = 128
WIN_BLOCKS = 2
NEG = -1e30
QK_SCALE = HEAD_DIM ** -0.5
LOG2E = math.log2(math.e)
V_ROWS = HEAD_DIM + 16
MAX_SCORE_BOUND = 40.0
TOKEN_TILE = 256
VMEM_LIMIT = 56 * 1024 * 1024


def _cparams(sem):
    return pltpu.CompilerParams(dimension_semantics=sem, vmem_limit_bytes=VMEM_LIMIT)


def _dot(a, b):
    return jnp.dot(a, b, preferred_element_type=F32)


def _dot_nt(a, b):
    return lax.dot_general(a, b, (((1,), (1,)), ((), ())), preferred_element_type=F32)


def _rms(x, g):
    ms = jnp.mean(x * x, axis=-1, keepdims=True)
    return x * lax.rsqrt(ms + EPS) * g


def _silu(x):
    return x * jax.nn.sigmoid(x)


def _const_spec(shape):
    nd = len(shape)
    return pl.BlockSpec(shape, lambda *_: (0,) * nd)


def _mod_kernel(s_ref, w_ref, b_ref, o_ref):
    s = _silu(s_ref[...])
    w = w_ref[0]
    s_hi = s.astype(BF16)
    s_lo = (s - s_hi.astype(F32)).astype(BF16)
    w_hi = w.astype(BF16)
    w_lo = (w - w_hi.astype(F32)).astype(BF16)
    o_ref[0] = _dot(s_hi, w_hi) + _dot(s_hi, w_lo) + _dot(s_lo, w_hi) + b_ref[0]


def _modulation(rows, w_mod, b_mod):
    depth, d, width = w_mod.shape
    tn = 1024
    out = pl.pallas_call(
        _mod_kernel,
        grid=(depth, width // tn),
        in_specs=[
            pl.BlockSpec((MOD_ROWS, d), lambda l, j: (0, 0)),
            pl.BlockSpec((1, d, tn), lambda l, j: (l, 0, j)),
            pl.BlockSpec((1, 1, tn), lambda l, j: (l, 0, j)),
        ],
        out_specs=pl.BlockSpec((1, MOD_ROWS, tn), lambda l, j: (l, 0, j)),
        out_shape=jax.ShapeDtypeStruct((depth, MOD_ROWS, width), F32),
        compiler_params=_cparams(("parallel", "parallel")),
        name="modulation",
    )(rows, w_mod, b_mod.reshape(depth, 1, width))
    return out.reshape(depth, MOD_ROWS, N_SUB * 3, 1, d).transpose(0, 2, 1, 3, 4)


def _mod_spec(layer, vec, row_of):
    return pl.BlockSpec((None, None, None, 1, D_MODEL), lambda b, t: (layer, vec, row_of(b, t), 0, 0))


def _mod_rows(layer, bsz):
    spec = lambda row_of: pl.BlockSpec((None, N_SUB * 3, None, 1, D_MODEL),
                                       lambda b, t: (layer, 0, row_of(b), 0, 0))
    return [spec(lambda b: b), spec(lambda b: bsz)]


def _tail_rows(main_ref, tail_ref, n_main_tiles):
    return jnp.where(pl.program_id(1) >= n_main_tiles, tail_ref[0], main_ref[0])


def _ffn_kernel(x_ref, lat_ref, ctx_ref, ng_ref, win_ref, wout_ref, *rest, sub, final, n_main_tiles, n_lat_tiles):
    o_ref = rest[-1]
    tiles = o_ref.shape[1] // TOKEN_TILE
    step = pl.program_id(1)
    for s in range(tiles):
        rows = slice(TOKEN_TILE * s, TOKEN_TILE * (s + 1))
        x = x_ref[0, rows] if n_main_tiles is None else _tail_rows(x_ref, rest[0], n_main_tiles)
        is_ctx = step * tiles + s >= n_lat_tiles
        shift, scale, gate = (jnp.where(is_ctx, ctx_ref[3 * sub + v], lat_ref[3 * sub + v]) for v in range(3))
        h = (_rms(x, ng_ref[...]) * (1.0 + scale) + shift).astype(BF16)
        gu = _dot(h, win_ref[...])
        a = (_silu(gu[:, :D_FF]) * gu[:, D_FF:]).astype(BF16)
        xn = x + (0.5 * gate) * _dot(a, wout_ref[...])
        if final:
            xn = _rms(xn, rest[0][...])
        o_ref[0, rows] = xn


def _tail_specs(tm, width, n_main_tiles):
    return (pl.BlockSpec((1, tm, width), lambda b, t: (b, jnp.minimum(t, n_main_tiles - 1), 0)),
            pl.BlockSpec((1, tm, width), lambda b, t: (b, jnp.maximum(t - n_main_tiles, 0), 0)))


def _ffn(x, mod, layer, sub, norm_g, w_in, w_out, n_rows, n_lat, final_g=None, tail=None):
    bsz = x.shape[0]
    final = final_g is not None
    assert not (final and tail is not None)
    n_tiles = n_rows // TOKEN_TILE
    per_step = 1 if tail is not None else _largest_divisor(n_tiles, (5, 4, 2, 1))
    tm = per_step * TOKEN_TILE
    n_main_tiles = None if tail is None else x.shape[1] // tm
    x_spec = pl.BlockSpec((1, tm, D_MODEL), lambda b, t: (b, t, 0))
    if tail is not None:
        x_spec, tail_spec = _tail_specs(tm, D_MODEL, n_main_tiles)
    in_specs = [
        x_spec,
        *_mod_rows(layer, bsz),
        _const_spec((1, D_MODEL)),
        pl.BlockSpec((D_MODEL, 2 * D_FF), lambda b, t: (0, 0), pipeline_mode=pl.Buffered(1)),
        pl.BlockSpec((D_FF, D_MODEL), lambda b, t: (0, 0), pipeline_mode=pl.Buffered(1)),
    ]
    args = [x, mod, mod, norm_g.reshape(1, D_MODEL), w_in, w_out]
    if final:
        in_specs.append(_const_spec((1, D_MODEL)))
        args.append(final_g.reshape(1, D_MODEL))
    if tail is not None:
        in_specs.append(tail_spec)
        args.append(tail)
    return pl.pallas_call(
        functools.partial(_ffn_kernel, sub=sub, final=final, n_main_tiles=n_main_tiles,
                          n_lat_tiles=n_lat // TOKEN_TILE),
        grid=(bsz, n_tiles // per_step),
        in_specs=in_specs,
        out_specs=pl.BlockSpec((1, tm, D_MODEL), lambda b, t: (b, t, 0)),
        out_shape=jax.ShapeDtypeStruct((bsz, n_rows, D_MODEL), F32),
        compiler_params=_cparams(("parallel", "parallel")),
        name="ffn_final" if final else "ffn",
    )(*args)


def _rope(x, cos, sin, first_half):
    w = x.shape[-1]
    swapped = jnp.where(first_half, pltpu.roll(x, w - 16, 1), pltpu.roll(x, 16, 1))
    return x * cos + swapped * sin


def _head_rms(x, g, low_head):
    x2 = x * x
    lo = jnp.sum(jnp.where(low_head, x2, 0.0), axis=-1, keepdims=True)
    hi = jnp.sum(jnp.where(low_head, 0.0, x2), axis=-1, keepdims=True)
    ms = jnp.where(low_head, lo, hi) * (1.0 / HEAD_DIM)
    return x * lax.rsqrt(ms + EPS) * g


def _store_padded(ref, x, low_head, rows):
    sw = pltpu.roll(x, HEAD_DIM, 1)
    zero = jnp.zeros_like(x)
    ref[0, 0, 0, rows] = jnp.where(low_head, x, zero).astype(ref.dtype)
    ref[0, 0, 1, rows] = jnp.where(low_head, zero, sw).astype(ref.dtype)
    ref[0, 1, 0, rows] = jnp.where(low_head, sw, zero).astype(ref.dtype)
    ref[0, 1, 1, rows] = jnp.where(low_head, zero, x).astype(ref.dtype)


def _proj_kernel(shift_ref, x_ref, lat_ref, ctx_ref, ng_ref, w_ref, cos_ref, sin_ref, qg_ref, kg_ref,
                 qw_ref, qt_ref, kw_ref, vw_ref, ka_ref, vt_ref, u_ref, xa_ref, *, n_lat_tiles):
    tm = TOKEN_TILE
    tiles = x_ref.shape[1] // tm
    step = pl.program_id(1)
    lane = lax.broadcasted_iota(jnp.int32, (tm, 2 * HEAD_DIM), 1)
    low_head = lane < HEAD_DIM
    first_half = jnp.bitwise_and(lane, 31) < 16
    first_row = lax.broadcasted_iota(jnp.int32, (HEAD_DIM, tm), 0) == 0
    shift_rows = jnp.where(first_row, shift_ref[0], 0.0)
    ones_rows = jnp.where(lax.broadcasted_iota(jnp.int32, (V_ROWS - HEAD_DIM, tm), 0) == 0, 1.0, 0.0)
    for s in range(tiles):
        rows = slice(tm * s, tm * (s + 1))
        x = x_ref[0, rows]
        is_ctx = step * tiles + s >= n_lat_tiles
        shift, scale = (jnp.where(is_ctx, ctx_ref[3 + v], lat_ref[3 + v]) for v in range(2))
        h = (_rms(x, ng_ref[...]) * (1.0 + scale) + shift).astype(BF16)
        p = _dot(h, w_ref[...])
        cos = cos_ref[rows]
        sin = sin_ref[rows]
        rope = lambda z: _rope(z, cos, sin, first_half)
        k_win = rope(p[:, 0:128])
        v_win = p[:, 128:256]
        u_ref[0, rows] = p[:, 256:512]
        k_glb = rope(_head_rms(p[:, 512:640], kg_ref[...], low_head))
        v_glb = p[:, 640:768]
        zero = jnp.zeros_like(k_glb)
        ka_ref[0, 0, rows] = jnp.where(low_head, k_glb, jnp.where(lane == HEAD_DIM, 1.0, zero)).astype(BF16)
        ka_ref[0, 1, rows] = jnp.where(low_head, jnp.where(lane == 0, 1.0, zero), k_glb).astype(BF16)
        for c in range(2):
            qw = rope(p[:, 768 + 128 * c:896 + 128 * c]) * QK_SCALE
            qw_ref[0, rows, 128 * c:128 * (c + 1)] = qw.astype(BF16)
            qd = rope(_head_rms(p[:, 1024 + 128 * c:1152 + 128 * c], qg_ref[...], low_head)) * (QK_SCALE * LOG2E)
            q_t = qd.T
            for g in range(2):
                own = q_t[HEAD_DIM * g:HEAD_DIM * (g + 1)]
                pair = [own, shift_rows] if c == 0 else [shift_rows, own]
                qt_ref[0, 2 * c + g, :, rows] = jnp.concatenate(pair, axis=0).astype(BF16)
        v_t = v_glb.T
        for j in range(2):
            vt_ref[0, j, :, rows] = jnp.concatenate([v_t[HEAD_DIM * j:HEAD_DIM * (j + 1)], ones_rows],
                                                    axis=0).astype(BF16)
        xa_ref[0, rows] = p[:, 1280:1536]
        _store_padded(kw_ref, k_win, low_head, rows)
        _store_padded(vw_ref, v_win, low_head, rows)


def _proj(x, mod, layer, norm_g, w, cos, sin, q_g, k_g, score_bound, n_lat):
    bsz, t_all, _ = x.shape
    tm = _largest_divisor(t_all // TOKEN_TILE, (5, 4, 2, 1)) * TOKEN_TILE
    tok = lambda width: pl.BlockSpec((1, tm, width), lambda b, t: (b, t, 0))
    pad = pl.BlockSpec((1, 2, 2, tm, 128), lambda b, t: (b, 0, 0, t, 0))
    tab = pl.BlockSpec((tm, 128), lambda b, t: (t, 0))
    pad_shape = jax.ShapeDtypeStruct((bsz, 2, 2, t_all, 128), BF16)
    return pl.pallas_call(
        functools.partial(_proj_kernel, n_lat_tiles=n_lat // TOKEN_TILE),
        grid=(bsz, t_all // tm),
        in_specs=[
            pl.BlockSpec(memory_space=pltpu.SMEM),
            tok(D_MODEL),
            *_mod_rows(layer, bsz),
            _const_spec((1, D_MODEL)),
            pl.BlockSpec((D_MODEL, PROJ_W), lambda b, t: (0, 0), pipeline_mode=pl.Buffered(1)),
            tab, tab,
            _const_spec((1, 128)), _const_spec((1, 128)),
        ],
        out_specs=[
            tok(256),
            pl.BlockSpec((1, 4, 128, tm), lambda b, t: (b, 0, 0, t)),
            pad, pad,
            pl.BlockSpec((1, 2, tm, 128), lambda b, t: (b, 0, t, 0)),
            pl.BlockSpec((1, 2, V_ROWS, tm), lambda b, t: (b, 0, 0, t)),
            tok(256), tok(256),
        ],
        out_shape=[
            jax.ShapeDtypeStruct((bsz, t_all, 256), BF16),
            jax.ShapeDtypeStruct((bsz, 4, 128, t_all), BF16),
            pad_shape, pad_shape,
            jax.ShapeDtypeStruct((bsz, 2, t_all, 128), BF16),
            jax.ShapeDtypeStruct((bsz, 2, V_ROWS, t_all), BF16),
            jax.ShapeDtypeStruct((bsz, t_all, 256), F32),
            jax.ShapeDtypeStruct((bsz, t_all, 256), F32),
        ],
        compiler_params=_cparams(("parallel", "parallel")),
        name="proj",
    )((-score_bound * LOG2E).reshape(1), x, mod, mod, norm_g.reshape(1, D_MODEL), w, cos, sin,
      jnp.tile(q_g, 2).reshape(1, 128), jnp.tile(k_g, 2).reshape(1, 128))


def _win_kernel(sink_ref, q_ref, kp_ref, kc_ref, kn_ref, kx_ref, vp_ref, vc_ref, vn_ref, vx_ref,
                o_ref, *, n_lat_blocks):
    i = pl.program_id(1)
    tile = WIN_BLOCKS * BLOCK
    band = tile + 2 * BLOCK
    tiles = q_ref.shape[1] // tile
    r = lax.broadcasted_iota(jnp.int32, (tile, band), 0)
    c = lax.broadcasted_iota(jnp.int32, (tile, band), 1)
    in_window = (c - r).astype(jnp.uint32) <= 2 * BLOCK

    def band_rows(edge_before, cur, edge_after, sub, j, g):
        first, last = tile * sub - BLOCK, tile * (sub + 1) + BLOCK
        parts = [edge_before[0, j, g]] if sub == 0 else []
        parts.append(cur[0, j, g, max(first, 0):min(last, tile * tiles)])
        if sub == tiles - 1:
            parts.append(edge_after[0, j, g])
        return parts

    for sub in range(tiles):
        rows = slice(tile * sub, tile * (sub + 1))
        blk0 = (i * tiles + sub) * WIN_BLOCKS
        lo = jnp.where(blk0 >= 1, 0, BLOCK)
        hi = jnp.where(blk0 < n_lat_blocks, jnp.minimum((n_lat_blocks - blk0 + 1) * BLOCK, band), 0)
        in_range = (c - lo).astype(jnp.uint32) < (hi - lo).astype(jnp.uint32)
        bias = jnp.where(in_window, jnp.where(in_range, 0.0, NEG), NEG)
        for j in range(2):
            q = q_ref[0, rows, 128 * j:128 * (j + 1)]
            acc = jnp.zeros((tile, 128), F32)
            for g in range(2):
                k_band = jnp.concatenate(band_rows(kp_ref, kc_ref, kn_ref, sub, j, g), axis=0)
                s = jnp.concatenate([_dot_nt(q, k_band) + bias, _dot_nt(q, kx_ref[0, j, g])], axis=1)
                v = jnp.concatenate(band_rows(vp_ref, vc_ref, vn_ref, sub, j, g) + [vx_ref[0, j, g]], axis=0)
                sink = sink_ref[2 * j + g]
                m = jnp.maximum(s.max(-1, keepdims=True), sink)
                p = jnp.exp(s - m)
                den = p.sum(-1, keepdims=True) + jnp.exp(sink - m)
                acc = acc + _dot(p.astype(BF16), v) / den
            o_ref[0, rows, 128 * j:128 * (j + 1)] = acc.astype(o_ref.dtype)


def _window_attention(q, k_pad, v_pad, sink, n_lat, n_ctx, with_ctx):
    bsz, t_all, _ = q.shape
    n_lat_blocks = n_lat // BLOCK
    n_blocks = t_all // BLOCK
    n_q_blocks = n_blocks if with_ctx else n_lat_blocks
    assert n_q_blocks % WIN_BLOCKS == 0 and n_lat_blocks % WIN_BLOCKS == 0
    n_tiles = n_q_blocks // WIN_BLOCKS
    per_step = _largest_divisor(n_tiles, (5, 4, 2, 1))
    step_blocks = per_step * WIN_BLOCKS
    tile = step_blocks * BLOCK
    edge = lambda off: pl.BlockSpec(
        (1, 2, 2, BLOCK, 128), lambda b, i: (b, 0, 0, jnp.clip(i * step_blocks + off, 0, n_blocks - 1), 0))
    cur = pl.BlockSpec((1, 2, 2, tile, 128), lambda b, i: (b, 0, 0, i, 0))
    ctx = pl.BlockSpec((1, 2, 2, n_ctx, 128), lambda b, i: (b, 0, 0, n_lat // n_ctx, 0))
    return pl.pallas_call(
        functools.partial(_win_kernel, n_lat_blocks=n_lat_blocks),
        grid=(bsz, n_tiles // per_step),
        in_specs=[
            pl.BlockSpec(memory_space=pltpu.SMEM),
            pl.BlockSpec((1, tile, 256), lambda b, i: (b, i, 0)),
            edge(-1), cur, edge(step_blocks), ctx,
            edge(-1), cur, edge(step_blocks), ctx,
        ],
        out_specs=pl.BlockSpec((1, tile, 256), lambda b, i: (b, i, 0)),
        out_shape=jax.ShapeDtypeStruct((bsz, n_q_blocks * BLOCK, 256), BF16),
        compiler_params=_cparams(("parallel", "parallel")),
        name="window_attention",
    )(sink, q, k_pad, k_pad, k_pad, k_pad, v_pad, v_pad, v_pad, v_pad)


def _glb_finish(acc_ref, o_ref):
    o_t = jnp.concatenate([acc_ref[g, 0:HEAD_DIM] / acc_ref[g, HEAD_DIM:HEAD_DIM + 1] for g in range(2)],
                          axis=0)
    o_ref[0] = o_t.T.astype(o_ref.dtype)


def _glb_bounded_kernel(qt_ref, k_ref, vt_ref, o_ref, acc_ref, *, tk):
    for ki in range(k_ref.shape[2] // tk):
        keys = slice(ki * tk, (ki + 1) * tk)
        for g in range(2):
            p = jnp.exp2(_dot(k_ref[0, 0, keys], qt_ref[0, g])).astype(BF16)
            pv = _dot(vt_ref[0, 0, :, keys], p)
            acc_ref[g] = pv if ki == 0 else acc_ref[g] + pv
    _glb_finish(acc_ref, o_ref)


def _glb_online_kernel(qt_ref, k_ref, vt_ref, o_ref, acc_ref, m_ref):
    ki = pl.program_id(3)

    @pl.when(ki == 0)
    def _():
        acc_ref[...] = jnp.zeros_like(acc_ref)
        m_ref[...] = jnp.full_like(m_ref, NEG)

    for g in range(2):
        s = _dot(k_ref[0, 0], qt_ref[0, g])
        m_prev = m_ref[g, 0:1]
        m_new = jnp.maximum(m_prev, s.max(axis=0, keepdims=True))
        p = jnp.exp2(s - m_new).astype(BF16)
        acc_ref[g] = jnp.exp2(m_prev - m_new) * acc_ref[g] + _dot(vt_ref[0, 0], p)
        m_ref[g] = jnp.broadcast_to(m_new, m_ref.shape[1:])

    @pl.when(ki == pl.num_programs(3) - 1)
    def _():
        _glb_finish(acc_ref, o_ref)


def _global_attention(qt, k_aug, vt_aug, q_row0, n_q, k_row0, n_k, tq, tk, bounded):
    bsz = qt.shape[0]
    q0, k0 = q_row0 // tq, k_row0 // tk
    scratch = [pltpu.VMEM((2, V_ROWS, tq), F32)]
    if bounded:
        once = dict(pipeline_mode=pl.Buffered(1))
        return pl.pallas_call(
            functools.partial(_glb_bounded_kernel, tk=tk),
            grid=(bsz, 2, n_q // tq),
            in_specs=[
                pl.BlockSpec((1, 2, 128, tq), lambda b, j, qi: (b, j, 0, q0 + qi)),
                pl.BlockSpec((1, 1, n_k, 128), lambda b, j, qi: (b, j, k_row0 // n_k, 0), **once),
                pl.BlockSpec((1, 1, V_ROWS, n_k), lambda b, j, qi: (b, j, 0, k_row0 // n_k), **once),
            ],
            out_specs=pl.BlockSpec((1, tq, 128), lambda b, j, qi: (b, qi, j)),
            out_shape=jax.ShapeDtypeStruct((bsz, n_q, 256), BF16),
            scratch_shapes=scratch,
            compiler_params=_cparams(("parallel", "parallel", "parallel")),
            name="global_attention",
        )(qt, k_aug, vt_aug)
    scratch.append(pltpu.VMEM((2, 8, tq), F32))
    return pl.pallas_call(
        _glb_online_kernel,
        grid=(bsz, 2, n_q // tq, n_k // tk),
        in_specs=[
            pl.BlockSpec((1, 2, 128, tq), lambda b, j, qi, ki: (b, j, 0, q0 + qi)),
            pl.BlockSpec((1, 1, tk, 128), lambda b, j, qi, ki: (b, j, k0 + ki, 0)),
            pl.BlockSpec((1, 1, V_ROWS, tk), lambda b, j, qi, ki: (b, j, 0, k0 + ki)),
        ],
        out_specs=pl.BlockSpec((1, tq, 128), lambda b, j, qi, ki: (b, qi, j)),
        out_shape=jax.ShapeDtypeStruct((bsz, n_q, 256), BF16),
        scratch_shapes=scratch,
        compiler_params=_cparams(("parallel", "parallel", "parallel", "arbitrary")),
        name="global_attention_online",
    )(qt, k_aug, vt_aug)


def _s5_tile(u, bmat_ref, cmat_ref, a_ref, pc_ref, carry_ref, reverse):
    rows = u.shape[0]
    n_groups = rows // SCAN_ROWS
    w = STATE_CHUNK
    ub = u.astype(BF16)
    order = range(n_groups - 1, -1, -1) if reverse else range(n_groups)
    last = 0 if reverse else SCAN_ROWS - 1
    y = None
    for c in range(N_STATE // w):
        bu = _dot(ub, bmat_ref[0, c])
        pr, pi = pc_ref[0, c, :, :w], pc_ref[0, c, :, w:]
        cr, ci = carry_ref[c, :, :w], carry_ref[c, :, w:]
        states = [None] * n_groups
        for g in order:
            vr = bu[SCAN_ROWS * g:SCAN_ROWS * (g + 1), :w]
            vi = bu[SCAN_ROWS * g:SCAN_ROWS * (g + 1), w:]
            for s, k in enumerate((1, 2, 4)):
                ar, ai = a_ref[0, s, c, :, :w], a_ref[0, s, c, :, w:]
                shift = SCAN_ROWS - k if reverse else k
                sr = pltpu.roll(vr, shift, 0)
                si = pltpu.roll(vi, shift, 0)
                vr, vi = vr + ar * sr - ai * si, vi + ar * si + ai * sr
            vr, vi = vr + pr * cr - pi * ci, vi + pr * ci + pi * cr
            cr = jnp.broadcast_to(vr[last:last + 1], (SCAN_ROWS, w))
            ci = jnp.broadcast_to(vi[last:last + 1], (SCAN_ROWS, w))
            states[g] = jnp.concatenate([vr, vi], axis=1)
        carry_ref[c, :, :w] = cr
        carry_ref[c, :, w:] = ci
        part = _dot(jnp.concatenate(states, axis=0).astype(BF16), cmat_ref[0, c])
        y = part if y is None else y + part
    return y


def _s5_kernel(ul_ref, uc_ref, bmat_ref, cmat_ref, a_ref, pc_ref, y_ref, yc_ref, carry_ref):
    direction = pl.program_id(0)
    step = pl.program_id(2)
    tiles = ul_ref.shape[1] // TOKEN_TILE
    for reverse in (False, True):
        @pl.when(direction == int(reverse))
        def _(reverse=reverse):
            tile = functools.partial(_s5_tile, bmat_ref=bmat_ref, cmat_ref=cmat_ref, a_ref=a_ref,
                                     pc_ref=pc_ref, carry_ref=carry_ref, reverse=reverse)

            @pl.when(step == 0)
            def _():
                carry_ref[...] = jnp.zeros_like(carry_ref)
                yc_ref[0, 0] = tile(uc_ref[0])

            for s in (range(tiles - 1, -1, -1) if reverse else range(tiles)):
                rows = slice(TOKEN_TILE * s, TOKEN_TILE * (s + 1))
                y_ref[0, 0, rows] = tile(ul_ref[0, rows])


def _s5_tables(a_re, a_im, log_dt, b_re, b_im, c_re, c_im):
    dt = jnp.exp(log_dt)[..., None]
    re_dt, im_dt = a_re * dt, a_im * dt
    lb_re, lb_im = jnp.exp(re_dt) * jnp.cos(im_dt), jnp.exp(re_dt) * jnp.sin(im_dt)
    den = a_re * a_re + a_im * a_im
    f_re = ((lb_re - 1.0) * a_re + lb_im * a_im) / den
    f_im = (lb_im * a_re - (lb_re - 1.0) * a_im) / den
    bb_re = f_re[..., None] * b_re - f_im[..., None] * b_im
    bb_im = f_re[..., None] * b_im + f_im[..., None] * b_re
    eye = jnp.eye(SSM_GROUPS, dtype=F32)
    in_w, st_w, n_chunks = SSM_GROUPS * SSM_GROUP, N_STATE, N_STATE // STATE_CHUNK
    b_blk = lambda z: jnp.einsum('dgph,gk->dghkp', z, eye).reshape(2, in_w, n_chunks, STATE_CHUNK)
    bmat = jnp.concatenate([b_blk(bb_re), b_blk(bb_im)], axis=-1).transpose(0, 2, 1, 3).astype(BF16)
    c_blk = lambda z: jnp.einsum('dghp,gk->dgpkh', z, eye).reshape(2, n_chunks, STATE_CHUNK, in_w)
    cmat = jnp.concatenate([c_blk(c_re), -c_blk(c_im)], axis=2).astype(BF16)
    steps = jnp.arange(1, SCAN_ROWS + 1, dtype=F32)[None, :, None, None]
    mag = jnp.exp(re_dt[:, None] * steps)
    ang = im_dt[:, None] * steps
    chunked = lambda z: z.reshape(2, SCAN_ROWS, n_chunks, STATE_CHUNK)
    powers = jnp.concatenate([chunked(mag * jnp.cos(ang)), chunked(mag * jnp.sin(ang))], axis=-1)
    row = jnp.arange(SCAN_ROWS)[:, None, None]
    pc = jnp.stack([powers[0], powers[1][::-1]]).transpose(0, 2, 1, 3)
    a_tabs = []
    for k in (1, 2, 4):
        fwd = jnp.where(row >= k, powers[0, k - 1][None], 0.0)
        bwd = jnp.where(row <= SCAN_ROWS - 1 - k, powers[1, k - 1][None], 0.0)
        a_tabs.append(jnp.stack([fwd, bwd]).transpose(0, 2, 1, 3))
    return bmat, cmat, jnp.stack(a_tabs, axis=1), pc


def _s5_scan(u, tables, n_lat, n_ctx):
    bsz, t_all, width = u.shape
    assert n_ctx == TOKEN_TILE
    n_tiles = n_lat // TOKEN_TILE
    per_step = _largest_divisor(n_tiles, (4, 2, 1))
    tm = per_step * TOKEN_TILE
    n_steps = n_tiles // per_step
    bmat, cmat, a_tab, pc = tables
    n_chunks, lanes = N_STATE // STATE_CHUNK, 2 * STATE_CHUNK
    block_of = lambda d, s: s + d * (n_steps - 1 - 2 * s)
    return pl.pallas_call(
        _s5_kernel,
        grid=(2, bsz, n_steps),
        in_specs=[
            pl.BlockSpec((1, tm, width), lambda d, b, s: (b, block_of(d, s), 0)),
            pl.BlockSpec((1, n_ctx, width), lambda d, b, s: (b, n_lat // n_ctx, 0)),
            pl.BlockSpec((1, n_chunks, width, lanes), lambda d, b, s: (d, 0, 0, 0)),
            pl.BlockSpec((1, n_chunks, lanes, width), lambda d, b, s: (d, 0, 0, 0)),
            pl.BlockSpec((1, 3, n_chunks, SCAN_ROWS, lanes), lambda d, b, s: (d, 0, 0, 0, 0)),
            pl.BlockSpec((1, n_chunks, SCAN_ROWS, lanes), lambda d, b, s: (d, 0, 0, 0)),
        ],
        out_specs=[
            pl.BlockSpec((1, 1, tm, width), lambda d, b, s: (d, b, block_of(d, s), 0)),
            pl.BlockSpec((1, 1, n_ctx, width), lambda d, b, s: (d, b, 0, 0)),
        ],
        out_shape=[
            jax.ShapeDtypeStruct((2, bsz, n_lat, width), F32),
            jax.ShapeDtypeStruct((2, bsz, n_ctx, width), F32),
        ],
        scratch_shapes=[pltpu.VMEM((n_chunks, SCAN_ROWS, lanes), F32)],
        compiler_params=_cparams(("parallel", "parallel", "arbitrary")),
        name="s5_scan",
    )(u, u, bmat, cmat, a_tab, pc)


def _sigmoid_tanh(x):
    return 0.5 * jnp.tanh(0.5 * x) + 0.5


def _gelu_tanh(x):
    c = math.sqrt(2.0 / math.pi)
    return x * (0.5 * (1.0 + jnp.tanh(c * (x + 0.044715 * (x * x * x)))))


def _merge_kernel(x_ref, sh_ref, sc_ref, gt_ref, ng_ref, xa_ref, xap_ref, xan_ref, yb_ref, yf_ref, yr_ref,
                  yft_ref, yrt_ref, u_ref, yd_ref, ydt_ref, wg_ref, pw_ref, ps_ref, dsk_ref, glu_ref, wb_ref,
                  wo_ref, o_ref, ext_ref, *, n_lat_tiles, n_lat, n_ctx):
    t = pl.program_id(1)
    x = x_ref[0]
    tm = x.shape[0]
    bw = BRANCH_W
    h = (_rms(x, ng_ref[...]) * (1.0 + sc_ref[...]) + sh_ref[...]).astype(BF16)
    gates = [_sigmoid_tanh(_dot(h, wg_ref[:, PROJ_W + D_MODEL * k:PROJ_W + D_MODEL * (k + 1)]))
             for k in range(N_BRANCH)]
    is_ctx = t >= n_lat_tiles
    seg_tile = jnp.where(is_ctx, t - n_lat_tiles, t)
    seg_tiles = jnp.where(is_ctx, n_ctx // tm, n_lat_tiles)
    seg_len = jnp.where(is_ctx, n_ctx, n_lat)
    xa = xa_ref[0]
    halo, span = POOL_HALO, tm + 2 * POOL_HALO
    for k in range(4):
        ext_ref[k, 0:halo, :] = jnp.zeros((halo, bw), F32)
        ext_ref[k, halo + span:, :] = jnp.zeros((halo, bw), F32)
    ext_ref[0, halo:2 * halo, :] = jnp.where(seg_tile > 0, xap_ref[0], 0.0)
    ext_ref[0, 2 * halo:2 * halo + tm, :] = xa
    ext_ref[0, 2 * halo + tm:halo + span, :] = jnp.where(seg_tile + 1 < seg_tiles, xan_ref[0], 0.0)
    wide = lambda k, off: ext_ref[k, halo + off:halo + off + span, :]
    ext_ref[1, halo:halo + span, :] = wide(0, -1) + wide(0, 0)
    ext_ref[2, halo:halo + span, :] = wide(1, -1) + wide(1, 1)
    ext_ref[3, halo:halo + span, :] = wide(2, -2) + wide(2, 2)
    own = lambda k, off: ext_ref[k, 2 * halo + off:2 * halo + off + tm, :]
    sums = [own(1, 0), own(2, 0), own(3, 0), own(3, -4) + own(3, 4)]
    lane = lax.broadcasted_iota(jnp.int32, (tm, bw), 1)
    pos = seg_tile * tm + lax.broadcasted_iota(jnp.int32, (tm, bw), 0)
    group = lane // (bw // len(POOL_WINDOWS))
    half_w = jnp.left_shift(1, group)
    count = jnp.minimum(pos + half_w, seg_len) - jnp.maximum(pos - half_w, 0)
    window_sum = jnp.where(group == 0, sums[0], jnp.where(group == 1, sums[1],
                                                           jnp.where(group == 2, sums[2], sums[3])))
    diff = window_sum / count.astype(F32) - xa
    y_a = _dot(diff.astype(BF16), pw_ref[...]) * ps_ref[...]
    y_f = jnp.where(is_ctx, yft_ref[0, 0], yf_ref[0, 0])
    y_r = jnp.where(is_ctx, yrt_ref[0, 0], yr_ref[0, 0])
    y_s = y_f + y_r + dsk_ref[...] * u_ref[0]
    z = _dot(_gelu_tanh(y_s).astype(BF16), glu_ref[...])
    y_c = z[:, :bw] * _sigmoid_tanh(z[:, bw:])
    branches = (y_a.astype(BF16), yb_ref[0], y_c.astype(BF16), _tail_rows(yd_ref, ydt_ref, n_lat_tiles))
    total = None
    for gate, y_k, k in zip(gates, branches, range(N_BRANCH)):
        term = gate * _dot(y_k, wb_ref[k])
        total = term if total is None else total + term
    o_ref[0] = x + gt_ref[...] * _dot(total.astype(BF16), wo_ref[...])


def _merge(x, mod, layer, row_of, norm_g, xa, y_b, y_s5, y_s5_tail, u, y_d, y_d_tail, w_gate, pool_bd, pool_scale,
           d_skip, glu_w, branch_w, out_w, n_rows, n_lat, n_ctx):
    bsz = x.shape[0]
    tm = TOKEN_TILE
    assert n_ctx % tm == 0 or n_rows == n_lat
    halo_blocks = xa.shape[1] // POOL_HALO
    per_tile = tm // POOL_HALO
    tok = lambda width: pl.BlockSpec((1, tm, width), lambda b, t: (b, t, 0))
    single = dict(pipeline_mode=pl.Buffered(1))
    return pl.pallas_call(
        functools.partial(_merge_kernel, n_lat_tiles=n_lat // tm, n_lat=n_lat, n_ctx=n_ctx),
        grid=(bsz, n_rows // tm),
        in_specs=[
            tok(D_MODEL),
            _mod_spec(layer, 3, row_of), _mod_spec(layer, 4, row_of), _mod_spec(layer, 5, row_of),
            _const_spec((1, D_MODEL)),
            tok(256),
            pl.BlockSpec((1, POOL_HALO, 256), lambda b, t: (b, jnp.maximum(t * per_tile - 1, 0), 0)),
            pl.BlockSpec((1, POOL_HALO, 256),
                         lambda b, t: (b, jnp.minimum((t + 1) * per_tile, halo_blocks - 1), 0)),
            tok(256),
            *[pl.BlockSpec((1, 1, tm, 256), lambda b, t, d=d: (d, b, jnp.minimum(t, n_lat // tm - 1), 0))
              for d in range(2)],
            *[pl.BlockSpec((1, 1, tm, 256), lambda b, t, d=d: (d, b, 0, 0)) for d in range(2)],
            tok(256), *_tail_specs(tm, 256, n_lat // tm),
            pl.BlockSpec((D_MODEL, PROJ_W + N_BRANCH * D_MODEL), lambda b, t: (0, 0), **single),
            _const_spec((256, 256)), _const_spec((1, 256)), _const_spec((1, 256)),
            _const_spec((256, 512)),
            pl.BlockSpec((N_BRANCH, 256, D_MODEL), lambda b, t: (0, 0, 0), **single),
            pl.BlockSpec((D_MODEL, D_MODEL), lambda b, t: (0, 0), **single),
        ],
        out_specs=tok(D_MODEL),
        out_shape=jax.ShapeDtypeStruct((bsz, n_rows, D_MODEL), F32),
        scratch_shapes=[pltpu.VMEM((4, tm + 4 * POOL_HALO, 256), F32)],
        compiler_params=_cparams(("parallel", "parallel")),
        name="merge",
    )(x, mod, mod, mod, norm_g.reshape(1, D_MODEL), xa, xa, xa, y_b, y_s5, y_s5, y_s5_tail, y_s5_tail, u, y_d, y_d_tail,
      w_gate, pool_bd, pool_scale.reshape(1, 256), d_skip.reshape(1, 256), glu_w, branch_w, out_w)


def _rope_tables(n_lat, n_ctx):
    rows = n_lat // GRID_W
    n_freq = HEAD_DIM // 4
    row = jnp.repeat(jnp.arange(rows), GRID_W)
    col = jnp.tile(jnp.arange(GRID_W), rows)
    inv = ROPE_THETA ** (-jnp.arange(n_freq, dtype=F32) / n_freq)
    ang = jnp.stack([row, col], axis=-1).astype(F32)[..., None] * inv
    cos = jnp.broadcast_to(jnp.cos(ang)[:, :, None, :], (n_lat, 2, 2, n_freq)).reshape(n_lat, HEAD_DIM)
    sign = jnp.array([-1.0, 1.0], F32)[None, None, :, None]
    sin = (jnp.sin(ang)[:, :, None, :] * sign).reshape(n_lat, HEAD_DIM)
    cos = jnp.concatenate([cos, jnp.ones((n_ctx, HEAD_DIM), F32)], axis=0)
    sin = jnp.concatenate([sin, jnp.zeros((n_ctx, HEAD_DIM), F32)], axis=0)
    return jnp.tile(cos, (1, 2)), jnp.tile(sin, (1, 2))


def _largest_divisor(n, candidates):
    for c in candidates:
        if n % c == 0:
            return c
    raise ValueError(f"no tile in {candidates} divides {n}")


def _pool_block_diag(pool_w):
    groups, width, _ = pool_w.shape
    eye = jnp.eye(groups, dtype=pool_w.dtype)
    return jnp.einsum('gcd,gk->gckd', pool_w, eye).reshape(groups * width, groups * width)


def kernel(x, c, ctx, c_ctx, w_mod, b_mod, norm_g, ffn_in, ffn_out, w_in, win_sink, qk_norm, pool_w, pool_scale, ssm_a_re, ssm_a_im, ssm_log_dt, ssm_b_re, ssm_b_im, ssm_c_re, ssm_c_im, ssm_d, glu_w, branch_w, out_w, final_g):
    bsz, n_lat, d = x.shape
    n_ctx = ctx.shape[1]
    depth = w_mod.shape[0]
    t_all = n_lat + n_ctx
    assert d == D_MODEL and bsz < MOD_ROWS and n_lat % TOKEN_TILE == 0 and n_ctx == TOKEN_TILE
    assert n_lat % GRID_W == 0

    cond_rows = jnp.zeros((MOD_ROWS, d), F32).at[:bsz].set(c).at[bsz].set(c_ctx)
    mod = _modulation(cond_rows, w_mod, b_mod)
    n_lat_tiles = n_lat // TOKEN_TILE
    row_of = lambda b, t: jnp.where(t >= n_lat_tiles, bsz, b)

    cos, sin = _rope_tables(n_lat, n_ctx)
    tq = _largest_divisor(n_lat, (1024, 512, 256))
    tk = _largest_divisor(t_all, (3328, 1280, 1024, 768, 512, 256))

    xs = x
    for l in range(depth):
        last = l == depth - 1
        bf = lambda w: w.astype(BF16)
        xs = _ffn(xs, mod, l, 0, norm_g[l, 0], bf(ffn_in[l, 0]), bf(ffn_out[l, 0]), t_all, n_lat,
                  tail=ctx if l == 0 else None)

        score_bound = 1.02 * HEAD_DIM ** 0.5 * jnp.max(jnp.abs(qk_norm[l, 0])) * jnp.max(jnp.abs(qk_norm[l, 1]))
        w_in_l = bf(w_in[l])
        qw, qt, kw, vw, ka, vt, u, xa = _proj(xs, mod, l, norm_g[l, 1], w_in_l, cos, sin,
                                              qk_norm[l, 0], qk_norm[l, 1], score_bound, n_lat)
        y_b = _window_attention(qw, kw, vw, win_sink[l], n_lat, n_ctx, not last)

        def global_attention(q_row0, n_q, k_row0, n_k, tq_, tk_):
            run = lambda bounded: lambda: _global_attention(qt, ka, vt, q_row0, n_q, k_row0, n_k, tq_, tk_,
                                                            bounded)
            return lax.cond(score_bound <= MAX_SCORE_BOUND, run(True), run(False))

        y_d = global_attention(0, n_lat, 0, t_all, tq, tk)
        tables = _s5_tables(ssm_a_re[l], ssm_a_im[l], ssm_log_dt[l], ssm_b_re[l], ssm_b_im[l],
                            ssm_c_re[l], ssm_c_im[l])
        y_s5, y_s5_ctx = _s5_scan(u, tables, n_lat, n_ctx)
        n_rows, y_d_ctx = n_lat, y_d
        if not last:
            y_d_ctx = global_attention(n_lat, n_ctx, n_lat, n_ctx, n_ctx, n_ctx)
            n_rows = t_all
        xs = _merge(xs, mod, l, row_of, norm_g[l, 1], xa, y_b, y_s5, y_s5_ctx, u, y_d, y_d_ctx, w_in_l,
                    bf(_pool_block_diag(pool_w[l])), pool_scale[l], ssm_d[l], bf(glu_w[l]), bf(branch_w[l]),
                    bf(out_w[l]), n_rows, n_lat, n_ctx)
        xs = _ffn(xs, mod, l, 2, norm_g[l, 2], bf(ffn_in[l, 1]), bf(ffn_out[l, 1]), n_rows, n_lat,
                  final_g=final_g if last else None)
    return xs
```
